```python
import jax, jax.numpy as jnp
from jax import lax
import numpy as np

D_MODEL = 1024
BATCH = 4
SEQ = 8192
DEPTH = 2
DEC_BATCH = 16
DEC_SEQ = 16
PAST_LEN = 2048

CHUNK = 64
Q_BLOCK = 128
N_A = DEPTH // 2
N_B = DEPTH - N_A
RET_HEADS = 4
RET_DK = 256
RET_DV = 512
MLA_HEADS = 8
MLA_NOPE = 128
MLA_ROPE = 64
MLA_V = 128
Q_LORA = 384
KV_LORA = 256
ROPE_THETA = 10000.0
N_GROUPS = 4
EXP_PER_GROUP = 4
N_EXPERTS = N_GROUPS * EXP_PER_GROUP
TOP_K = 2
EXPERT_FF = 512
LN_EPS = 1e-5
RMS_EPS = 1e-6
DN_ALPHA = (2 * DEPTH) ** 0.25
DN_BETA = (8 * DEPTH) ** -0.25
NEG_INF = -1e30

kernel_name = "yoco_retention_mla_hmoe_stream_step"

F32 = jnp.float32


def layer_norm(x, g, b):
    xf = x.astype(F32)
    mu = xf.mean(-1, keepdims=True)
    var = jnp.mean(jnp.square(xf - mu), -1, keepdims=True)
    return ((xf - mu) * lax.rsqrt(var + LN_EPS) * g.astype(F32) + b.astype(F32)).astype(x.dtype)


def rms_norm(x, g):
    xf = x.astype(F32)
    return (xf * lax.rsqrt(jnp.mean(xf * xf, -1, keepdims=True) + RMS_EPS) * g.astype(F32)).astype(x.dtype)


def rope(x, pos):
    half = x.shape[-1] // 2
    inv = ROPE_THETA ** (-jnp.arange(half, dtype=F32) / half)
    ang = pos.astype(F32)[:, None] * inv[None, :]
    if x.ndim == 4:
        ang = ang[:, None, :]
    cos, sin = jnp.cos(ang).astype(x.dtype), jnp.sin(ang).astype(x.dtype)
    x1, x2 = x[..., :half], x[..., half:]
    return jnp.concatenate([x1 * cos - x2 * sin, x1 * sin + x2 * cos], axis=-1)


def retention_scan(q, k, v, s0, chunk):
    b, l = q.shape[:2]
    n = l // chunk
    lg = jnp.log1p(-jnp.exp2(-5.0 - jnp.arange(RET_HEADS, dtype=F32)))
    idx = jnp.arange(chunk, dtype=F32)
    diff = idx[:, None] - idx[None, :]
    dmask = jnp.where(diff >= 0, jnp.exp(lg[:, None, None] * jnp.maximum(diff, 0.0)), 0.0)
    xi = jnp.exp(lg[None, :] * (idx[:, None] + 1.0))[None, :, :, None]
    zeta = jnp.exp(lg[None, :] * (chunk - 1.0 - idx[:, None]))[None, :, :, None]
    cdec = jnp.exp(lg * chunk)[None, :, None, None]

    def to_chunks(t):
        return t.reshape(b, n, chunk, *t.shape[2:]).swapaxes(0, 1)

    def step(s, inp):
        qi, ki, vi = inp
        sc = jnp.einsum('bihd,bjhd->bhij', qi, ki) * dmask
        o = jnp.einsum('bhij,bjhe->bihe', sc, vi) + jnp.einsum('bihd,bhde->bihe', qi, s) * xi
        s = s * cdec + jnp.einsum('bjhd,bjhe->bhde', ki * zeta, vi)
        return s, o

    s, o = lax.scan(step, s0, (to_chunks(q), to_chunks(k), to_chunks(v)))
    return o.swapaxes(0, 1).reshape(b, l, *o.shape[3:]), s


def retention_mixer(h, pos, s0, chunk, w_in, w_out):
    b, l, _ = h.shape
    qd, vd = RET_HEADS * RET_DK, RET_HEADS * RET_DV
    q, k, v, g = jnp.split(h @ w_in, [qd, 2 * qd, 2 * qd + vd], axis=-1)
    q = rope(q.reshape(b, l, RET_HEADS, RET_DK), pos)
    k = rope(k.reshape(b, l, RET_HEADS, RET_DK), pos) * (RET_DK ** -0.5)
    v = v.reshape(b, l, RET_HEADS, RET_DV)
    o, s = retention_scan(q.astype(F32), k.astype(F32), v.astype(F32), s0.astype(F32), chunk)
    mu = o.mean(-1, keepdims=True)
    var = jnp.mean(jnp.square(o - mu), -1, keepdims=True)
    o = ((o - mu) * lax.rsqrt(var + LN_EPS)).reshape(b, l, vd).astype(h.dtype)
    return (jax.nn.silu(g) * o) @ w_out, s.astype(s0.dtype)


def mla_shared_kv(xa, pos, w_dkv, kv_norm_g):
    kv = xa @ w_dkv
    latent = rms_norm(kv[..., :KV_LORA], kv_norm_g)
    k_rope = rope(kv[..., KV_LORA:], pos)
    return latent, k_rope


def mla_expand(latent, w_ukv):
    b, l, _ = latent.shape
    kv = (latent @ w_ukv).reshape(b, l, MLA_HEADS, MLA_NOPE + MLA_V)
    return kv[..., :MLA_NOPE], kv[..., MLA_NOPE:]


def mla_attend(q_nope, q_pe, q_pos, k_nope, k_rope, v, k_pos):
    b, l = q_nope.shape[:2]
    qb = min(Q_BLOCK, l)
    n = l // qb
    scale = (MLA_NOPE + MLA_ROPE) ** -0.5
    key_chunk = k_pos // CHUNK

    def blk(inp):
        qn, qp, qpos = inp
        s = jnp.einsum('bqhd,bkhd->bhqk', qn, k_nope) + jnp.einsum('bqhr,bkr->bhqk', qp, k_rope)
        allowed = key_chunk[None, :] <= (qpos // CHUNK)[:, None]
        s = jnp.where(allowed[None, None], s.astype(F32) * scale, NEG_INF)
        p = jax.nn.softmax(s, axis=-1).astype(v.dtype)
        return jnp.einsum('bhqk,bkhe->bqhe', p, v)

    def split(t):
        return t.reshape(b, n, qb, *t.shape[2:]).swapaxes(0, 1)

    o = lax.map(blk, (split(q_nope), split(q_pe), q_pos.reshape(n, qb)))
    return o.swapaxes(0, 1).reshape(b, l, MLA_HEADS * MLA_V)


def mla_mixer(h, pos, k_nope, k_rope, v, k_pos, w_dq, q_norm_g, w_uq, w_out):
    b, l, _ = h.shape
    q = (rms_norm(h @ w_dq, q_norm_g) @ w_uq).reshape(b, l, MLA_HEADS, MLA_NOPE + MLA_ROPE)
    q_nope, q_pe = q[..., :MLA_NOPE], rope(q[..., MLA_NOPE:], pos)
    return mla_attend(q_nope, q_pe, pos, k_nope, k_rope, v, k_pos) @ w_out


def hier_moe(h, w_rg, b_rg, w_re, b_re, w_gu, w_dn):
    lg = (h @ w_rg + b_rg).astype(F32)
    g_idx = jnp.argmax(lg, axis=-1)
    p_group = jax.nn.softmax(lg, axis=-1).max(-1, keepdims=True)
    le = (h @ w_re + b_re).astype(F32).reshape(*h.shape[:-1], N_GROUPS, EXP_PER_GROUP)
    le = jnp.einsum('blg,blge->ble', jax.nn.one_hot(g_idx, N_GROUPS, dtype=F32), le)
    top_p, top_i = lax.top_k(jax.nn.softmax(le, axis=-1), TOP_K)
    w = top_p / top_p.sum(-1, keepdims=True) * p_group
    eid = g_idx[..., None] * EXP_PER_GROUP + top_i
    gates = jnp.einsum('blk,blke->ble', w, jax.nn.one_hot(eid, N_EXPERTS, dtype=F32)).astype(h.dtype)
    out = jnp.zeros_like(h)
    for e in range(N_EXPERTS):
        a, u = jnp.split(h @ w_gu[e], 2, axis=-1)
        out = out + gates[..., e:e + 1] * ((jax.nn.silu(a) * u) @ w_dn[e])
    return out


def trunk(x, c, pos, ret_s0, past_latent, past_k_rope, ret_chunk,
          w_ada, b_ada, ln_g, ln_b, w_ret_in, w_ret_out, w_dq, q_norm_g, w_uq, w_mla_out,
          w_dkv, kv_norm_g, w_ukv, w_route_group, b_route_group, w_route_expert, b_route_expert,
          w_expert_gate_up, w_expert_down):
    new_ret = []
    latent_new = k_rope_new = None
    k_nope = v = k_rope_all = k_pos = None
    for layer in range(DEPTH):
        mod = (jax.nn.silu(c) @ w_ada[layer] + b_ada[layer])[:, None, :]
        sh1, sc1, g1, sh2, sc2, g2 = jnp.split(mod, 6, axis=-1)
        h = x * (1 + sc1) + sh1
        if layer < N_A:
            y, s = retention_mixer(h, pos, ret_s0[layer], ret_chunk, w_ret_in[layer], w_ret_out[layer])
            new_ret.append(s)
        else:
            if layer == N_A:
                latent_new, k_rope_new = mla_shared_kv(x, pos, w_dkv, kv_norm_g)
                if past_latent is None:
                    lat_all, k_rope_all = latent_new, k_rope_new
                else:
                    lat_all = jnp.concatenate([past_latent, latent_new], axis=1)
                    k_rope_all = jnp.concatenate([past_k_rope, k_rope_new], axis=1)
                k_pos = jnp.arange(lat_all.shape[1], dtype=jnp.int32)
                k_nope, v = mla_expand(lat_all, w_ukv)
            j = layer - N_A
            y = mla_mixer(h, pos, k_nope, k_rope_all, v, k_pos, w_dq[j], q_norm_g[j], w_uq[j], w_mla_out[j])
        x = layer_norm(DN_ALPHA * x + g1 * y, ln_g[layer, 0], ln_b[layer, 0])
        h = x * (1 + sc2) + sh2
        f = hier_moe(h, w_route_group[layer], b_route_group[layer], w_route_expert[layer],
                     b_route_expert[layer], w_expert_gate_up[layer], w_expert_down[layer])
        x = layer_norm(DN_ALPHA * x + g2 * f, ln_g[layer, 1], ln_b[layer, 1])
    return x, jnp.stack(new_ret), latent_new, k_rope_new


def setup_inputs(seed: int = 0) -> dict:
    key = jax.random.key(seed)
    ks = jax.random.split(key, 26)

    def nrm(k, shape, s):
        return jax.random.normal(k, shape, F32) * s

    qk_w, v_w = RET_HEADS * RET_DK, RET_HEADS * RET_DV
    return {
        "x_prompt": nrm(ks[0], (BATCH, SEQ, D_MODEL), 1.0),
        "x_sample": nrm(ks[1], (DEC_BATCH, DEC_SEQ, D_MODEL), 1.0),
        "state_retention": nrm(ks[2], (N_A, DEC_BATCH, RET_HEADS, RET_DK, RET_DV), 0.5),
        "cache_kv_latent": nrm(ks[3], (DEC_BATCH, PAST_LEN, KV_LORA), 1.0),
        "cache_k_rope": nrm(ks[4], (DEC_BATCH, PAST_LEN, MLA_ROPE), 1.0),
        "c_prompt": nrm(ks[5], (BATCH, D_MODEL), 1.0),
        "c_sample": nrm(ks[6], (DEC_BATCH, D_MODEL), 1.0),
        "w_ada": nrm(ks[7], (DEPTH, D_MODEL, 6 * D_MODEL), 0.5 * D_MODEL ** -0.5),
        "b_ada": nrm(ks[8], (DEPTH, 6 * D_MODEL), 0.01),
        "ln_g": 1.0 + nrm(ks[9], (DEPTH, 2, D_MODEL), 0.01),
        "ln_b": nrm(ks[10], (DEPTH, 2, D_MODEL), 0.01),
        "w_ret_in": nrm(ks[11], (N_A, D_MODEL, 2 * qk_w + 2 * v_w), D_MODEL ** -0.5),
        "w_ret_out": nrm(ks[12], (N_A, v_w, D_MODEL), DN_BETA * v_w ** -0.5),
        "w_dq": nrm(ks[13], (N_B, D_MODEL, Q_LORA), D_MODEL ** -0.5),
        "q_norm_g": 1.0 + nrm(ks[14], (N_B, Q_LORA), 0.01),
        "w_uq": nrm(ks[15], (N_B, Q_LORA, MLA_HEADS * (MLA_NOPE + MLA_ROPE)), Q_LORA ** -0.5),
        "w_mla_out": nrm(ks[16], (N_B, MLA_HEADS * MLA_V, D_MODEL), DN_BETA * (MLA_HEADS * MLA_V) ** -0.5),
        "w_dkv": nrm(ks[17], (D_MODEL, KV_LORA + MLA_ROPE), D_MODEL ** -0.5),
        "kv_norm_g": 1.0 + nrm(ks[18], (KV_LORA,), 0.01),
        "w_ukv": nrm(ks[19], (KV_LORA, MLA_HEADS * (MLA_NOPE + MLA_V)), KV_LORA ** -0.5),
        "w_route_group": nrm(ks[20], (DEPTH, D_MODEL, N_GROUPS), D_MODEL ** -0.5),
        "b_route_group": nrm(ks[21], (DEPTH, N_GROUPS), 0.01),
        "w_route_expert": nrm(ks[22], (DEPTH, D_MODEL, N_EXPERTS), D_MODEL ** -0.5),
        "b_route_expert": nrm(ks[23], (DEPTH, N_EXPERTS), 0.01),
        "w_expert_gate_up": nrm(ks[24], (DEPTH, N_EXPERTS, D_MODEL, 2 * EXPERT_FF), D_MODEL ** -0.5),
        "w_expert_down": nrm(ks[25], (DEPTH, N_EXPERTS, EXPERT_FF, D_MODEL), DN_BETA * EXPERT_FF ** -0.5),
    }


def reference(x_prompt, x_sample, state_retention, cache_kv_latent, cache_k_rope, c_prompt, c_sample,
              w_ada, b_ada, ln_g, ln_b, w_ret_in, w_ret_out, w_dq, q_norm_g, w_uq, w_mla_out,
              w_dkv, kv_norm_g, w_ukv, w_route_group, b_route_group, w_route_expert, b_route_expert,
              w_expert_gate_up, w_expert_down):
    weights = (w_ada, b_ada, ln_g, ln_b, w_ret_in, w_ret_out, w_dq, q_norm_g, w_uq, w_mla_out,
               w_dkv, kv_norm_g, w_ukv, w_route_group, b_route_group, w_route_expert, b_route_expert,
               w_expert_gate_up, w_expert_down)
    bp, lp, _ = x_prompt.shape
    pos_p = jnp.arange(lp, dtype=jnp.int32)
    s0_p = jnp.zeros((N_A, bp, RET_HEADS, RET_DK, RET_DV), x_prompt.dtype)
    y_prompt, state_ret_prompt, kv_latent_prompt, k_rope_prompt = trunk(
        x_prompt, c_prompt, pos_p, s0_p, None, None, CHUNK, *weights)
    past = cache_kv_latent.shape[1]
    ls = x_sample.shape[1]
    pos_s = past + jnp.arange(ls, dtype=jnp.int32)
    y_sample, state_ret_sample, kv_latent_sample, k_rope_sample = trunk(
        x_sample, c_sample, pos_s, state_retention, cache_kv_latent, cache_k_rope, ls, *weights)
    return (y_prompt, y_sample, state_ret_prompt, state_ret_sample,
            kv_latent_prompt, k_rope_prompt, kv_latent_sample, k_rope_sample)
```

```python
import functools
import math

import numpy as np
import jax
import jax.numpy as jnp
from jax import lax
from jax.experimental import pallas as pl
from jax.experimental.pallas import tpu as pltpu

F32 = jnp.float32
BF16 = jnp.bfloat16

D_MODEL = 1024
DEPTH = 2
CHUNK = 64
LOG2_CHUNK = 6
N_A = DEPTH // 2
RET_HEADS = 4
RET_DK = 256
RET_DV = 512
MLA_HEADS = 8
MLA_NOPE = 128
MLA_ROPE = 64
MLA_V = 128
Q_LORA = 384
KV_LORA = 256
ROPE_THETA = 10000.0
N_GROUPS = 4
EXP_PER_GROUP = 4
N_EXPERTS = N_GROUPS * EXP_PER_GROUP
EXPERT_FF = 512
LN_EPS = 1e-5
RMS_EPS = 1e-6
DN_ALPHA = (2 * DEPTH) ** 0.25
NEG_INF = -1e30

LANES = 128
ROUTE_W = LANES
TOKEN_TILE = 512
RET_CHUNK = 256
ATTN_TILE = 512
HEAD_W = 2 * LANES
VMEM_LIMIT = 56 * 1024 * 1024


def _cparams(sem, vmem=VMEM_LIMIT):
    return pltpu.CompilerParams(dimension_semantics=sem, vmem_limit_bytes=vmem)


def _silu(x):
    return x * jax.nn.sigmoid(x)


def _layer_norm(z, g, b):
    mu = jnp.mean(z, axis=-1, keepdims=True)
    zc = z - mu
    var = jnp.mean(zc * zc, axis=-1, keepdims=True)
    return zc * lax.rsqrt(var + LN_EPS) * g + b


def _ada_kernel(c_ref, w_ref, b_ref, o_ref):
    s = _silu(c_ref[...]).astype(BF16)
    o_ref[0] = jnp.dot(s, w_ref[0].astype(BF16), preferred_element_type=F32) + b_ref[0]


def _ada(c_all, w_ada, b_ada):
    r = c_all.shape[0]
    n = w_ada.shape[-1]
    tn = 1536
    return pl.pallas_call(
        _ada_kernel,
        grid=(DEPTH, n // tn),
        in_specs=[pl.BlockSpec((r, D_MODEL), lambda l, j: (0, 0)),
                  pl.BlockSpec((1, D_MODEL, tn), lambda l, j: (l, 0, j)),
                  pl.BlockSpec((1, 1, tn), lambda l, j: (l, 0, j))],
        out_specs=pl.BlockSpec((1, r, tn), lambda l, j: (l, 0, j)),
        out_shape=jax.ShapeDtypeStruct((DEPTH, r, n), F32),
        compiler_params=_cparams(("parallel", "parallel")),
        name="ada_mod",
    )(c_all, w_ada, b_ada.reshape(DEPTH, 1, n))


class _Tiles:
    def __init__(self, b, l):
        self.b, self.l, self.t = b, l, b * l
        self.per_token = (l % TOKEN_TILE) != 0
        self.tm = self.t if self.per_token else TOKEN_TILE
        self.n = self.t // self.tm
        self.tiles_per_seq = 1 if self.per_token else l // self.tm

    def mod(self, m):
        if self.per_token:
            return jnp.repeat(m, self.l, axis=0)[None]
        return m[:, None, :]

    def mod_spec(self, arr):
        tps = self.tiles_per_seq
        return pl.BlockSpec((1,) + arr.shape[1:], lambda i: (i // tps, 0, 0))

    def table(self, tab):
        return jnp.tile(tab, (self.b, 1)) if self.per_token else tab

    def table_spec(self, tab):
        nt = tab.shape[0] // self.tm
        return pl.BlockSpec((self.tm, tab.shape[1]), lambda i: (i % nt, 0))

    def rows(self, w):
        return pl.BlockSpec((self.tm, w), lambda i: (i, 0))


def _const_spec(arr):
    nd = arr.ndim
    return pl.BlockSpec(arr.shape, lambda *_: (0,) * nd)


def _ret_in_kernel(x_ref, sc_ref, sh_ref, w_ref, cos_ref, sin_ref, q_ref, k_ref, v_ref, sg_ref):
    h = (x_ref[...] * (1.0 + sc_ref[0]) + sh_ref[0]).astype(BF16)
    cos = cos_ref[...]
    sin = sin_ref[...]
    qk_w = RET_HEADS * RET_DK
    half = RET_DK // 2

    def rope_store(r, out_ref, scale):
        for hh in range(RET_HEADS):
            a = r[:, hh * RET_DK:hh * RET_DK + half]
            b = r[:, hh * RET_DK + half:(hh + 1) * RET_DK]
            out_ref[:, hh * RET_DK:hh * RET_DK + half] = ((a * cos - b * sin) * scale).astype(BF16)
            out_ref[:, hh * RET_DK + half:(hh + 1) * RET_DK] = ((a * sin + b * cos) * scale).astype(BF16)

    r = jnp.dot(h, w_ref[:, 0:qk_w], preferred_element_type=F32)
    rope_store(r, q_ref, 1.0)
    r = jnp.dot(h, w_ref[:, qk_w:2 * qk_w], preferred_element_type=F32)
    rope_store(r, k_ref, RET_DK ** -0.5)
    for c in range(2):
        lo = 2 * qk_w + c * qk_w
        v_ref[:, c * qk_w:(c + 1) * qk_w] = jnp.dot(
            h, w_ref[:, lo:lo + qk_w], preferred_element_type=F32).astype(BF16)
    for c in range(2):
        lo = 4 * qk_w + c * qk_w
        g = jnp.dot(h, w_ref[:, lo:lo + qk_w], preferred_element_type=F32)
        sg_ref[:, c * qk_w:(c + 1) * qk_w] = _silu(g).astype(BF16)


def _ret_in(tl, x, sc, sh, w_in, cos, sin):
    qk_w, v_w = RET_HEADS * RET_DK, RET_HEADS * RET_DV
    return pl.pallas_call(
        _ret_in_kernel,
        grid=(tl.n,),
        in_specs=[tl.rows(D_MODEL), tl.mod_spec(sc), tl.mod_spec(sh), _const_spec(w_in),
                  tl.table_spec(cos), tl.table_spec(sin)],
        out_specs=[tl.rows(qk_w), tl.rows(qk_w), tl.rows(v_w), tl.rows(v_w)],
        out_shape=[jax.ShapeDtypeStruct((tl.t, qk_w), BF16), jax.ShapeDtypeStruct((tl.t, qk_w), BF16),
                   jax.ShapeDtypeStruct((tl.t, v_w), BF16), jax.ShapeDtypeStruct((tl.t, v_w), BF16)],
        compiler_params=_cparams(("parallel",)),
        name="ret_in_proj",
    )(x, sc, sh, w_in, cos, sin)


def _retention_tables(chunk):
    lg = np.log1p(-np.exp2(-5.0 - np.arange(RET_HEADS, dtype=np.float64)))
    idx = np.arange(chunk, dtype=np.float64)
    diff = idx[:, None] - idx[None, :]
    dmask = np.where(diff >= 0, np.exp(lg[:, None, None] * np.maximum(diff, 0.0)), 0.0)
    xi = np.exp(lg[:, None] * (idx[None, :] + 1.0))[:, :, None]
    zeta = np.exp(lg[:, None] * (chunk - 1.0 - idx[None, :]))[:, :, None]
    cdec = np.exp(lg * chunk)
    return (jnp.asarray(dmask, F32), jnp.asarray(xi, F32), jnp.asarray(zeta, F32),
            [float(np.float32(c)) for c in cdec])


def _retention_kernel(q_ref, k_ref, v_ref, sg_ref, s0_ref, dm_ref, xi_ref, zeta_ref,
                      o_ref, sout_ref, s_sc, *, chunk, n_chunks, cdec):
    j = pl.program_id(1)

    @pl.when(j == 0)
    def _():
        s_sc[...] = s0_ref[0]

    for c in range(n_chunks):
        rows = slice(c * chunk, (c + 1) * chunk)
        for hh in range(RET_HEADS):
            kcols = slice(hh * RET_DK, (hh + 1) * RET_DK)
            vcols = slice(hh * RET_DV, (hh + 1) * RET_DV)
            q = q_ref[0, rows, kcols]
            k = k_ref[0, rows, kcols]
            v = v_ref[0, rows, vcols]
            s_old = s_sc[hh]
            sc = lax.dot_general(q, k, (((1,), (1,)), ((), ())), preferred_element_type=F32) * dm_ref[hh]
            o = jnp.dot(sc.astype(BF16), v, preferred_element_type=F32)
            o = o + jnp.dot(q, s_old.astype(BF16), preferred_element_type=F32) * xi_ref[hh]
            kz_t = (k.astype(F32) * zeta_ref[hh]).T.astype(BF16)
            s_sc[hh] = s_old * cdec[hh] + jnp.dot(kz_t, v, preferred_element_type=F32)
            mu = jnp.mean(o, axis=-1, keepdims=True)
            oc = o - mu
            var = jnp.mean(oc * oc, axis=-1, keepdims=True)
            on = oc * lax.rsqrt(var + LN_EPS)
            o_ref[0, rows, vcols] = (sg_ref[0, rows, vcols].astype(F32) * on).astype(BF16)

    @pl.when(j == pl.num_programs(1) - 1)
    def _():
        sout_ref[0] = s_sc[...]


def _retention(b, l, q, k, v, sg, s0, chunk, block):
    qk_w, v_w = RET_HEADS * RET_DK, RET_HEADS * RET_DV
    dmask, xi, zeta, cdec = _retention_tables(chunk)
    kern = functools.partial(_retention_kernel, chunk=chunk, n_chunks=block // chunk, cdec=cdec)
    seq = lambda w: pl.BlockSpec((1, block, w), lambda bi, j: (bi, j, 0))
    st = pl.BlockSpec((1, RET_HEADS, RET_DK, RET_DV), lambda bi, j: (bi, 0, 0, 0))
    return pl.pallas_call(
        kern,
        grid=(b, l // block),
        in_specs=[seq(qk_w), seq(qk_w), seq(v_w), seq(v_w), st,
                  _const_spec(dmask), _const_spec(xi), _const_spec(zeta)],
        out_specs=[seq(v_w), st],
        out_shape=[jax.ShapeDtypeStruct((b, l, v_w), BF16),
                   jax.ShapeDtypeStruct((b, RET_HEADS, RET_DK, RET_DV), F32)],
        scratch_shapes=[pltpu.VMEM((RET_HEADS, RET_DK, RET_DV), F32)],
        compiler_params=_cparams(("parallel", "arbitrary")),
        name="retention_scan",
    )(q.reshape(b, l, qk_w), k.reshape(b, l, qk_w), v.reshape(b, l, v_w), sg.reshape(b, l, v_w),
      s0, dmask, xi, zeta)


def _route_rows(lt):
    g = [lt[i:i + 1] for i in range(N_GROUPS)]
    m = jnp.maximum(jnp.maximum(g[0], g[1]), jnp.maximum(g[2], g[3]))
    gi = jnp.where(g[0] == m, 0, jnp.where(g[1] == m, 1, jnp.where(g[2] == m, 2, 3))).astype(jnp.int32)
    denom = jnp.exp(g[0] - m) + jnp.exp(g[1] - m) + jnp.exp(g[2] - m) + jnp.exp(g[3] - m)
    p_group = 1.0 / denom
    le = []
    for e in range(EXP_PER_GROUP):
        r = [lt[N_GROUPS + gg * EXP_PER_GROUP + e:N_GROUPS + gg * EXP_PER_GROUP + e + 1]
             for gg in range(N_GROUPS)]
        le.append(jnp.where(gi == 0, r[0], jnp.where(gi == 1, r[1], jnp.where(gi == 2, r[2], r[3]))))
    me = jnp.maximum(jnp.maximum(le[0], le[1]), jnp.maximum(le[2], le[3]))
    ex = [jnp.exp(x - me) for x in le]

    def first_argmax(vals):
        mx = jnp.maximum(jnp.maximum(vals[0], vals[1]), jnp.maximum(vals[2], vals[3]))
        ix = jnp.where(vals[0] == mx, 0, jnp.where(vals[1] == mx, 1, jnp.where(vals[2] == mx, 2, 3)))
        return mx, ix.astype(jnp.int32)

    e1, i1 = first_argmax(ex)
    ex2 = [jnp.where(i1 == e, -1.0, ex[e]) for e in range(EXP_PER_GROUP)]
    e2, i2 = first_argmax(ex2)
    tot = e1 + e2
    w1 = e1 / tot * p_group
    w2 = e2 / tot * p_group
    gates = [jnp.where(i1 == e, w1, jnp.where(i2 == e, w2, 0.0)) for e in range(EXP_PER_GROUP)]
    return gi, gates


def _mix_out_kernel(a_ref, w_ref, x_ref, g1_ref, lng_ref, lnb_ref, sc2_ref, sh2_ref,
                    wrh_ref, wrl_ref, br_ref, x1_ref, hx_ref, gi_ref):
    y = jnp.dot(a_ref[...], w_ref[...], preferred_element_type=F32)
    x1 = _layer_norm(DN_ALPHA * x_ref[...] + g1_ref[0] * y, lng_ref[...], lnb_ref[...])
    x1_ref[...] = x1
    h2 = x1 * (1.0 + sc2_ref[0]) + sh2_ref[0]
    hx_ref[:, :D_MODEL] = h2
    hi = h2.astype(BF16)
    lo = (h2 - hi.astype(F32)).astype(BF16)
    wrh = wrh_ref[...]
    logits = (jnp.dot(hi, wrh, preferred_element_type=F32)
              + (jnp.dot(hi, wrl_ref[...], preferred_element_type=F32)
                 + jnp.dot(lo, wrh, preferred_element_type=F32))) + br_ref[...]
    lt = logits.T
    gi, gates = _route_rows(lt)
    gi_ref[0] = gi
    tm = lt.shape[1]
    row = lax.broadcasted_iota(jnp.int32, (8, tm), 0)
    g8 = jnp.where(row == 0, gates[0], jnp.where(row == 1, gates[1], jnp.where(row == 2, gates[2],
                   jnp.where(row == 3, gates[3], 0.0))))
    gt = jnp.concatenate([g8, jnp.zeros((ROUTE_W - 8, tm), F32)], axis=0)
    hx_ref[:, D_MODEL:] = gt.T


def _mix_out(tl, a, w, x, g1, lng, lnb, sc2, sh2, wrh, wrl, br, name):
    kd = a.shape[1]
    return pl.pallas_call(
        _mix_out_kernel,
        grid=(tl.n,),
        in_specs=[tl.rows(kd), _const_spec(w), tl.rows(D_MODEL), tl.mod_spec(g1),
                  _const_spec(lng), _const_spec(lnb), tl.mod_spec(sc2), tl.mod_spec(sh2),
                  _const_spec(wrh), _const_spec(wrl), _const_spec(br)],
        out_specs=[tl.rows(D_MODEL), tl.rows(D_MODEL + ROUTE_W),
                   pl.BlockSpec((1, 1, tl.tm), lambda i: (i, 0, 0))],
        out_shape=[jax.ShapeDtypeStruct((tl.t, D_MODEL), F32),
                   jax.ShapeDtypeStruct((tl.t, D_MODEL + ROUTE_W), F32),
                   jax.ShapeDtypeStruct((tl.n, 1, tl.tm), jnp.int32)],
        compiler_params=_cparams(("parallel",)),
        name=name,
    )(a, w, x, g1, lng, lnb, sc2, sh2, wrh, wrl, br)


def _row_gather_kernel(idx_ref, src_ref, dst_ref, sem, *, rows):
    base = pl.program_id(0) * rows

    def issue(r, carry):
        s = idx_ref[base + r]
        pltpu.make_async_copy(src_ref.at[pl.ds(s, 1)], dst_ref.at[pl.ds(base + r, 1)], sem).start()
        return carry

    lax.fori_loop(0, rows, issue, 0)
    pltpu.make_async_copy(src_ref.at[pl.ds(0, rows)], dst_ref.at[pl.ds(base, rows)], sem).wait()


def _row_gather(src, idx, rows, name):
    n = idx.shape[0]
    kern = functools.partial(_row_gather_kernel, rows=rows)
    return pl.pallas_call(
        kern,
        grid_spec=pltpu.PrefetchScalarGridSpec(
            num_scalar_prefetch=1,
            grid=(n // rows,),
            in_specs=[pl.BlockSpec(memory_space=pl.ANY)],
            out_specs=pl.BlockSpec(memory_space=pl.ANY),
            scratch_shapes=[pltpu.SemaphoreType.DMA(())]),
        out_shape=jax.ShapeDtypeStruct((n, src.shape[1]), src.dtype),
        compiler_params=_cparams(("arbitrary",)),
        name=name,
    )(idx, src)


def _moe_kernel(tg_ref, nv_ref, hx_ref, wgu_ref, wdn_ref, o_ref):
    i = pl.program_id(0)

    @pl.when(i < nv_ref[0])
    def _():
        hb = hx_ref[:, :D_MODEL].astype(BF16)
        acc = None
        for e in range(EXP_PER_GROUP):
            au = jnp.dot(hb, wgu_ref[0, e], preferred_element_type=F32)
            act = (_silu(au[:, :EXPERT_FF]) * au[:, EXPERT_FF:]).astype(BF16)
            y = jnp.dot(act, wdn_ref[0, e], preferred_element_type=F32)
            gy = hx_ref[:, D_MODEL + e:D_MODEL + e + 1] * y
            acc = gy if acc is None else acc + gy
        o_ref[...] = acc

    @pl.when(i >= nv_ref[0])
    def _():
        o_ref[...] = jnp.zeros_like(o_ref)


def _moe_experts(hs, tile_group, n_valid, wgu, wdn, tme):
    n_tiles = hs.shape[0] // tme
    return pl.pallas_call(
        _moe_kernel,
        grid_spec=pltpu.PrefetchScalarGridSpec(
            num_scalar_prefetch=2,
            grid=(n_tiles,),
            in_specs=[pl.BlockSpec((tme, D_MODEL + ROUTE_W), lambda i, tg, nv: (i, 0)),
                      pl.BlockSpec((1, EXP_PER_GROUP, D_MODEL, 2 * EXPERT_FF),
                                   lambda i, tg, nv: (tg[i], 0, 0, 0)),
                      pl.BlockSpec((1, EXP_PER_GROUP, EXPERT_FF, D_MODEL),
                                   lambda i, tg, nv: (tg[i], 0, 0, 0))],
            out_specs=pl.BlockSpec((tme, D_MODEL), lambda i, tg, nv: (i, 0))),
        out_shape=jax.ShapeDtypeStruct((hs.shape[0], D_MODEL), F32),
        compiler_params=_cparams(("arbitrary",)),
        name="moe_experts",
    )(tile_group, n_valid, hs, wgu, wdn)


def _moe(tl, hx, gi, wgu, wdn):
    t = tl.t
    tme = TOKEN_TILE if t % TOKEN_TILE == 0 else 64
    n_tiles = t // tme + N_GROUPS
    onehot = (gi[:, None] == jnp.arange(N_GROUPS, dtype=jnp.int32)[None, :]).astype(jnp.int32)
    counts = jnp.sum(onehot, axis=0)
    rank = jnp.sum((jnp.cumsum(onehot, axis=0) - onehot) * onehot, axis=1)
    padded = ((counts + tme - 1) // tme) * tme
    ends = jnp.cumsum(padded)
    starts = ends - padded
    dest = (starts[gi] + rank).astype(jnp.int32)
    src = jnp.zeros((n_tiles * tme,), jnp.int32).at[dest].set(jnp.arange(t, dtype=jnp.int32))
    tile_start = jnp.arange(n_tiles, dtype=jnp.int32) * tme
    tile_group = jnp.minimum(jnp.sum((tile_start[:, None] >= ends[None, :]).astype(jnp.int32), axis=1),
                             N_GROUPS - 1).astype(jnp.int32)
    n_valid = (ends[-1] // tme).astype(jnp.int32).reshape(1)
    hs = _row_gather(hx, src, tme, "moe_sort_rows")
    fs = _moe_experts(hs, tile_group, n_valid, wgu, wdn, tme)
    return _row_gather(fs, dest, tme, "moe_unsort_rows")


def _ln_res_kernel(x_ref, f_ref, g_ref, lng_ref, lnb_ref, o_ref):
    o_ref[...] = _layer_norm(DN_ALPHA * x_ref[...] + g_ref[0] * f_ref[...], lng_ref[...], lnb_ref[...])


def _ln_res(tl, x, f, g2, lng, lnb):
    return pl.pallas_call(
        _ln_res_kernel,
        grid=(tl.n,),
        in_specs=[tl.rows(D_MODEL), tl.rows(D_MODEL), tl.mod_spec(g2), _const_spec(lng), _const_spec(lnb)],
        out_specs=tl.rows(D_MODEL),
        out_shape=jax.ShapeDtypeStruct((tl.t, D_MODEL), F32),
        compiler_params=_cparams(("parallel",)),
        name="ffn_residual_ln",
    )(x, f, g2, lng, lnb)


def _kv_latent_kernel(x_ref, w_ref, g_ref, cos_ref, sin_ref, lat_ref, kr_ref, krp_ref):
    kv = jnp.dot(x_ref[...].astype(BF16), w_ref[...], preferred_element_type=F32)
    c = kv[:, :KV_LORA]
    lat_ref[...] = c * lax.rsqrt(jnp.mean(c * c, axis=-1, keepdims=True) + RMS_EPS) * g_ref[...]
    kr = kv[:, KV_LORA:KV_LORA + LANES] * cos_ref[...] + kv[:, KV_LORA + LANES:] * sin_ref[...]
    krp_ref[...] = kr
    kr_ref[...] = kr[:, :MLA_ROPE]


def _kv_latent(tl, x, w_dkv_ext, kv_g, cos, sin):
    return pl.pallas_call(
        _kv_latent_kernel,
        grid=(tl.n,),
        in_specs=[tl.rows(D_MODEL), _const_spec(w_dkv_ext), _const_spec(kv_g),
                  tl.table_spec(cos), tl.table_spec(sin)],
        out_specs=[tl.rows(KV_LORA), tl.rows(MLA_ROPE), tl.rows(LANES)],
        out_shape=[jax.ShapeDtypeStruct((tl.t, KV_LORA), F32), jax.ShapeDtypeStruct((tl.t, MLA_ROPE), F32),
                   jax.ShapeDtypeStruct((tl.t, LANES), F32)],
        compiler_params=_cparams(("parallel",)),
        name="mla_kv_latent",
    )(x, w_dkv_ext, kv_g, cos, sin)


def _kv_expand_kernel(lat_ref, krp_ref, w_ref, k_ref, v_ref):
    kvx = jnp.dot(lat_ref[...].astype(BF16), w_ref[...], preferred_element_type=F32)
    krb = krp_ref[...].astype(BF16)
    for hh in range(MLA_HEADS):
        k_ref[:, hh * HEAD_W:hh * HEAD_W + MLA_NOPE] = kvx[:, hh * MLA_NOPE:(hh + 1) * MLA_NOPE].astype(BF16)
        k_ref[:, hh * HEAD_W + MLA_NOPE:(hh + 1) * HEAD_W] = krb
    v_ref[...] = kvx[:, MLA_HEADS * MLA_NOPE:].astype(BF16)


def _kv_expand(lat, krp, w_ukv_r, tm):
    t = lat.shape[0]
    rows = lambda w: pl.BlockSpec((tm, w), lambda i: (i, 0))
    return pl.pallas_call(
        _kv_expand_kernel,
        grid=(t // tm,),
        in_specs=[rows(KV_LORA), rows(LANES), _const_spec(w_ukv_r)],
        out_specs=[rows(MLA_HEADS * HEAD_W), rows(MLA_HEADS * MLA_V)],
        out_shape=[jax.ShapeDtypeStruct((t, MLA_HEADS * HEAD_W), BF16),
                   jax.ShapeDtypeStruct((t, MLA_HEADS * MLA_V), BF16)],
        compiler_params=_cparams(("parallel",)),
        name="mla_kv_expand",
    )(lat, krp, w_ukv_r)


def _q_proj_kernel(x_ref, sc_ref, sh_ref, wdq_ref, g_ref, wuq_ref, cos_ref, sin_ref, q_ref, *, qscale):
    h = (x_ref[...] * (1.0 + sc_ref[0]) + sh_ref[0]).astype(BF16)
    cq = jnp.dot(h, wdq_ref[...], preferred_element_type=F32)
    qn = (cq * lax.rsqrt(jnp.mean(cq * cq, axis=-1, keepdims=True) + RMS_EPS) * g_ref[...]).astype(BF16)
    cos = cos_ref[...]
    sin = sin_ref[...]
    nw = MLA_HEADS * LANES
    qnope = jnp.dot(qn, wuq_ref[:, :nw], preferred_element_type=F32)
    qpe = jnp.dot(qn, wuq_ref[:, nw:2 * nw], preferred_element_type=F32)
    qpe_sw = jnp.dot(qn, wuq_ref[:, 2 * nw:], preferred_element_type=F32)
    for hh in range(MLA_HEADS):
        cols = slice(hh * LANES, (hh + 1) * LANES)
        q_ref[:, hh * HEAD_W:hh * HEAD_W + LANES] = (qnope[:, cols] * qscale).astype(BF16)
        pe = qpe[:, cols] * cos + qpe_sw[:, cols] * sin
        q_ref[:, hh * HEAD_W + LANES:(hh + 1) * HEAD_W] = (pe * qscale).astype(BF16)


def _q_proj(tl, x, sc, sh, w_dq, q_g, w_uq_ext, cos, sin, qscale):
    kern = functools.partial(_q_proj_kernel, qscale=qscale)
    return pl.pallas_call(
        kern,
        grid=(tl.n,),
        in_specs=[tl.rows(D_MODEL), tl.mod_spec(sc), tl.mod_spec(sh), _const_spec(w_dq), _const_spec(q_g),
                  _const_spec(w_uq_ext), tl.table_spec(cos), tl.table_spec(sin)],
        out_specs=tl.rows(MLA_HEADS * HEAD_W),
        out_shape=jax.ShapeDtypeStruct((tl.t, MLA_HEADS * HEAD_W), BF16),
        compiler_params=_cparams(("parallel",)),
        name="mla_q_proj",
    )(x, sc, sh, w_dq, q_g, w_uq_ext, cos, sin)


def _attn_kernel(q_ref, k_ref, v_ref, o_ref, m_sc, l_sc, acc_sc, *, tq, tk, q_pos0, n_keys):
    i = pl.program_id(2)
    q = q_ref[0]
    m_sc[...] = jnp.full(m_sc.shape, NEG_INF, F32)
    l_sc[...] = jnp.zeros(l_sc.shape, F32)
    acc_sc[...] = jnp.zeros(acc_sc.shape, F32)
    row0 = q_pos0 + i * tq
    cq0 = row0 // CHUNK
    cq1 = (row0 + tq - 1) // CHUNK
    n_full = jnp.minimum(((cq0 + 1) * CHUNK) // tk, n_keys // tk)
    n_blk = jnp.minimum(((cq1 + 1) * CHUNK + tk - 1) // tk, (n_keys + tk - 1) // tk)

    def step(j, masked):
        k = k_ref[0, pl.ds(pl.multiple_of(j * tk, tk), tk), :]
        v = v_ref[0, pl.ds(pl.multiple_of(j * tk, tk), tk), :]
        s = lax.dot_general(q, k, (((1,), (1,)), ((), ())), preferred_element_type=F32)
        if masked:
            qpos = row0 + lax.broadcasted_iota(jnp.int32, (tq, 1), 0)
            kpos = j * tk + lax.broadcasted_iota(jnp.int32, (1, tk), 1)
            allowed = ((kpos >> LOG2_CHUNK) <= (qpos >> LOG2_CHUNK)) & (kpos < n_keys)
            s = jnp.where(allowed, s, NEG_INF)
        m_old = m_sc[...]
        m_new = jnp.maximum(m_old, jnp.max(s, axis=1, keepdims=True))
        alpha = jnp.exp2(m_old - m_new)
        p = jnp.exp2(s - m_new)
        l_sc[...] = alpha * l_sc[...] + jnp.sum(p, axis=1, keepdims=True)
        acc_sc[...] = alpha * acc_sc[...] + jnp.dot(p.astype(BF16), v, preferred_element_type=F32)
        m_sc[...] = m_new

    def full_step(j, carry):
        step(j, False)
        return carry

    def masked_step(j, carry):
        step(j, True)
        return carry

    lax.fori_loop(0, n_full, full_step, 0)
    lax.fori_loop(n_full, n_blk, masked_step, 0)
    o_ref[0] = (acc_sc[...] / l_sc[...]).astype(BF16)


def _attention(qc, kc, vv, b, lq, lk, tq, tk, q_pos0, n_keys):
    kern = functools.partial(_attn_kernel, tq=tq, tk=tk, q_pos0=q_pos0, n_keys=n_keys)
    return pl.pallas_call(
        kern,
        grid=(b, MLA_HEADS, lq // tq),
        in_specs=[pl.BlockSpec((1, tq, HEAD_W), lambda bi, h, i: (bi, i, h)),
                  pl.BlockSpec((1, lk, HEAD_W), lambda bi, h, i: (bi, 0, h)),
                  pl.BlockSpec((1, lk, MLA_V), lambda bi, h, i: (bi, 0, h))],
        out_specs=pl.BlockSpec((1, tq, MLA_V), lambda bi, h, i: (bi, i, h)),
        out_shape=jax.ShapeDtypeStruct((b, lq, MLA_HEADS * MLA_V), BF16),
        scratch_shapes=[pltpu.VMEM((tq, 1), F32), pltpu.VMEM((tq, 1), F32), pltpu.VMEM((tq, MLA_V), F32)],
        compiler_params=_cparams(("parallel", "parallel", "arbitrary")),
        name="mla_attention",
    )(qc.reshape(b, lq, MLA_HEADS * HEAD_W), kc.reshape(b, lk, MLA_HEADS * HEAD_W),
      vv.reshape(b, lk, MLA_HEADS * MLA_V))


def _rope_tables(pos, half):
    inv = ROPE_THETA ** (-jnp.arange(half, dtype=F32) / half)
    ang = pos.astype(F32)[:, None] * inv[None, :]
    return jnp.cos(ang), jnp.sin(ang)


def _mla_rope_tables(pos):
    cos, sin = _rope_tables(pos, MLA_ROPE // 2)
    z = jnp.zeros((pos.shape[0], LANES - MLA_ROPE), F32)
    return jnp.concatenate([cos, cos, z], axis=1), jnp.concatenate([-sin, sin, z], axis=1)


def _swap_halves(w):
    half = w.shape[-1] // 2
    return jnp.concatenate([w[..., half:], w[..., :half]], axis=-1)


def _pad_lanes(w):
    return jnp.pad(w, [(0, 0)] * (w.ndim - 1) + [(0, LANES - w.shape[-1])])


def _prep_weights(w_ret_in, w_ret_out, w_dq, q_norm_g, w_uq, w_mla_out, w_dkv, kv_norm_g, w_ukv,
                  w_route_group, b_route_group, w_route_expert, b_route_expert,
                  w_expert_gate_up, w_expert_down):
    p = {}
    p["w_ret_in"] = [w_ret_in[i].astype(BF16) for i in range(N_A)]
    p["w_ret_out"] = [w_ret_out[i].astype(BF16) for i in range(N_A)]
    p["w_dq"] = [w_dq[j].astype(BF16) for j in range(DEPTH - N_A)]
    p["q_norm_g"] = [q_norm_g[j][None, :] for j in range(DEPTH - N_A)]
    w_uq_ext = []
    for j in range(DEPTH - N_A):
        wq = w_uq[j].reshape(Q_LORA, MLA_HEADS, MLA_NOPE + MLA_ROPE)
        nope = wq[:, :, :MLA_NOPE].reshape(Q_LORA, MLA_HEADS * LANES)
        pe = wq[:, :, MLA_NOPE:]
        w_uq_ext.append(jnp.concatenate(
            [nope, _pad_lanes(pe).reshape(Q_LORA, MLA_HEADS * LANES),
             _pad_lanes(_swap_halves(pe)).reshape(Q_LORA, MLA_HEADS * LANES)], axis=1).astype(BF16))
    p["w_uq_ext"] = w_uq_ext
    p["w_mla_out"] = [w_mla_out[j].astype(BF16) for j in range(DEPTH - N_A)]
    kr = w_dkv[:, KV_LORA:]
    p["w_dkv_ext"] = jnp.concatenate([w_dkv[:, :KV_LORA], _pad_lanes(kr), _pad_lanes(_swap_halves(kr))],
                                     axis=1).astype(BF16)
    p["kv_norm_g"] = kv_norm_g[None, :]
    p["w_ukv_r"] = (w_ukv.reshape(KV_LORA, MLA_HEADS, 2, MLA_NOPE).transpose(0, 2, 1, 3)
                    .reshape(KV_LORA, 2 * MLA_HEADS * MLA_NOPE).astype(BF16))
    wr = jnp.concatenate([w_route_group, w_route_expert], axis=-1)
    wr = jnp.pad(wr, ((0, 0), (0, 0), (0, ROUTE_W - wr.shape[-1])))
    wr_hi = wr.astype(BF16)
    wr_lo = (wr - wr_hi.astype(F32)).astype(BF16)
    br = jnp.concatenate([b_route_group, b_route_expert], axis=-1)
    br = jnp.pad(br, ((0, 0), (0, ROUTE_W - br.shape[-1])))
    p["wr_hi"] = [wr_hi[l] for l in range(DEPTH)]
    p["wr_lo"] = [wr_lo[l] for l in range(DEPTH)]
    p["br"] = [br[l][None, :] for l in range(DEPTH)]
    p["w_gu"] = [w_expert_gate_up[l].astype(BF16).reshape(N_GROUPS, EXP_PER_GROUP, D_MODEL, 2 * EXPERT_FF)
                 for l in range(DEPTH)]
    p["w_dn"] = [w_expert_down[l].astype(BF16).reshape(N_GROUPS, EXP_PER_GROUP, EXPERT_FF, D_MODEL)
                 for l in range(DEPTH)]
    return p


def _trunk(x3, mod, pos0, ret_s0, past_latent, past_k_rope, ret_chunk, ret_block, ln_g, ln_b, p):
    b, l, _ = x3.shape
    tl = _Tiles(b, l)
    x = x3.reshape(tl.t, D_MODEL)
    pos = pos0 + jnp.arange(l, dtype=jnp.int32)
    new_ret = []
    latent_new = k_rope_new = None
    kc = vv = None
    lk = n_keys = None
    for layer in range(DEPTH):
        sh1, sc1, g1, sh2, sc2, g2 = [tl.mod(m) for m in jnp.split(mod[layer], 6, axis=-1)]
        lng = [ln_g[layer, s][None, :] for s in range(2)]
        lnb = [ln_b[layer, s][None, :] for s in range(2)]
        if layer < N_A:
            cos, sin = _rope_tables(pos, RET_DK // 2)
            q, k, v, sg = _ret_in(tl, x, sc1, sh1, p["w_ret_in"][layer], tl.table(cos), tl.table(sin))
            a, s_new = _retention(b, l, q, k, v, sg, ret_s0[layer], ret_chunk, ret_block)
            a = a.reshape(tl.t, RET_HEADS * RET_DV)
            new_ret.append(s_new)
            w_out = p["w_ret_out"][layer]
            name = "ret_out_ln_route"
        else:
            j = layer - N_A
            cos, sin = _mla_rope_tables(pos)
            cos, sin = tl.table(cos), tl.table(sin)
            if layer == N_A:
                latent_new, k_rope_new, krp = _kv_latent(tl, x, p["w_dkv_ext"], p["kv_norm_g"], cos, sin)
                lat_all = latent_new.reshape(b, l, KV_LORA)
                krp_all = krp.reshape(b, l, LANES)
                if past_latent is not None:
                    lat_all = jnp.concatenate([past_latent, lat_all], axis=1)
                    krp_all = jnp.concatenate([_pad_lanes(past_k_rope), krp_all], axis=1)
                n_keys = lat_all.shape[1]
                lk = -(-n_keys // LANES) * LANES
                if lk != n_keys:
                    lat_all = jnp.pad(lat_all, ((0, 0), (0, lk - n_keys), (0, 0)))
                    krp_all = jnp.pad(krp_all, ((0, 0), (0, lk - n_keys), (0, 0)))
                tk_rows = TOKEN_TILE if (b * lk) % TOKEN_TILE == 0 else 2 * LANES
                kc, vv = _kv_expand(lat_all.reshape(b * lk, KV_LORA), krp_all.reshape(b * lk, LANES),
                                    p["w_ukv_r"], tk_rows)
            qscale = float((MLA_NOPE + MLA_ROPE) ** -0.5 * math.log2(math.e))
            qc = _q_proj(tl, x, sc1, sh1, p["w_dq"][j], p["q_norm_g"][j], p["w_uq_ext"][j], cos, sin, qscale)
            if l % ATTN_TILE == 0:
                tq, tk = ATTN_TILE, ATTN_TILE
            else:
                tq, tk = l, lk
            a = _attention(qc, kc, vv, b, l, lk, tq, tk, pos0, n_keys).reshape(tl.t, MLA_HEADS * MLA_V)
            w_out = p["w_mla_out"][j]
            name = "mla_out_ln_route"
        x1, hx, gi = _mix_out(tl, a, w_out, x, g1, lng[0], lnb[0], sc2, sh2,
                              p["wr_hi"][layer], p["wr_lo"][layer], p["br"][layer], name)
        f = _moe(tl, hx, gi.reshape(tl.t), p["w_gu"][layer], p["w_dn"][layer])
        x = _ln_res(tl, x1, f, g2, lng[1], lnb[1])
    return (x.reshape(b, l, D_MODEL), jnp.stack(new_ret), latent_new.reshape(b, l, KV_LORA),
            k_rope_new.reshape(b, l, MLA_ROPE))


def kernel(x_prompt, x_sample, state_retention, cache_kv_latent, cache_k_rope, c_prompt, c_sample,
           w_ada, b_ada, ln_g, ln_b, w_ret_in, w_ret_out, w_dq, q_norm_g, w_uq, w_mla_out,
           w_dkv, kv_norm_g, w_ukv, w_route_group, b_route_group, w_route_expert, b_route_expert,
           w_expert_gate_up, w_expert_down):
    bp, lp, _ = x_prompt.shape
    bs, ls, _ = x_sample.shape
    p = _prep_weights(w_ret_in, w_ret_out, w_dq, q_norm_g, w_uq, w_mla_out, w_dkv, kv_norm_g, w_ukv,
                      w_route_group, b_route_group, w_route_expert, b_route_expert,
                      w_expert_gate_up, w_expert_down)
    n_seq = bp + bs
    n_rows = -(-n_seq // 8) * 8
    c_all = jnp.concatenate([c_prompt, c_sample, jnp.zeros((n_rows - n_seq, D_MODEL), F32)], axis=0)
    mod = _ada(c_all, w_ada, b_ada)
    s0_p = jnp.zeros((N_A, bp, RET_HEADS, RET_DK, RET_DV), F32)
    ret_chunk_p = RET_CHUNK if lp % RET_CHUNK == 0 else CHUNK
    y_p, st_p, lat_p, kr_p = _trunk(x_prompt, mod[:, :bp], 0, s0_p, None, None,
                                    ret_chunk_p, max(ret_chunk_p, min(lp, TOKEN_TILE)), ln_g, ln_b, p)
    past = cache_kv_latent.shape[1]
    y_s, st_s, lat_s, kr_s = _trunk(x_sample, mod[:, bp:n_seq], past, state_retention,
                                    cache_kv_latent, cache_k_rope, ls, ls, ln_g, ln_b, p)
    return (y_p, y_s, st_p, st_s, lat_p, kr_p, lat_s, kr_s)
```

```python
import functools
import math

import numpy as np
import jax
import jax.numpy as jnp
from jax import lax
from jax.experimental import pallas as pl
from jax.experimental.pallas import tpu as pltpu

F32 = jnp.float32
BF16 = jnp.bfloat16

D_MODEL = 1024
DEPTH = 2
CHUNK = 64
LOG2_CHUNK = 6
N_A = DEPTH // 2
RET_HEADS = 4
RET_DK = 256
RET_DV = 512
MLA_HEADS = 8
MLA_NOPE = 128
MLA_ROPE = 64
MLA_V = 128
Q_LORA = 384
KV_LORA = 256
ROPE_THETA = 10000.0
N_GROUPS = 4
EXP_PER_GROUP = 4
N_EXPERTS = N_GROUPS * EXP_PER_GROUP
EXPERT_FF = 512
LN_EPS = 1e-5
RMS_EPS = 1e-6
DN_ALPHA = (2 * DEPTH) ** 0.25
NEG_INF = -1e30

LANES = 128
ROUTE_W = LANES
TOKEN_TILE = 512
RET_CHUNK = 256
ATTN_TILE = 512
HEAD_W = 2 * LANES
VMEM_LIMIT = 56 * 1024 * 1024


def _cparams(sem, vmem=VMEM_LIMIT):
    return pltpu.CompilerParams(dimension_semantics=sem, vmem_limit_bytes=vmem)


def _silu(x):
    return x * jax.nn.sigmoid(x)


def _layer_norm(z, g, b):
    mu = jnp.mean(z, axis=-1, keepdims=True)
    zc = z - mu
    var = jnp.mean(zc * zc, axis=-1, keepdims=True)
    return zc * lax.rsqrt(var + LN_EPS) * g + b


def _ada_kernel(c_ref, w_ref, b_ref, o_ref):
    s = _silu(c_ref[...]).astype(BF16)
    o_ref[0] = jnp.dot(s, w_ref[0].astype(BF16), preferred_element_type=F32) + b_ref[0]


def _ada(c_all, w_ada, b_ada):
    r = c_all.shape[0]
    n = w_ada.shape[-1]
    tn = 1536
    return pl.pallas_call(
        _ada_kernel,
        grid=(DEPTH, n // tn),
        in_specs=[pl.BlockSpec((r, D_MODEL), lambda l, j: (0, 0)),
                  pl.BlockSpec((1, D_MODEL, tn), lambda l, j: (l, 0, j)),
                  pl.BlockSpec((1, 1, tn), lambda l, j: (l, 0, j))],
        out_specs=pl.BlockSpec((1, r, tn), lambda l, j: (l, 0, j)),
        out_shape=jax.ShapeDtypeStruct((DEPTH, r, n), F32),
        compiler_params=_cparams(("parallel", "parallel")),
        name="ada_mod",
    )(c_all, w_ada, b_ada.reshape(DEPTH, 1, n))


class _Tiles:
    def __init__(self, b, l):
        self.b, self.l, self.t = b, l, b * l
        self.per_token = (l % TOKEN_TILE) != 0
        self.tm = self.t if self.per_token else TOKEN_TILE
        self.n = self.t // self.tm
        self.tiles_per_seq = 1 if self.per_token else l // self.tm

    def mod(self, m):
        if self.per_token:
            return jnp.repeat(m, self.l, axis=0)[None]
        return m[:, None, :]

    def mod_spec(self, arr):
        tps = self.tiles_per_seq
        return pl.BlockSpec((1,) + arr.shape[1:], lambda i: (i // tps, 0, 0))

    def table(self, tab):
        return jnp.tile(tab, (self.b, 1)) if self.per_token else tab

    def table_spec(self, tab):
        nt = tab.shape[0] // self.tm
        return pl.BlockSpec((self.tm, tab.shape[1]), lambda i: (i % nt, 0))

    def rows(self, w):
        return pl.BlockSpec((self.tm, w), lambda i: (i, 0))


def _const_spec(arr):
    nd = arr.ndim
    return pl.BlockSpec(arr.shape, lambda *_: (0,) * nd)


def _ret_in_kernel(x_ref, sc_ref, sh_ref, w_ref, cos_ref, sin_ref, q_ref, k_ref, v_ref, sg_ref):
    h = (x_ref[...] * (1.0 + sc_ref[0]) + sh_ref[0]).astype(BF16)
    cos = cos_ref[...]
    sin = sin_ref[...]
    qk_w = RET_HEADS * RET_DK
    half = RET_DK // 2

    def rope_store(r, out_ref, scale):
        for hh in range(RET_HEADS):
            a = r[:, hh * RET_DK:hh * RET_DK + half]
            b = r[:, hh * RET_DK + half:(hh + 1) * RET_DK]
            out_ref[:, hh * RET_DK:hh * RET_DK + half] = ((a * cos - b * sin) * scale).astype(BF16)
            out_ref[:, hh * RET_DK + half:(hh + 1) * RET_DK] = ((a * sin + b * cos) * scale).astype(BF16)

    r = jnp.dot(h, w_ref[:, 0:qk_w], preferred_element_type=F32)
    rope_store(r, q_ref, 1.0)
    r = jnp.dot(h, w_ref[:, qk_w:2 * qk_w], preferred_element_type=F32)
    rope_store(r, k_ref, RET_DK ** -0.5)
    for c in range(2):
        lo = 2 * qk_w + c * qk_w
        v_ref[:, c * qk_w:(c + 1) * qk_w] = jnp.dot(
            h, w_ref[:, lo:lo + qk_w], preferred_element_type=F32).astype(BF16)
    for c in range(2):
        lo = 4 * qk_w + c * qk_w
        g = jnp.dot(h, w_ref[:, lo:lo + qk_w], preferred_element_type=F32)
        sg_ref[:, c * qk_w:(c + 1) * qk_w] = _silu(g).astype(BF16)


def _ret_in(tl, x, sc, sh, w_in, cos, sin):
    qk_w, v_w = RET_HEADS * RET_DK, RET_HEADS * RET_DV
    return pl.pallas_call(
        _ret_in_kernel,
        grid=(tl.n,),
        in_specs=[tl.rows(D_MODEL), tl.mod_spec(sc), tl.mod_spec(sh), _const_spec(w_in),
                  tl.table_spec(cos), tl.table_spec(sin)],
        out_specs=[tl.rows(qk_w), tl.rows(qk_w), tl.rows(v_w), tl.rows(v_w)],
        out_shape=[jax.ShapeDtypeStruct((tl.t, qk_w), BF16), jax.ShapeDtypeStruct((tl.t, qk_w), BF16),
                   jax.ShapeDtypeStruct((tl.t, v_w), BF16), jax.ShapeDtypeStruct((tl.t, v_w), BF16)],
        compiler_params=_cparams(("parallel",)),
        name="ret_in_proj",
    )(x, sc, sh, w_in, cos, sin)


def _retention_tables(chunk):
    lg = np.log1p(-np.exp2(-5.0 - np.arange(RET_HEADS, dtype=np.float64)))
    idx = np.arange(chunk, dtype=np.float64)
    diff = idx[:, None] - idx[None, :]
    dmask = np.where(diff >= 0, np.exp(lg[:, None, None] * np.maximum(diff, 0.0)), 0.0)
    xi = np.exp(lg[:, None] * (idx[None, :] + 1.0))[:, :, None]
    zeta = np.exp(lg[:, None] * (chunk - 1.0 - idx[None, :]))[:, :, None]
    cdec = np.exp(lg * chunk)
    return (jnp.asarray(dmask, F32), jnp.asarray(xi, F32), jnp.asarray(zeta, F32),
            [float(np.float32(c)) for c in cdec])


def _retention_kernel(q_ref, k_ref, v_ref, sg_ref, s0_ref, dm_ref, xi_ref, zeta_ref,
                      o_ref, sout_ref, s_sc, *, chunk, n_chunks, cdec):
    j = pl.program_id(1)

    @pl.when(j == 0)
    def _():
        s_sc[...] = s0_ref[0]

    for c in range(n_chunks):
        rows = slice(c * chunk, (c + 1) * chunk)
        for hh in range(RET_HEADS):
            kcols = slice(hh * RET_DK, (hh + 1) * RET_DK)
            vcols = slice(hh * RET_DV, (hh + 1) * RET_DV)
            q = q_ref[0, rows, kcols]
            k = k_ref[0, rows, kcols]
            v = v_ref[0, rows, vcols]
            s_old = s_sc[hh]
            sc = lax.dot_general(q, k, (((1,), (1,)), ((), ())), preferred_element_type=F32) * dm_ref[hh]
            o = jnp.dot(sc.astype(BF16), v, preferred_element_type=F32)
            o = o + jnp.dot(q, s_old.astype(BF16), preferred_element_type=F32) * xi_ref[hh]
            kz_t = (k.astype(F32) * zeta_ref[hh]).T.astype(BF16)
            s_sc[hh] = s_old * cdec[hh] + jnp.dot(kz_t, v, preferred_element_type=F32)
            mu = jnp.mean(o, axis=-1, keepdims=True)
            oc = o - mu
            var = jnp.mean(oc * oc, axis=-1, keepdims=True)
            on = oc * lax.rsqrt(var + LN_EPS)
            o_ref[0, rows, vcols] = (sg_ref[0, rows, vcols].astype(F32) * on).astype(BF16)

    @pl.when(j == pl.num_programs(1) - 1)
    def _():
        sout_ref[0] = s_sc[...]


def _retention(b, l, q, k, v, sg, s0, chunk, block):
    qk_w, v_w = RET_HEADS * RET_DK, RET_HEADS * RET_DV
    dmask, xi, zeta, cdec = _retention_tables(chunk)
    kern = functools.partial(_retention_kernel, chunk=chunk, n_chunks=block // chunk, cdec=cdec)
    seq = lambda w: pl.BlockSpec((1, block, w), lambda bi, j: (bi, j, 0))
    st = pl.BlockSpec((1, RET_HEADS, RET_DK, RET_DV), lambda bi, j: (bi, 0, 0, 0))
    return pl.pallas_call(
        kern,
        grid=(b, l // block),
        in_specs=[seq(qk_w), seq(qk_w), seq(v_w), seq(v_w), st,
                  _const_spec(dmask), _const_spec(xi), _const_spec(zeta)],
        out_specs=[seq(v_w), st],
        out_shape=[jax.ShapeDtypeStruct((b, l, v_w), BF16),
                   jax.ShapeDtypeStruct((b, RET_HEADS, RET_DK, RET_DV), F32)],
        scratch_shapes=[pltpu.VMEM((RET_HEADS, RET_DK, RET_DV), F32)],
        compiler_params=_cparams(("parallel", "arbitrary")),
        name="retention_scan",
    )(q.reshape(b, l, qk_w), k.reshape(b, l, qk_w), v.reshape(b, l, v_w), sg.reshape(b, l, v_w),
      s0, dmask, xi, zeta)


def _route_rows(lt):
    g = [lt[i:i + 1] for i in range(N_GROUPS)]
    m = jnp.maximum(jnp.maximum(g[0], g[1]), jnp.maximum(g[2], g[3]))
    gi = jnp.where(g[0] == m, 0, jnp.where(g[1] == m, 1, jnp.where(g[2] == m, 2, 3))).astype(jnp.int32)
    denom = jnp.exp(g[0] - m) + jnp.exp(g[1] - m) + jnp.exp(g[2] - m) + jnp.exp(g[3] - m)
    p_group = 1.0 / denom
    le = []
    for e in range(EXP_PER_GROUP):
        r = [lt[N_GROUPS + gg * EXP_PER_GROUP + e:N_GROUPS + gg * EXP_PER_GROUP + e + 1]
             for gg in range(N_GROUPS)]
        le.append(jnp.where(gi == 0, r[0], jnp.where(gi == 1, r[1], jnp.where(gi == 2, r[2], r[3]))))
    me = jnp.maximum(jnp.maximum(le[0], le[1]), jnp.maximum(le[2], le[3]))
    ex = [jnp.exp(x - me) for x in le]

    def first_argmax(vals):
        mx = jnp.maximum(jnp.maximum(vals[0], vals[1]), jnp.maximum(vals[2], vals[3]))
        ix = jnp.where(vals[0] == mx, 0, jnp.where(vals[1] == mx, 1, jnp.where(vals[2] == mx, 2, 3)))
        return mx, ix.astype(jnp.int32)

    e1, i1 = first_argmax(ex)
    ex2 = [jnp.where(i1 == e, -1.0, ex[e]) for e in range(EXP_PER_GROUP)]
    e2, i2 = first_argmax(ex2)
    tot = e1 + e2
    w1 = e1 / tot * p_group
    w2 = e2 / tot * p_group
    gates = [jnp.where(i1 == e, w1, jnp.where(i2 == e, w2, 0.0)) for e in range(EXP_PER_GROUP)]
    return gi, gates


def _mix_out_kernel(a_ref, w_ref, x_ref, g1_ref, lng_ref, lnb_ref, sc2_ref, sh2_ref,
                    wrh_ref, wrl_ref, br_ref, x1_ref, gt_ref, gi_ref):
    y = jnp.dot(a_ref[...], w_ref[...], preferred_element_type=F32)
    x1 = _layer_norm(DN_ALPHA * x_ref[...] + g1_ref[0] * y, lng_ref[...], lnb_ref[...])
    x1_ref[...] = x1
    h2 = x1 * (1.0 + sc2_ref[0]) + sh2_ref[0]
    hi = h2.astype(BF16)
    lo = (h2 - hi.astype(F32)).astype(BF16)
    wrh = wrh_ref[...]
    logits = (jnp.dot(hi, wrh, preferred_element_type=F32)
              + (jnp.dot(hi, wrl_ref[...], preferred_element_type=F32)
                 + jnp.dot(lo, wrh, preferred_element_type=F32))) + br_ref[...]
    lt = logits.T
    gi, gates = _route_rows(lt)
    gi_ref[0] = gi
    tm = lt.shape[1]
    row = lax.broadcasted_iota(jnp.int32, (8, tm), 0)
    g8 = jnp.where(row == 0, gates[0], jnp.where(row == 1, gates[1], jnp.where(row == 2, gates[2],
                   jnp.where(row == 3, gates[3], 0.0))))
    gt = jnp.concatenate([g8, jnp.zeros((ROUTE_W - 8, tm), F32)], axis=0)
    gt_ref[...] = gt.T


def _mix_out(tl, a, w, x, g1, lng, lnb, sc2, sh2, wrh, wrl, br, name):
    kd = a.shape[1]
    return pl.pallas_call(
        _mix_out_kernel,
        grid=(tl.n,),
        in_specs=[tl.rows(kd), _const_spec(w), tl.rows(D_MODEL), tl.mod_spec(g1),
                  _const_spec(lng), _const_spec(lnb), tl.mod_spec(sc2), tl.mod_spec(sh2),
                  _const_spec(wrh), _const_spec(wrl), _const_spec(br)],
        out_specs=[tl.rows(D_MODEL), tl.rows(ROUTE_W),
                   pl.BlockSpec((1, 1, tl.tm), lambda i: (i, 0, 0))],
        out_shape=[jax.ShapeDtypeStruct((tl.t, D_MODEL), F32),
                   jax.ShapeDtypeStruct((tl.t, ROUTE_W), F32),
                   jax.ShapeDtypeStruct((tl.n, 1, tl.tm), jnp.int32)],
        compiler_params=_cparams(("parallel",)),
        name=name,
    )(a, w, x, g1, lng, lnb, sc2, sh2, wrh, wrl, br)


TOK_SUB = 8
PACK_W = D_MODEL // 2
HI_MASK = 0xFFFF0000


def _tile_rows(ref_or_val, s, rows):
    return ref_or_val[pl.ds(s, rows, stride=TOK_SUB), :]


def _moe_sort_kernel(idx_ref, x1_ref, sc2_ref, sh2_ref, gt_ref, dst_ref, buf, sem, *, tm):
    i = pl.program_id(0)
    n = pl.num_programs(0)
    slot = i % 2

    def wait_slot(sl):
        pltpu.make_async_copy(buf.at[sl], dst_ref.at[pl.ds(0, tm * TOK_SUB)], sem.at[sl]).wait()

    @pl.when(i >= 2)
    def _():
        wait_slot(slot)

    h2 = x1_ref[...] * (1.0 + sc2_ref[0]) + sh2_ref[0]
    lo = lax.bitcast_convert_type(h2[:, :PACK_W].astype(BF16).astype(F32), jnp.uint32) >> 16
    hi = lax.bitcast_convert_type(h2[:, PACK_W:].astype(BF16).astype(F32), jnp.uint32) & jnp.uint32(HI_MASK)
    words = lax.bitcast_convert_type(lo | hi, F32)
    for s in range(PACK_W // LANES):
        buf[slot, pl.ds(s, tm, stride=TOK_SUB), :] = words[:, s * LANES:(s + 1) * LANES]
    buf[slot, pl.ds(PACK_W // LANES, tm, stride=TOK_SUB), :] = gt_ref[...]
    for s in range(PACK_W // LANES + 1, TOK_SUB):
        buf[slot, pl.ds(s, tm, stride=TOK_SUB), :] = jnp.zeros((tm, LANES), F32)

    base = i * tm

    def issue(r, carry):
        d = idx_ref[base + r]
        pltpu.make_async_copy(buf.at[slot, pl.ds(pl.multiple_of(r * TOK_SUB, TOK_SUB), TOK_SUB)],
                              dst_ref.at[pl.ds(pl.multiple_of(d * TOK_SUB, TOK_SUB), TOK_SUB)],
                              sem.at[slot]).start()
        return carry

    lax.fori_loop(0, tm, issue, 0, unroll=8)

    @pl.when(i == n - 1)
    def _():
        @pl.when(n >= 2)
        def _():
            wait_slot(1 - slot)
        wait_slot(slot)


def _moe_sort(tl, x1, sc2, sh2, gt, dst_idx, n_rows):
    tm = tl.tm
    n_steps = dst_idx.shape[0] // tm
    last = tl.n - 1
    tps = tl.tiles_per_seq
    kern = functools.partial(_moe_sort_kernel, tm=tm)
    rows = lambda w: pl.BlockSpec((tm, w), lambda i, idx: (jnp.minimum(i, last), 0))
    mod = lambda arr: pl.BlockSpec((1,) + arr.shape[1:], lambda i, idx: (jnp.minimum(i, last) // tps, 0, 0))
    return pl.pallas_call(
        kern,
        grid_spec=pltpu.PrefetchScalarGridSpec(
            num_scalar_prefetch=1,
            grid=(n_steps,),
            in_specs=[rows(D_MODEL), mod(sc2), mod(sh2), rows(ROUTE_W)],
            out_specs=pl.BlockSpec(memory_space=pl.ANY),
            scratch_shapes=[pltpu.VMEM((2, tm * TOK_SUB, LANES), F32), pltpu.SemaphoreType.DMA((2,))]),
        out_shape=jax.ShapeDtypeStruct((n_rows * TOK_SUB, LANES), F32),
        compiler_params=_cparams(("arbitrary",)),
        name="moe_sort_rows",
    )(dst_idx, x1, sc2, sh2, gt)


def _moe_kernel(tg_ref, nv_ref, tok_ref, wgu_ref, wdn_ref, o_ref, *, tme):
    i = pl.program_id(0)

    @pl.when(i < nv_ref[0])
    def _():
        words = jnp.concatenate([_tile_rows(tok_ref, s, tme) for s in range(PACK_W // LANES)], axis=1)
        words = lax.bitcast_convert_type(words, jnp.uint32)
        lo = lax.bitcast_convert_type(words << 16, F32)
        hi = lax.bitcast_convert_type(words & jnp.uint32(HI_MASK), F32)
        hb = jnp.concatenate([lo, hi], axis=1).astype(BF16)
        gates = _tile_rows(tok_ref, PACK_W // LANES, tme)
        acc = None
        for e in range(EXP_PER_GROUP):
            au = jnp.dot(hb, wgu_ref[0, e], preferred_element_type=F32)
            act = (_silu(au[:, :EXPERT_FF]) * au[:, EXPERT_FF:]).astype(BF16)
            y = jnp.dot(act, wdn_ref[0, e], preferred_element_type=F32)
            gy = gates[:, e:e + 1] * y
            acc = gy if acc is None else acc + gy
        for s in range(TOK_SUB):
            o_ref[pl.ds(s, tme, stride=TOK_SUB), :] = acc[:, s * LANES:(s + 1) * LANES]

    @pl.when(i >= nv_ref[0])
    def _():
        o_ref[...] = jnp.zeros_like(o_ref)


def _moe_experts(tok, tile_group, n_valid, wgu, wdn, tme):
    n_tiles = tok.shape[0] // (tme * TOK_SUB)
    kern = functools.partial(_moe_kernel, tme=tme)
    return pl.pallas_call(
        kern,
        grid_spec=pltpu.PrefetchScalarGridSpec(
            num_scalar_prefetch=2,
            grid=(n_tiles,),
            in_specs=[pl.BlockSpec((tme * TOK_SUB, LANES), lambda i, tg, nv: (i, 0)),
                      pl.BlockSpec((1, EXP_PER_GROUP, D_MODEL, 2 * EXPERT_FF),
                                   lambda i, tg, nv: (tg[i], 0, 0, 0)),
                      pl.BlockSpec((1, EXP_PER_GROUP, EXPERT_FF, D_MODEL),
                                   lambda i, tg, nv: (tg[i], 0, 0, 0))],
            out_specs=pl.BlockSpec((tme * TOK_SUB, LANES), lambda i, tg, nv: (i, 0))),
        out_shape=jax.ShapeDtypeStruct(tok.shape, F32),
        compiler_params=_cparams(("arbitrary",)),
        name="moe_experts",
    )(tile_group, n_valid, tok, wgu, wdn)


def _ln_res_kernel(idx_ref, fs_ref, x_ref, g_ref, lng_ref, lnb_ref, o_ref, buf, sem, *, tm):
    i = pl.program_id(0)
    n = pl.num_programs(0)
    slot = i % 2

    def issue(step, sl):
        base = step * tm

        def body(r, carry):
            d = idx_ref[base + r]
            pltpu.make_async_copy(fs_ref.at[pl.ds(pl.multiple_of(d * TOK_SUB, TOK_SUB), TOK_SUB)],
                                  buf.at[sl, pl.ds(pl.multiple_of(r * TOK_SUB, TOK_SUB), TOK_SUB)],
                                  sem.at[sl]).start()
            return carry

        lax.fori_loop(0, tm, body, 0, unroll=8)

    @pl.when(i == 0)
    def _():
        issue(0, 0)

    @pl.when(i + 1 < n)
    def _():
        issue(i + 1, 1 - slot)

    pltpu.make_async_copy(fs_ref.at[pl.ds(0, tm * TOK_SUB)], buf.at[slot], sem.at[slot]).wait()
    f = jnp.concatenate([buf[slot, pl.ds(s, tm, stride=TOK_SUB), :] for s in range(TOK_SUB)], axis=1)
    o_ref[...] = _layer_norm(DN_ALPHA * x_ref[...] + g_ref[0] * f, lng_ref[...], lnb_ref[...])


def _ln_res(tl, x, fs, dest, g2, lng, lnb):
    tm = tl.tm
    tps = tl.tiles_per_seq
    kern = functools.partial(_ln_res_kernel, tm=tm)
    const = lambda arr: pl.BlockSpec(arr.shape, lambda i, idx: (0,) * arr.ndim)
    return pl.pallas_call(
        kern,
        grid_spec=pltpu.PrefetchScalarGridSpec(
            num_scalar_prefetch=1,
            grid=(tl.n,),
            in_specs=[pl.BlockSpec(memory_space=pl.ANY),
                      pl.BlockSpec((tm, D_MODEL), lambda i, idx: (i, 0)),
                      pl.BlockSpec((1,) + g2.shape[1:], lambda i, idx: (i // tps, 0, 0)),
                      const(lng), const(lnb)],
            out_specs=pl.BlockSpec((tm, D_MODEL), lambda i, idx: (i, 0)),
            scratch_shapes=[pltpu.VMEM((2, tm * TOK_SUB, LANES), F32), pltpu.SemaphoreType.DMA((2,))]),
        out_shape=jax.ShapeDtypeStruct((tl.t, D_MODEL), F32),
        compiler_params=_cparams(("arbitrary",)),
        name="ffn_residual_ln",
    )(dest, fs, x, g2, lng, lnb)


def _moe_ffn(tl, x1, sc2, sh2, gt, gi, g2, lng, lnb, wgu, wdn):
    t = tl.t
    tme = TOKEN_TILE if t % TOKEN_TILE == 0 else 64
    n_tiles = t // tme + N_GROUPS
    n_pad = N_GROUPS * tme
    assert n_pad % tl.tm == 0
    onehot = (gi[:, None] == jnp.arange(N_GROUPS, dtype=jnp.int32)[None, :]).astype(jnp.int32)
    counts = jnp.sum(onehot, axis=0)
    rank = jnp.sum((jnp.cumsum(onehot, axis=0) - onehot) * onehot, axis=1)
    padded = ((counts + tme - 1) // tme) * tme
    ends = jnp.cumsum(padded)
    starts = ends - padded
    dest = (starts[gi] + rank).astype(jnp.int32)
    cpad = jnp.cumsum(padded - counts)
    k = jnp.arange(n_pad, dtype=jnp.int32)
    seg = jnp.sum((k[:, None] >= cpad[None, :]).astype(jnp.int32), axis=1)
    seg_base = jnp.concatenate([starts + counts, ends[-1:]])
    seg_first = jnp.concatenate([jnp.zeros((1,), cpad.dtype), cpad])
    pad_rows = (seg_base[seg] + (k - seg_first[seg])).astype(jnp.int32)
    tile_start = jnp.arange(n_tiles, dtype=jnp.int32) * tme
    tile_group = jnp.minimum(jnp.sum((tile_start[:, None] >= ends[None, :]).astype(jnp.int32), axis=1),
                             N_GROUPS - 1).astype(jnp.int32)
    n_valid = (ends[-1] // tme).astype(jnp.int32).reshape(1)
    tok = _moe_sort(tl, x1, sc2, sh2, gt, jnp.concatenate([dest, pad_rows]), n_tiles * tme)
    fs = _moe_experts(tok, tile_group, n_valid, wgu, wdn, tme)
    return _ln_res(tl, x1, fs, dest, g2, lng, lnb)


def _kv_latent_kernel(x_ref, w_ref, g_ref, cos_ref, sin_ref, lat_ref, kr_ref, krp_ref):
    kv = jnp.dot(x_ref[...].astype(BF16), w_ref[...], preferred_element_type=F32)
    c = kv[:, :KV_LORA]
    lat_ref[...] = c * lax.rsqrt(jnp.mean(c * c, axis=-1, keepdims=True) + RMS_EPS) * g_ref[...]
    kr = kv[:, KV_LORA:KV_LORA + LANES] * cos_ref[...] + kv[:, KV_LORA + LANES:] * sin_ref[...]
    krp_ref[...] = kr
    kr_ref[...] = kr[:, :MLA_ROPE]


def _kv_latent(tl, x, w_dkv_ext, kv_g, cos, sin):
    return pl.pallas_call(
        _kv_latent_kernel,
        grid=(tl.n,),
        in_specs=[tl.rows(D_MODEL), _const_spec(w_dkv_ext), _const_spec(kv_g),
                  tl.table_spec(cos), tl.table_spec(sin)],
        out_specs=[tl.rows(KV_LORA), tl.rows(MLA_ROPE), tl.rows(LANES)],
        out_shape=[jax.ShapeDtypeStruct((tl.t, KV_LORA), F32), jax.ShapeDtypeStruct((tl.t, MLA_ROPE), F32),
                   jax.ShapeDtypeStruct((tl.t, LANES), F32)],
        compiler_params=_cparams(("parallel",)),
        name="mla_kv_latent",
    )(x, w_dkv_ext, kv_g, cos, sin)


def _kv_expand_kernel(lat_ref, krp_ref, w_ref, k_ref, v_ref):
    kvx = jnp.dot(lat_ref[...].astype(BF16), w_ref[...], preferred_element_type=F32)
    krb = krp_ref[...].astype(BF16)
    ones_col = jnp.where(lax.broadcasted_iota(jnp.int32, krb.shape, 1) == 0, 1.0, 0.0).astype(BF16)
    v0 = MLA_HEADS * MLA_NOPE
    for hh in range(MLA_HEADS):
        k_ref[:, hh * HEAD_W:hh * HEAD_W + MLA_NOPE] = kvx[:, hh * MLA_NOPE:(hh + 1) * MLA_NOPE].astype(BF16)
        k_ref[:, hh * HEAD_W + MLA_NOPE:(hh + 1) * HEAD_W] = krb
        v_ref[:, hh * HEAD_W:hh * HEAD_W + MLA_V] = kvx[:, v0 + hh * MLA_V:v0 + (hh + 1) * MLA_V].astype(BF16)
        v_ref[:, hh * HEAD_W + MLA_V:(hh + 1) * HEAD_W] = ones_col


def _kv_expand(lat, krp, w_ukv_r, tm):
    t = lat.shape[0]
    rows = lambda w: pl.BlockSpec((tm, w), lambda i: (i, 0))
    return pl.pallas_call(
        _kv_expand_kernel,
        grid=(t // tm,),
        in_specs=[rows(KV_LORA), rows(LANES), _const_spec(w_ukv_r)],
        out_specs=[rows(MLA_HEADS * HEAD_W), rows(MLA_HEADS * HEAD_W)],
        out_shape=[jax.ShapeDtypeStruct((t, MLA_HEADS * HEAD_W), BF16),
                   jax.ShapeDtypeStruct((t, MLA_HEADS * HEAD_W), BF16)],
        compiler_params=_cparams(("parallel",)),
        name="mla_kv_expand",
    )(lat, krp, w_ukv_r)


def _q_proj_kernel(x_ref, sc_ref, sh_ref, wdq_ref, g_ref, wuq_ref, cos_ref, sin_ref, q_ref, *, qscale):
    h = (x_ref[...] * (1.0 + sc_ref[0]) + sh_ref[0]).astype(BF16)
    cq = jnp.dot(h, wdq_ref[...], preferred_element_type=F32)
    qn = (cq * lax.rsqrt(jnp.mean(cq * cq, axis=-1, keepdims=True) + RMS_EPS) * g_ref[...]).astype(BF16)
    cos = cos_ref[...]
    sin = sin_ref[...]
    nw = MLA_HEADS * LANES
    qnope = jnp.dot(qn, wuq_ref[:, :nw], preferred_element_type=F32)
    qpe = jnp.dot(qn, wuq_ref[:, nw:2 * nw], preferred_element_type=F32)
    qpe_sw = jnp.dot(qn, wuq_ref[:, 2 * nw:], preferred_element_type=F32)
    for hh in range(MLA_HEADS):
        cols = slice(hh * LANES, (hh + 1) * LANES)
        q_ref[:, hh * HEAD_W:hh * HEAD_W + LANES] = (qnope[:, cols] * qscale).astype(BF16)
        pe = qpe[:, cols] * cos + qpe_sw[:, cols] * sin
        q_ref[:, hh * HEAD_W + LANES:(hh + 1) * HEAD_W] = (pe * qscale).astype(BF16)


def _q_proj(tl, x, sc, sh, w_dq, q_g, w_uq_ext, cos, sin, qscale):
    kern = functools.partial(_q_proj_kernel, qscale=qscale)
    return pl.pallas_call(
        kern,
        grid=(tl.n,),
        in_specs=[tl.rows(D_MODEL), tl.mod_spec(sc), tl.mod_spec(sh), _const_spec(w_dq), _const_spec(q_g),
                  _const_spec(w_uq_ext), tl.table_spec(cos), tl.table_spec(sin)],
        out_specs=tl.rows(MLA_HEADS * HEAD_W),
        out_shape=jax.ShapeDtypeStruct((tl.t, MLA_HEADS * HEAD_W), BF16),
        compiler_params=_cparams(("parallel",)),
        name="mla_q_proj",
    )(x, sc, sh, w_dq, q_g, w_uq_ext, cos, sin)


def _attn_kernel(q_ref, k_ref, v_ref, o_ref, m_sc, acc_sc, sa_sc, sb_sc, *, tq, tk, q_pos0, n_keys, n_sub):
    i = pl.program_id(2)
    m_sc[...] = jnp.full(m_sc.shape, NEG_INF, F32)
    acc_sc[...] = jnp.zeros(acc_sc.shape, F32)
    row0 = q_pos0 + i * tq
    cq0 = row0 // CHUNK
    cq1 = (row0 + tq - 1) // CHUNK
    n_full = jnp.minimum(((cq0 + 1) * CHUNK) // tk, n_keys // tk)
    n_blk = jnp.minimum(((cq1 + 1) * CHUNK + tk - 1) // tk, (n_keys + tk - 1) // tk)
    rs = tq // n_sub

    def scores(j, s_ref):
        k = k_ref[0, pl.ds(pl.multiple_of(j * tk, tk), tk), :]
        s_ref[...] = lax.dot_general(q_ref[0], k, (((1,), (1,)), ((), ())), preferred_element_type=F32)

    def consume(j, s_ref, masked):
        v = v_ref[0, pl.ds(pl.multiple_of(j * tk, tk), tk), :]
        for u in range(n_sub):
            rows = slice(u * rs, (u + 1) * rs)
            s = s_ref[rows, :]
            if masked:
                qpos = row0 + u * rs + lax.broadcasted_iota(jnp.int32, (rs, 1), 0)
                kpos = j * tk + lax.broadcasted_iota(jnp.int32, (1, tk), 1)
                allowed = ((kpos >> LOG2_CHUNK) <= (qpos >> LOG2_CHUNK)) & (kpos < n_keys)
                s = jnp.where(allowed, s, NEG_INF)
            m_old = m_sc[rows, :]
            m_new = jnp.maximum(m_old, jnp.max(s, axis=1, keepdims=True))
            alpha = jnp.exp2(m_old - m_new)
            p = jnp.exp2(s - jnp.tile(m_new, (1, tk // LANES)))
            pv = jnp.dot(p.astype(BF16), v, preferred_element_type=F32)
            acc_sc[rows, :] = jnp.tile(alpha, (1, HEAD_W // LANES)) * acc_sc[rows, :] + pv
            m_sc[rows, :] = m_new

    def pair_step(jj, carry):
        j = 2 * jj
        scores(j + 1, sb_sc)
        consume(j, sa_sc, False)
        scores(j + 2, sa_sc)
        consume(j + 1, sb_sc, False)
        return carry

    def single_step(j, carry):
        consume(j, sa_sc, True)
        scores(j + 1, sa_sc)
        return carry

    n_pairs = jnp.minimum(n_full, n_blk - 1) // 2
    left = n_blk - 2 * n_pairs
    tail = 2 - left % 2
    t0 = n_blk - tail
    scores(0, sa_sc)
    lax.fori_loop(0, n_pairs, pair_step, 0)
    lax.fori_loop(2 * n_pairs, t0, single_step, 0)

    @pl.when(tail == 2)
    def _():
        scores(t0 + 1, sb_sc)
        consume(t0, sa_sc, True)
        consume(t0 + 1, sb_sc, True)

    @pl.when(tail == 1)
    def _():
        consume(t0, sa_sc, True)

    acc = acc_sc[...]
    o_ref[0] = (acc[:, :MLA_V] / acc[:, MLA_V:MLA_V + 1]).astype(BF16)


def _attention(qc, kc, vv, b, lq, lk, tq, tk, q_pos0, n_keys):
    n_sub = 2 if tq % (2 * LANES) == 0 else 1
    kern = functools.partial(_attn_kernel, tq=tq, tk=tk, q_pos0=q_pos0, n_keys=n_keys, n_sub=n_sub)
    return pl.pallas_call(
        kern,
        grid=(b, MLA_HEADS, lq // tq),
        in_specs=[pl.BlockSpec((1, tq, HEAD_W), lambda bi, h, i: (bi, i, h)),
                  pl.BlockSpec((1, lk, HEAD_W), lambda bi, h, i: (bi, 0, h)),
                  pl.BlockSpec((1, lk, HEAD_W), lambda bi, h, i: (bi, 0, h))],
        out_specs=pl.BlockSpec((1, tq, MLA_V), lambda bi, h, i: (bi, i, h)),
        out_shape=jax.ShapeDtypeStruct((b, lq, MLA_HEADS * MLA_V), BF16),
        scratch_shapes=[pltpu.VMEM((tq, LANES), F32), pltpu.VMEM((tq, HEAD_W), F32),
                        pltpu.VMEM((tq, tk), F32), pltpu.VMEM((tq, tk), F32)],
        compiler_params=_cparams(("parallel", "parallel", "arbitrary")),
        name="mla_attention",
    )(qc.reshape(b, lq, MLA_HEADS * HEAD_W), kc.reshape(b, lk, MLA_HEADS * HEAD_W),
      vv.reshape(b, lk, MLA_HEADS * HEAD_W))


def _rope_tables(pos, half):
    inv = ROPE_THETA ** (-jnp.arange(half, dtype=F32) / half)
    ang = pos.astype(F32)[:, None] * inv[None, :]
    return jnp.cos(ang), jnp.sin(ang)


def _mla_rope_tables(pos):
    cos, sin = _rope_tables(pos, MLA_ROPE // 2)
    z = jnp.zeros((pos.shape[0], LANES - MLA_ROPE), F32)
    return jnp.concatenate([cos, cos, z], axis=1), jnp.concatenate([-sin, sin, z], axis=1)


def _swap_halves(w):
    half = w.shape[-1] // 2
    return jnp.concatenate([w[..., half:], w[..., :half]], axis=-1)


def _pad_lanes(w):
    return jnp.pad(w, [(0, 0)] * (w.ndim - 1) + [(0, LANES - w.shape[-1])])


def _prep_weights(w_ret_in, w_ret_out, w_dq, q_norm_g, w_uq, w_mla_out, w_dkv, kv_norm_g, w_ukv,
                  w_route_group, b_route_group, w_route_expert, b_route_expert,
                  w_expert_gate_up, w_expert_down):
    p = {}
    p["w_ret_in"] = [w_ret_in[i].astype(BF16) for i in range(N_A)]
    p["w_ret_out"] = [w_ret_out[i].astype(BF16) for i in range(N_A)]
    p["w_dq"] = [w_dq[j].astype(BF16) for j in range(DEPTH - N_A)]
    p["q_norm_g"] = [q_norm_g[j][None, :] for j in range(DEPTH - N_A)]
    w_uq_ext = []
    for j in range(DEPTH - N_A):
        wq = w_uq[j].reshape(Q_LORA, MLA_HEADS, MLA_NOPE + MLA_ROPE)
        nope = wq[:, :, :MLA_NOPE].reshape(Q_LORA, MLA_HEADS * LANES)
        pe = wq[:, :, MLA_NOPE:]
        w_uq_ext.append(jnp.concatenate(
            [nope, _pad_lanes(pe).reshape(Q_LORA, MLA_HEADS * LANES),
             _pad_lanes(_swap_halves(pe)).reshape(Q_LORA, MLA_HEADS * LANES)], axis=1).astype(BF16))
    p["w_uq_ext"] = w_uq_ext
    p["w_mla_out"] = [w_mla_out[j].astype(BF16) for j in range(DEPTH - N_A)]
    kr = w_dkv[:, KV_LORA:]
    p["w_dkv_ext"] = jnp.concatenate([w_dkv[:, :KV_LORA], _pad_lanes(kr), _pad_lanes(_swap_halves(kr))],
                                     axis=1).astype(BF16)
    p["kv_norm_g"] = kv_norm_g[None, :]
    p["w_ukv_r"] = (w_ukv.reshape(KV_LORA, MLA_HEADS, 2, MLA_NOPE).transpose(0, 2, 1, 3)
                    .reshape(KV_LORA, 2 * MLA_HEADS * MLA_NOPE).astype(BF16))
    wr = jnp.concatenate([w_route_group, w_route_expert], axis=-1)
    wr = jnp.pad(wr, ((0, 0), (0, 0), (0, ROUTE_W - wr.shape[-1])))
    wr_hi = wr.astype(BF16)
    wr_lo = (wr - wr_hi.astype(F32)).astype(BF16)
    br = jnp.concatenate([b_route_group, b_route_expert], axis=-1)
    br = jnp.pad(br, ((0, 0), (0, ROUTE_W - br.shape[-1])))
    p["wr_hi"] = [wr_hi[l] for l in range(DEPTH)]
    p["wr_lo"] = [wr_lo[l] for l in range(DEPTH)]
    p["br"] = [br[l][None, :] for l in range(DEPTH)]
    p["w_gu"] = [w_expert_gate_up[l].astype(BF16).reshape(N_GROUPS, EXP_PER_GROUP, D_MODEL, 2 * EXPERT_FF)
                 for l in range(DEPTH)]
    p["w_dn"] = [w_expert_down[l].astype(BF16).reshape(N_GROUPS, EXP_PER_GROUP, EXPERT_FF, D_MODEL)
                 for l in range(DEPTH)]
    return p


def _trunk(x3, mod, pos0, ret_s0, past_latent, past_k_rope, ret_chunk, ret_block, ln_g, ln_b, p):
    b, l, _ = x3.shape
    tl = _Tiles(b, l)
    x = x3.reshape(tl.t, D_MODEL)
    pos = pos0 + jnp.arange(l, dtype=jnp.int32)
    new_ret = []
    latent_new = k_rope_new = None
    kc = vv = None
    lk = n_keys = None
    for layer in range(DEPTH):
        sh1, sc1, g1, sh2, sc2, g2 = [tl.mod(m) for m in jnp.split(mod[layer], 6, axis=-1)]
        lng = [ln_g[layer, s][None, :] for s in range(2)]
        lnb = [ln_b[layer, s][None, :] for s in range(2)]
        if layer < N_A:
            cos, sin = _rope_tables(pos, RET_DK // 2)
            q, k, v, sg = _ret_in(tl, x, sc1, sh1, p["w_ret_in"][layer], tl.table(cos), tl.table(sin))
            a, s_new = _retention(b, l, q, k, v, sg, ret_s0[layer], ret_chunk, ret_block)
            a = a.reshape(tl.t, RET_HEADS * RET_DV)
            new_ret.append(s_new)
            w_out = p["w_ret_out"][layer]
            name = "ret_out_ln_route"
        else:
            j = layer - N_A
            cos, sin = _mla_rope_tables(pos)
            cos, sin = tl.table(cos), tl.table(sin)
            if layer == N_A:
                latent_new, k_rope_new, krp = _kv_latent(tl, x, p["w_dkv_ext"], p["kv_norm_g"], cos, sin)
                lat_all = latent_new.reshape(b, l, KV_LORA)
                krp_all = krp.reshape(b, l, LANES)
                if past_latent is not None:
                    lat_all = jnp.concatenate([past_latent, lat_all], axis=1)
                    krp_all = jnp.concatenate([_pad_lanes(past_k_rope), krp_all], axis=1)
                n_keys = lat_all.shape[1]
                lk = -(-n_keys // LANES) * LANES
                if lk != n_keys:
                    lat_all = jnp.pad(lat_all, ((0, 0), (0, lk - n_keys), (0, 0)))
                    krp_all = jnp.pad(krp_all, ((0, 0), (0, lk - n_keys), (0, 0)))
                tk_rows = TOKEN_TILE if (b * lk) % TOKEN_TILE == 0 else 2 * LANES
                kc, vv = _kv_expand(lat_all.reshape(b * lk, KV_LORA), krp_all.reshape(b * lk, LANES),
                                    p["w_ukv_r"], tk_rows)
            qscale = float((MLA_NOPE + MLA_ROPE) ** -0.5 * math.log2(math.e))
            qc = _q_proj(tl, x, sc1, sh1, p["w_dq"][j], p["q_norm_g"][j], p["w_uq_ext"][j], cos, sin, qscale)
            if l % ATTN_TILE == 0:
                tq, tk = ATTN_TILE, ATTN_TILE
            else:
                tq, tk = l, lk
            a = _attention(qc, kc, vv, b, l, lk, tq, tk, pos0, n_keys).reshape(tl.t, MLA_HEADS * MLA_V)
            w_out = p["w_mla_out"][j]
            name = "mla_out_ln_route"
        x1, gt, gi = _mix_out(tl, a, w_out, x, g1, lng[0], lnb[0], sc2, sh2,
                              p["wr_hi"][layer], p["wr_lo"][layer], p["br"][layer], name)
        x = _moe_ffn(tl, x1, sc2, sh2, gt, gi.reshape(tl.t), g2, lng[1], lnb[1],
                     p["w_gu"][layer], p["w_dn"][layer])
    return (x.reshape(b, l, D_MODEL), jnp.stack(new_ret), latent_new.reshape(b, l, KV_LORA),
            k_rope_new.reshape(b, l, MLA_ROPE))


def kernel(x_prompt, x_sample, state_retention, cache_kv_latent, cache_k_rope, c_prompt, c_sample,
           w_ada, b_ada, ln_g, ln_b, w_ret_in, w_ret_out, w_dq, q_norm_g, w_uq, w_mla_out,
           w_dkv, kv_norm_g, w_ukv, w_route_group, b_route_group, w_route_expert, b_route_expert,
           w_expert_gate_up, w_expert_down):
    bp, lp, _ = x_prompt.shape
    bs, ls, _ = x_sample.shape
    p = _prep_weights(w_ret_in, w_ret_out, w_dq, q_norm_g, w_uq, w_mla_out, w_dkv, kv_norm_g, w_ukv,
                      w_route_group, b_route_group, w_route_expert, b_route_expert,
                      w_expert_gate_up, w_expert_down)
    n_seq = bp + bs
    n_rows = -(-n_seq // 8) * 8
    c_all = jnp.concatenate([c_prompt, c_sample, jnp.zeros((n_rows - n_seq, D_MODEL), F32)], axis=0)
    mod = _ada(c_all, w_ada, b_ada)
    s0_p = jnp.zeros((N_A, bp, RET_HEADS, RET_DK, RET_DV), F32)
    ret_chunk_p = RET_CHUNK if lp % RET_CHUNK == 0 else CHUNK
    y_p, st_p, lat_p, kr_p = _trunk(x_prompt, mod[:, :bp], 0, s0_p, None, None,
                                    ret_chunk_p, max(ret_chunk_p, min(lp, TOKEN_TILE)), ln_g, ln_b, p)
    past = cache_kv_latent.shape[1]
    y_s, st_s, lat_s, kr_s = _trunk(x_sample, mod[:, bp:n_seq], past, state_retention,
                                    cache_kv_latent, cache_k_rope, ls, ls, ln_g, ln_b, p)
    return (y_p, y_s, st_p, st_s, lat_p, kr_p, lat_s, kr_s)
```

```python
import functools
import math

import numpy as np
import jax
import jax.numpy as jnp
from jax import lax
from jax.experimental import pallas as pl
from jax.experimental.pallas import tpu as pltpu

F32 = jnp.float32
BF16 = jnp.bfloat16

D_MODEL = 1024
DEPTH = 2
CHUNK = 64
LOG2_CHUNK = 6
N_A = DEPTH // 2
RET_HEADS = 4
RET_DK = 256
RET_DV = 512
MLA_HEADS = 8
MLA_NOPE = 128
MLA_ROPE = 64
MLA_V = 128
Q_LORA = 384
KV_LORA = 256
ROPE_THETA = 10000.0
N_GROUPS = 4
EXP_PER_GROUP = 4
N_EXPERTS = N_GROUPS * EXP_PER_GROUP
N_PAIRS = EXP_PER_GROUP * (EXP_PER_GROUP - 1) // 2
N_BUCKETS = N_GROUPS * N_PAIRS
PAIR_A = (0, 0, 0, 1, 1, 2)
PAIR_B = (1, 2, 3, 2, 3, 3)
EXPERT_FF = 512
LN_EPS = 1e-5
RMS_EPS = 1e-6
DN_ALPHA = (2 * DEPTH) ** 0.25
NEG_INF = -1e30

LANES = 128
ROUTE_W = LANES
TOKEN_TILE = 512
EXPERT_TILE = 256
RET_CHUNK = 256
ATTN_TILE = 512
HEAD_W = 2 * LANES
VMEM_LIMIT = 56 * 1024 * 1024


def _cparams(sem, vmem=VMEM_LIMIT):
    return pltpu.CompilerParams(dimension_semantics=sem, vmem_limit_bytes=vmem)


def _silu(x):
    return x * jax.nn.sigmoid(x)


def _layer_norm(z, g, b):
    mu = jnp.mean(z, axis=-1, keepdims=True)
    zc = z - mu
    var = jnp.mean(zc * zc, axis=-1, keepdims=True)
    return zc * lax.rsqrt(var + LN_EPS) * g + b


def _ada_kernel(c_ref, w_ref, b_ref, o_ref):
    s = _silu(c_ref[...]).astype(BF16)
    o_ref[0] = jnp.dot(s, w_ref[0].astype(BF16), preferred_element_type=F32) + b_ref[0]


def _ada(c_all, w_ada, b_ada):
    r = c_all.shape[0]
    n = w_ada.shape[-1]
    tn = 1536
    return pl.pallas_call(
        _ada_kernel,
        grid=(DEPTH, n // tn),
        in_specs=[pl.BlockSpec((r, D_MODEL), lambda l, j: (0, 0)),
                  pl.BlockSpec((1, D_MODEL, tn), lambda l, j: (l, 0, j)),
                  pl.BlockSpec((1, 1, tn), lambda l, j: (l, 0, j))],
        out_specs=pl.BlockSpec((1, r, tn), lambda l, j: (l, 0, j)),
        out_shape=jax.ShapeDtypeStruct((DEPTH, r, n), F32),
        compiler_params=_cparams(("parallel", "parallel")),
        name="ada_mod",
    )(c_all, w_ada, b_ada.reshape(DEPTH, 1, n))


class _Tiles:
    def __init__(self, b, l):
        self.b, self.l, self.t = b, l, b * l
        self.per_token = (l % TOKEN_TILE) != 0
        self.tm = self.t if self.per_token else TOKEN_TILE
        self.n = self.t // self.tm
        self.tiles_per_seq = 1 if self.per_token else l // self.tm

    def mod(self, m):
        if self.per_token:
            return jnp.repeat(m, self.l, axis=0)[None]
        return m[:, None, :]

    def mod_spec(self, arr):
        tps = self.tiles_per_seq
        return pl.BlockSpec((1,) + arr.shape[1:], lambda i: (i // tps, 0, 0))

    def table(self, tab):
        return jnp.tile(tab, (self.b, 1)) if self.per_token else tab

    def table_spec(self, tab):
        nt = tab.shape[0] // self.tm
        return pl.BlockSpec((self.tm, tab.shape[1]), lambda i: (i % nt, 0))

    def rows(self, w):
        return pl.BlockSpec((self.tm, w), lambda i: (i, 0))


def _const_spec(arr):
    nd = arr.ndim
    return pl.BlockSpec(arr.shape, lambda *_: (0,) * nd)


def _ret_in_kernel(x_ref, sc_ref, sh_ref, w_ref, cos_ref, sin_ref, q_ref, k_ref, v_ref, sg_ref):
    h = (x_ref[...] * (1.0 + sc_ref[0]) + sh_ref[0]).astype(BF16)
    cos = cos_ref[...]
    sin = sin_ref[...]
    qk_w = RET_HEADS * RET_DK
    half = RET_DK // 2

    def rope_store(r, out_ref, scale):
        for hh in range(RET_HEADS):
            a = r[:, hh * RET_DK:hh * RET_DK + half]
            b = r[:, hh * RET_DK + half:(hh + 1) * RET_DK]
            out_ref[:, hh * RET_DK:hh * RET_DK + half] = ((a * cos - b * sin) * scale).astype(BF16)
            out_ref[:, hh * RET_DK + half:(hh + 1) * RET_DK] = ((a * sin + b * cos) * scale).astype(BF16)

    r = jnp.dot(h, w_ref[:, 0:qk_w], preferred_element_type=F32)
    rope_store(r, q_ref, 1.0)
    r = jnp.dot(h, w_ref[:, qk_w:2 * qk_w], preferred_element_type=F32)
    rope_store(r, k_ref, RET_DK ** -0.5)
    for c in range(2):
        lo = 2 * qk_w + c * qk_w
        v_ref[:, c * qk_w:(c + 1) * qk_w] = jnp.dot(
            h, w_ref[:, lo:lo + qk_w], preferred_element_type=F32).astype(BF16)
    for c in range(2):
        lo = 4 * qk_w + c * qk_w
        g = jnp.dot(h, w_ref[:, lo:lo + qk_w], preferred_element_type=F32)
        sg_ref[:, c * qk_w:(c + 1) * qk_w] = _silu(g).astype(BF16)


def _ret_in(tl, x, sc, sh, w_in, cos, sin):
    qk_w, v_w = RET_HEADS * RET_DK, RET_HEADS * RET_DV
    return pl.pallas_call(
        _ret_in_kernel,
        grid=(tl.n,),
        in_specs=[tl.rows(D_MODEL), tl.mod_spec(sc), tl.mod_spec(sh), _const_spec(w_in),
                  tl.table_spec(cos), tl.table_spec(sin)],
        out_specs=[tl.rows(qk_w), tl.rows(qk_w), tl.rows(v_w), tl.rows(v_w)],
        out_shape=[jax.ShapeDtypeStruct((tl.t, qk_w), BF16), jax.ShapeDtypeStruct((tl.t, qk_w), BF16),
                   jax.ShapeDtypeStruct((tl.t, v_w), BF16), jax.ShapeDtypeStruct((tl.t, v_w), BF16)],
        compiler_params=_cparams(("parallel",)),
        name="ret_in_proj",
    )(x, sc, sh, w_in, cos, sin)


def _retention_tables(chunk):
    lg = np.log1p(-np.exp2(-5.0 - np.arange(RET_HEADS, dtype=np.float64)))
    idx = np.arange(chunk, dtype=np.float64)
    diff = idx[:, None] - idx[None, :]
    dmask = np.where(diff >= 0, np.exp(lg[:, None, None] * np.maximum(diff, 0.0)), 0.0)
    xi = np.exp(lg[:, None] * (idx[None, :] + 1.0))[:, :, None]
    zeta = np.exp(lg[:, None] * (chunk - 1.0 - idx[None, :]))[:, :, None]
    cdec = np.exp(lg * chunk)
    return (jnp.asarray(dmask, F32), jnp.asarray(xi, F32), jnp.asarray(zeta, F32),
            [float(np.float32(c)) for c in cdec])


def _retention_kernel(q_ref, k_ref, v_ref, sg_ref, s0_ref, dm_ref, xi_ref, zeta_ref,
                      o_ref, sout_ref, s_sc, *, chunk, n_chunks, cdec):
    j = pl.program_id(1)

    @pl.when(j == 0)
    def _():
        s_sc[...] = s0_ref[0]

    for c in range(n_chunks):
        rows = slice(c * chunk, (c + 1) * chunk)
        for hh in range(RET_HEADS):
            kcols = slice(hh * RET_DK, (hh + 1) * RET_DK)
            vcols = slice(hh * RET_DV, (hh + 1) * RET_DV)
            q = q_ref[0, rows, kcols]
            k = k_ref[0, rows, kcols]
            v = v_ref[0, rows, vcols]
            s_old = s_sc[hh]
            sc = lax.dot_general(q, k, (((1,), (1,)), ((), ())), preferred_element_type=F32) * dm_ref[hh]
            o = jnp.dot(sc.astype(BF16), v, preferred_element_type=F32)
            o = o + jnp.dot(q, s_old.astype(BF16), preferred_element_type=F32) * xi_ref[hh]
            kz_t = (k.astype(F32) * zeta_ref[hh]).T.astype(BF16)
            s_sc[hh] = s_old * cdec[hh] + jnp.dot(kz_t, v, preferred_element_type=F32)
            mu = jnp.mean(o, axis=-1, keepdims=True)
            oc = o - mu
            var = jnp.mean(oc * oc, axis=-1, keepdims=True)
            on = oc * lax.rsqrt(var + LN_EPS)
            o_ref[0, rows, vcols] = (sg_ref[0, rows, vcols].astype(F32) * on).astype(BF16)

    @pl.when(j == pl.num_programs(1) - 1)
    def _():
        sout_ref[0] = s_sc[...]


def _retention(b, l, q, k, v, sg, s0, chunk, block):
    qk_w, v_w = RET_HEADS * RET_DK, RET_HEADS * RET_DV
    dmask, xi, zeta, cdec = _retention_tables(chunk)
    kern = functools.partial(_retention_kernel, chunk=chunk, n_chunks=block // chunk, cdec=cdec)
    seq = lambda w: pl.BlockSpec((1, block, w), lambda bi, j: (bi, j, 0))
    st = pl.BlockSpec((1, RET_HEADS, RET_DK, RET_DV), lambda bi, j: (bi, 0, 0, 0))
    return pl.pallas_call(
        kern,
        grid=(b, l // block),
        in_specs=[seq(qk_w), seq(qk_w), seq(v_w), seq(v_w), st,
                  _const_spec(dmask), _const_spec(xi), _const_spec(zeta)],
        out_specs=[seq(v_w), st],
        out_shape=[jax.ShapeDtypeStruct((b, l, v_w), BF16),
                   jax.ShapeDtypeStruct((b, RET_HEADS, RET_DK, RET_DV), F32)],
        scratch_shapes=[pltpu.VMEM((RET_HEADS, RET_DK, RET_DV), F32)],
        compiler_params=_cparams(("parallel", "arbitrary")),
        name="retention_scan",
    )(q.reshape(b, l, qk_w), k.reshape(b, l, qk_w), v.reshape(b, l, v_w), sg.reshape(b, l, v_w),
      s0, dmask, xi, zeta)


def _route_rows(lt):
    g = [lt[i:i + 1] for i in range(N_GROUPS)]
    m = jnp.maximum(jnp.maximum(g[0], g[1]), jnp.maximum(g[2], g[3]))
    gi = jnp.where(g[0] == m, 0, jnp.where(g[1] == m, 1, jnp.where(g[2] == m, 2, 3))).astype(jnp.int32)
    denom = jnp.exp(g[0] - m) + jnp.exp(g[1] - m) + jnp.exp(g[2] - m) + jnp.exp(g[3] - m)
    p_group = 1.0 / denom
    le = []
    for e in range(EXP_PER_GROUP):
        r = [lt[N_GROUPS + gg * EXP_PER_GROUP + e:N_GROUPS + gg * EXP_PER_GROUP + e + 1]
             for gg in range(N_GROUPS)]
        le.append(jnp.where(gi == 0, r[0], jnp.where(gi == 1, r[1], jnp.where(gi == 2, r[2], r[3]))))
    me = jnp.maximum(jnp.maximum(le[0], le[1]), jnp.maximum(le[2], le[3]))
    ex = [jnp.exp(x - me) for x in le]

    def first_argmax(vals):
        mx = jnp.maximum(jnp.maximum(vals[0], vals[1]), jnp.maximum(vals[2], vals[3]))
        ix = jnp.where(vals[0] == mx, 0, jnp.where(vals[1] == mx, 1, jnp.where(vals[2] == mx, 2, 3)))
        return mx, ix.astype(jnp.int32)

    e1, i1 = first_argmax(ex)
    ex2 = [jnp.where(i1 == e, -1.0, ex[e]) for e in range(EXP_PER_GROUP)]
    e2, i2 = first_argmax(ex2)
    tot = e1 + e2
    w1 = e1 / tot * p_group
    w2 = e2 / tot * p_group
    first_low = i1 < i2
    a = jnp.where(first_low, i1, i2)
    b = jnp.where(first_low, i2, i1)
    pair = jnp.where(a == 0, b - 1, jnp.where(a == 1, b + 1, N_PAIRS - 1))
    bucket = gi * N_PAIRS + pair
    return bucket, jnp.where(first_low, w1, w2), jnp.where(first_low, w2, w1)


def _mix_out_kernel(a_ref, w_ref, x_ref, g1_ref, lng_ref, lnb_ref, sc2_ref, sh2_ref,
                    wrh_ref, wrl_ref, br_ref, x1_ref, gt_ref, gi_ref):
    y = jnp.dot(a_ref[...], w_ref[...], preferred_element_type=F32)
    x1 = _layer_norm(DN_ALPHA * x_ref[...] + g1_ref[0] * y, lng_ref[...], lnb_ref[...])
    x1_ref[...] = x1
    h2 = x1 * (1.0 + sc2_ref[0]) + sh2_ref[0]
    hi = h2.astype(BF16)
    lo = (h2 - hi.astype(F32)).astype(BF16)
    wrh = wrh_ref[...]
    logits = (jnp.dot(hi, wrh, preferred_element_type=F32)
              + (jnp.dot(hi, wrl_ref[...], preferred_element_type=F32)
                 + jnp.dot(lo, wrh, preferred_element_type=F32))) + br_ref[...]
    lt = logits.T
    bucket, gate_a, gate_b = _route_rows(lt)
    gi_ref[0] = bucket
    tm = lt.shape[1]
    row = lax.broadcasted_iota(jnp.int32, (8, tm), 0)
    g8 = jnp.where(row == 0, gate_a, jnp.where(row == 1, gate_b, 0.0))
    gt = jnp.concatenate([g8, jnp.zeros((ROUTE_W - 8, tm), F32)], axis=0)
    gt_ref[...] = gt.T


def _mix_out(tl, a, w, x, g1, lng, lnb, sc2, sh2, wrh, wrl, br, name):
    kd = a.shape[1]
    return pl.pallas_call(
        _mix_out_kernel,
        grid=(tl.n,),
        in_specs=[tl.rows(kd), _const_spec(w), tl.rows(D_MODEL), tl.mod_spec(g1),
                  _const_spec(lng), _const_spec(lnb), tl.mod_spec(sc2), tl.mod_spec(sh2),
                  _const_spec(wrh), _const_spec(wrl), _const_spec(br)],
        out_specs=[tl.rows(D_MODEL), tl.rows(ROUTE_W),
                   pl.BlockSpec((1, 1, tl.tm), lambda i: (i, 0, 0))],
        out_shape=[jax.ShapeDtypeStruct((tl.t, D_MODEL), F32),
                   jax.ShapeDtypeStruct((tl.t, ROUTE_W), F32),
                   jax.ShapeDtypeStruct((tl.n, 1, tl.tm), jnp.int32)],
        compiler_params=_cparams(("parallel",)),
        name=name,
    )(a, w, x, g1, lng, lnb, sc2, sh2, wrh, wrl, br)


TOK_SUB = D_MODEL // LANES


def _moe_sort_kernel(idx_ref, x1_ref, sc2_ref, sh2_ref, gt_ref, tok_ref, gs_ref, buf, gbuf, sem, *, tm):
    i = pl.program_id(0)
    n = pl.num_programs(0)
    slot = i % 2

    def wait_slot(sl):
        pltpu.make_async_copy(buf.at[sl], tok_ref.at[pl.ds(0, tm * TOK_SUB)], sem.at[0, sl]).wait()
        pltpu.make_async_copy(gbuf.at[sl], gs_ref.at[pl.ds(0, tm)], sem.at[1, sl]).wait()

    @pl.when(i >= 2)
    def _():
        wait_slot(slot)

    h2 = x1_ref[...] * (1.0 + sc2_ref[0]) + sh2_ref[0]
    for s in range(TOK_SUB):
        buf[slot, pl.ds(s, tm, stride=TOK_SUB), :] = h2[:, s * LANES:(s + 1) * LANES]
    gbuf[slot] = gt_ref[...]

    base = i * tm

    def issue(r, carry):
        d = idx_ref[base + r]
        pltpu.make_async_copy(buf.at[slot, pl.ds(pl.multiple_of(r * TOK_SUB, TOK_SUB), TOK_SUB)],
                              tok_ref.at[pl.ds(pl.multiple_of(d * TOK_SUB, TOK_SUB), TOK_SUB)],
                              sem.at[0, slot]).start()
        pltpu.make_async_copy(gbuf.at[slot, pl.ds(r, 1)], gs_ref.at[pl.ds(d, 1)], sem.at[1, slot]).start()
        return carry

    lax.fori_loop(0, tm, issue, 0, unroll=8)

    @pl.when(i == n - 1)
    def _():
        @pl.when(n >= 2)
        def _():
            wait_slot(1 - slot)
        wait_slot(slot)


def _moe_sort(tl, x1, sc2, sh2, gt, dst_idx, n_rows):
    tm = tl.tm
    n_steps = dst_idx.shape[0] // tm
    last = tl.n - 1
    tps = tl.tiles_per_seq
    kern = functools.partial(_moe_sort_kernel, tm=tm)
    rows = lambda w: pl.BlockSpec((tm, w), lambda i, idx: (jnp.minimum(i, last), 0))
    mod = lambda arr: pl.BlockSpec((1,) + arr.shape[1:], lambda i, idx: (jnp.minimum(i, last) // tps, 0, 0))
    return pl.pallas_call(
        kern,
        grid_spec=pltpu.PrefetchScalarGridSpec(
            num_scalar_prefetch=1,
            grid=(n_steps,),
            in_specs=[rows(D_MODEL), mod(sc2), mod(sh2), rows(ROUTE_W)],
            out_specs=[pl.BlockSpec(memory_space=pl.ANY), pl.BlockSpec(memory_space=pl.ANY)],
            scratch_shapes=[pltpu.VMEM((2, tm * TOK_SUB, LANES), F32), pltpu.VMEM((2, tm, ROUTE_W), F32),
                            pltpu.SemaphoreType.DMA((2, 2))]),
        out_shape=[jax.ShapeDtypeStruct((n_rows * TOK_SUB, LANES), F32),
                   jax.ShapeDtypeStruct((n_rows, ROUTE_W), F32)],
        compiler_params=_cparams(("arbitrary",)),
        name="moe_sort_rows",
    )(dst_idx, x1, sc2, sh2, gt)


def _moe_kernel(ea_ref, eb_ref, nv_ref, tok_ref, gs_ref, wgu_a_ref, wgu_b_ref, wdn_a_ref, wdn_b_ref, o_ref,
                *, tme):
    i = pl.program_id(0)

    @pl.when(i < nv_ref[0])
    def _():
        hb = jnp.concatenate([tok_ref[pl.ds(s, tme, stride=TOK_SUB), :] for s in range(TOK_SUB)],
                             axis=1).astype(BF16)
        acc = None
        for lane, (wgu_ref, wdn_ref) in enumerate(((wgu_a_ref, wdn_a_ref), (wgu_b_ref, wdn_b_ref))):
            au = jnp.dot(hb, wgu_ref[0], preferred_element_type=F32)
            act = (_silu(au[:, :EXPERT_FF]) * au[:, EXPERT_FF:]).astype(BF16)
            y = jnp.dot(act, wdn_ref[0], preferred_element_type=F32)
            gy = gs_ref[:, lane:lane + 1] * y
            acc = gy if acc is None else acc + gy
        for s in range(TOK_SUB):
            o_ref[pl.ds(s, tme, stride=TOK_SUB), :] = acc[:, s * LANES:(s + 1) * LANES]

    @pl.when(i >= nv_ref[0])
    def _():
        o_ref[...] = jnp.zeros_like(o_ref)


def _moe_experts(tok, gs, tile_ea, tile_eb, n_valid, wgu, wdn, tme):
    n_tiles = gs.shape[0] // tme
    kern = functools.partial(_moe_kernel, tme=tme)
    gu = lambda pick: pl.BlockSpec((1, D_MODEL, 2 * EXPERT_FF), lambda i, ea, eb, nv: (pick(ea, eb)[i], 0, 0))
    dn = lambda pick: pl.BlockSpec((1, EXPERT_FF, D_MODEL), lambda i, ea, eb, nv: (pick(ea, eb)[i], 0, 0))
    first = lambda ea, eb: ea
    second = lambda ea, eb: eb
    return pl.pallas_call(
        kern,
        grid_spec=pltpu.PrefetchScalarGridSpec(
            num_scalar_prefetch=3,
            grid=(n_tiles,),
            in_specs=[pl.BlockSpec((tme * TOK_SUB, LANES), lambda i, ea, eb, nv: (i, 0)),
                      pl.BlockSpec((tme, ROUTE_W), lambda i, ea, eb, nv: (i, 0)),
                      gu(first), gu(second), dn(first), dn(second)],
            out_specs=pl.BlockSpec((tme * TOK_SUB, LANES), lambda i, ea, eb, nv: (i, 0))),
        out_shape=jax.ShapeDtypeStruct(tok.shape, F32),
        compiler_params=_cparams(("arbitrary",)),
        name="moe_experts",
    )(tile_ea, tile_eb, n_valid, tok, gs, wgu, wgu, wdn, wdn)


def _ln_res_kernel(idx_ref, fs_ref, x_ref, g_ref, lng_ref, lnb_ref, o_ref, buf, sem, *, tm):
    i = pl.program_id(0)
    n = pl.num_programs(0)
    slot = i % 2

    def issue(step, sl):
        base = step * tm

        def body(r, carry):
            d = idx_ref[base + r]
            pltpu.make_async_copy(fs_ref.at[pl.ds(pl.multiple_of(d * TOK_SUB, TOK_SUB), TOK_SUB)],
                                  buf.at[sl, pl.ds(pl.multiple_of(r * TOK_SUB, TOK_SUB), TOK_SUB)],
                                  sem.at[sl]).start()
            return carry

        lax.fori_loop(0, tm, body, 0, unroll=8)

    @pl.when(i == 0)
    def _():
        issue(0, 0)

    @pl.when(i + 1 < n)
    def _():
        issue(i + 1, 1 - slot)

    pltpu.make_async_copy(fs_ref.at[pl.ds(0, tm * TOK_SUB)], buf.at[slot], sem.at[slot]).wait()
    f = jnp.concatenate([buf[slot, pl.ds(s, tm, stride=TOK_SUB), :] for s in range(TOK_SUB)], axis=1)
    o_ref[...] = _layer_norm(DN_ALPHA * x_ref[...] + g_ref[0] * f, lng_ref[...], lnb_ref[...])


def _ln_res(tl, x, fs, dest, g2, lng, lnb):
    tm = tl.tm
    tps = tl.tiles_per_seq
    kern = functools.partial(_ln_res_kernel, tm=tm)
    const = lambda arr: pl.BlockSpec(arr.shape, lambda i, idx: (0,) * arr.ndim)
    return pl.pallas_call(
        kern,
        grid_spec=pltpu.PrefetchScalarGridSpec(
            num_scalar_prefetch=1,
            grid=(tl.n,),
            in_specs=[pl.BlockSpec(memory_space=pl.ANY),
                      pl.BlockSpec((tm, D_MODEL), lambda i, idx: (i, 0)),
                      pl.BlockSpec((1,) + g2.shape[1:], lambda i, idx: (i // tps, 0, 0)),
                      const(lng), const(lnb)],
            out_specs=pl.BlockSpec((tm, D_MODEL), lambda i, idx: (i, 0)),
            scratch_shapes=[pltpu.VMEM((2, tm * TOK_SUB, LANES), F32), pltpu.SemaphoreType.DMA((2,))]),
        out_shape=jax.ShapeDtypeStruct((tl.t, D_MODEL), F32),
        compiler_params=_cparams(("arbitrary",)),
        name="ffn_residual_ln",
    )(dest, fs, x, g2, lng, lnb)


def _moe_ffn(tl, x1, sc2, sh2, gt, gi, g2, lng, lnb, wgu, wdn):
    t = tl.t
    tme = EXPERT_TILE if t % TOKEN_TILE == 0 else 64
    n_tiles = t // tme + N_BUCKETS
    n_pad = N_BUCKETS * tme
    assert n_pad % tl.tm == 0
    onehot = (gi[:, None] == jnp.arange(N_BUCKETS, dtype=jnp.int32)[None, :]).astype(jnp.int32)
    counts = jnp.sum(onehot, axis=0)
    rank = jnp.sum((jnp.cumsum(onehot, axis=0) - onehot) * onehot, axis=1)
    padded = ((counts + tme - 1) // tme) * tme
    ends = jnp.cumsum(padded)
    starts = ends - padded
    dest = (starts[gi] + rank).astype(jnp.int32)
    cpad = jnp.cumsum(padded - counts)
    k = jnp.arange(n_pad, dtype=jnp.int32)
    seg = jnp.sum((k[:, None] >= cpad[None, :]).astype(jnp.int32), axis=1)
    seg_base = jnp.concatenate([starts + counts, ends[-1:]])
    seg_first = jnp.concatenate([jnp.zeros((1,), cpad.dtype), cpad])
    pad_rows = (seg_base[seg] + (k - seg_first[seg])).astype(jnp.int32)
    tile_start = jnp.arange(n_tiles, dtype=jnp.int32) * tme
    tile_bucket = jnp.minimum(jnp.sum((tile_start[:, None] >= ends[None, :]).astype(jnp.int32), axis=1),
                              N_BUCKETS - 1)
    first_expert = (tile_bucket // N_PAIRS) * EXP_PER_GROUP
    tile_ea = (first_expert + jnp.asarray(PAIR_A, jnp.int32)[tile_bucket % N_PAIRS]).astype(jnp.int32)
    tile_eb = (first_expert + jnp.asarray(PAIR_B, jnp.int32)[tile_bucket % N_PAIRS]).astype(jnp.int32)
    n_valid = (ends[-1] // tme).astype(jnp.int32).reshape(1)
    tok, gs = _moe_sort(tl, x1, sc2, sh2, gt, jnp.concatenate([dest, pad_rows]), n_tiles * tme)
    fs = _moe_experts(tok, gs, tile_ea, tile_eb, n_valid, wgu, wdn, tme)
    return _ln_res(tl, x1, fs, dest, g2, lng, lnb)


def _kv_latent_kernel(x_ref, w_ref, g_ref, cos_ref, sin_ref, lat_ref, kr_ref, krp_ref):
    kv = jnp.dot(x_ref[...].astype(BF16), w_ref[...], preferred_element_type=F32)
    c = kv[:, :KV_LORA]
    lat_ref[...] = c * lax.rsqrt(jnp.mean(c * c, axis=-1, keepdims=True) + RMS_EPS) * g_ref[...]
    kr = kv[:, KV_LORA:KV_LORA + LANES] * cos_ref[...] + kv[:, KV_LORA + LANES:] * sin_ref[...]
    krp_ref[...] = kr
    kr_ref[...] = kr[:, :MLA_ROPE]


def _kv_latent(tl, x, w_dkv_ext, kv_g, cos, sin):
    return pl.pallas_call(
        _kv_latent_kernel,
        grid=(tl.n,),
        in_specs=[tl.rows(D_MODEL), _const_spec(w_dkv_ext), _const_spec(kv_g),
                  tl.table_spec(cos), tl.table_spec(sin)],
        out_specs=[tl.rows(KV_LORA), tl.rows(MLA_ROPE), tl.rows(LANES)],
        out_shape=[jax.ShapeDtypeStruct((tl.t, KV_LORA), F32), jax.ShapeDtypeStruct((tl.t, MLA_ROPE), F32),
                   jax.ShapeDtypeStruct((tl.t, LANES), F32)],
        compiler_params=_cparams(("parallel",)),
        name="mla_kv_latent",
    )(x, w_dkv_ext, kv_g, cos, sin)


def _kv_expand_kernel(lat_ref, krp_ref, w_ref, k_ref, v_ref):
    kvx = jnp.dot(lat_ref[...].astype(BF16), w_ref[...], preferred_element_type=F32)
    krb = krp_ref[...].astype(BF16)
    ones_col = jnp.where(lax.broadcasted_iota(jnp.int32, krb.shape, 1) == 0, 1.0, 0.0).astype(BF16)
    v0 = MLA_HEADS * MLA_NOPE
    for hh in range(MLA_HEADS):
        k_ref[:, hh * HEAD_W:hh * HEAD_W + MLA_NOPE] = kvx[:, hh * MLA_NOPE:(hh + 1) * MLA_NOPE].astype(BF16)
        k_ref[:, hh * HEAD_W + MLA_NOPE:(hh + 1) * HEAD_W] = krb
        v_ref[:, hh * HEAD_W:hh * HEAD_W + MLA_V] = kvx[:, v0 + hh * MLA_V:v0 + (hh + 1) * MLA_V].astype(BF16)
        v_ref[:, hh * HEAD_W + MLA_V:(hh + 1) * HEAD_W] = ones_col


def _kv_expand(lat, krp, w_ukv_r, tm):
    t = lat.shape[0]
    rows = lambda w: pl.BlockSpec((tm, w), lambda i: (i, 0))
    return pl.pallas_call(
        _kv_expand_kernel,
        grid=(t // tm,),
        in_specs=[rows(KV_LORA), rows(LANES), _const_spec(w_ukv_r)],
        out_specs=[rows(MLA_HEADS * HEAD_W), rows(MLA_HEADS * HEAD_W)],
        out_shape=[jax.ShapeDtypeStruct((t, MLA_HEADS * HEAD_W), BF16),
                   jax.ShapeDtypeStruct((t, MLA_HEADS * HEAD_W), BF16)],
        compiler_params=_cparams(("parallel",)),
        name="mla_kv_expand",
    )(lat, krp, w_ukv_r)


def _q_proj_kernel(x_ref, sc_ref, sh_ref, wdq_ref, g_ref, wuq_ref, cos_ref, sin_ref, q_ref, *, qscale):
    h = (x_ref[...] * (1.0 + sc_ref[0]) + sh_ref[0]).astype(BF16)
    cq = jnp.dot(h, wdq_ref[...], preferred_element_type=F32)
    qn = (cq * lax.rsqrt(jnp.mean(cq * cq, axis=-1, keepdims=True) + RMS_EPS) * g_ref[...]).astype(BF16)
    cos = cos_ref[...]
    sin = sin_ref[...]
    nw = MLA_HEADS * LANES
    qnope = jnp.dot(qn, wuq_ref[:, :nw], preferred_element_type=F32)
    qpe = jnp.dot(qn, wuq_ref[:, nw:2 * nw], preferred_element_type=F32)
    qpe_sw = jnp.dot(qn, wuq_ref[:, 2 * nw:], preferred_element_type=F32)
    for hh in range(MLA_HEADS):
        cols = slice(hh * LANES, (hh + 1) * LANES)
        q_ref[:, hh * HEAD_W:hh * HEAD_W + LANES] = (qnope[:, cols] * qscale).astype(BF16)
        pe = qpe[:, cols] * cos + qpe_sw[:, cols] * sin
        q_ref[:, hh * HEAD_W + LANES:(hh + 1) * HEAD_W] = (pe * qscale).astype(BF16)


def _q_proj(tl, x, sc, sh, w_dq, q_g, w_uq_ext, cos, sin, qscale):
    kern = functools.partial(_q_proj_kernel, qscale=qscale)
    return pl.pallas_call(
        kern,
        grid=(tl.n,),
        in_specs=[tl.rows(D_MODEL), tl.mod_spec(sc), tl.mod_spec(sh), _const_spec(w_dq), _const_spec(q_g),
                  _const_spec(w_uq_ext), tl.table_spec(cos), tl.table_spec(sin)],
        out_specs=tl.rows(MLA_HEADS * HEAD_W),
        out_shape=jax.ShapeDtypeStruct((tl.t, MLA_HEADS * HEAD_W), BF16),
        compiler_params=_cparams(("parallel",)),
        name="mla_q_proj",
    )(x, sc, sh, w_dq, q_g, w_uq_ext, cos, sin)


def _attn_kernel(q_ref, k_ref, v_ref, o_ref, m_sc, acc_sc, sa_sc, sb_sc, *, tq, tk, q_pos0, n_keys, n_sub, hp):
    i = pl.program_id(2)
    m_sc[...] = jnp.full(m_sc.shape, NEG_INF, F32)
    acc_sc[...] = jnp.zeros(acc_sc.shape, F32)
    row0 = q_pos0 + i * tq
    cq0 = row0 // CHUNK
    cq1 = (row0 + tq - 1) // CHUNK
    n_full = jnp.minimum(((cq0 + 1) * CHUNK) // tk, n_keys // tk)
    n_blk = jnp.minimum(((cq1 + 1) * CHUNK + tk - 1) // tk, (n_keys + tk - 1) // tk)
    rs = tq // n_sub

    def scores(j, s_ref):
        for h in range(hp):
            hc = slice(h * HEAD_W, (h + 1) * HEAD_W)
            k = k_ref[0, pl.ds(pl.multiple_of(j * tk, tk), tk), hc]
            s_ref[h] = lax.dot_general(q_ref[0, :, hc], k, (((1,), (1,)), ((), ())),
                                       preferred_element_type=F32)

    def consume(j, s_ref, masked):
        for h in range(hp):
            v = v_ref[0, pl.ds(pl.multiple_of(j * tk, tk), tk), h * HEAD_W:(h + 1) * HEAD_W]
            for u in range(n_sub):
                rows = slice(u * rs, (u + 1) * rs)
                s = s_ref[h, rows, :]
                if masked:
                    qpos = row0 + u * rs + lax.broadcasted_iota(jnp.int32, (rs, 1), 0)
                    kpos = j * tk + lax.broadcasted_iota(jnp.int32, (1, tk), 1)
                    allowed = ((kpos >> LOG2_CHUNK) <= (qpos >> LOG2_CHUNK)) & (kpos < n_keys)
                    s = jnp.where(allowed, s, NEG_INF)
                m_old = m_sc[h, rows, :]
                m_new = jnp.maximum(m_old, jnp.max(s, axis=1, keepdims=True))
                alpha = jnp.exp2(m_old - m_new)
                p = jnp.exp2(s - jnp.tile(m_new, (1, tk // LANES)))
                pv = jnp.dot(p.astype(BF16), v, preferred_element_type=F32)
                acc_sc[h, rows, :] = jnp.tile(alpha, (1, HEAD_W // LANES)) * acc_sc[h, rows, :] + pv
                m_sc[h, rows, :] = m_new

    def pair_step(jj, carry):
        j = 2 * jj
        scores(j + 1, sb_sc)
        consume(j, sa_sc, False)
        scores(j + 2, sa_sc)
        consume(j + 1, sb_sc, False)
        return carry

    def single_step(j, carry):
        consume(j, sa_sc, True)
        scores(j + 1, sa_sc)
        return carry

    n_pairs = jnp.minimum(n_full, n_blk - 1) // 2
    left = n_blk - 2 * n_pairs
    tail = 2 - left % 2
    t0 = n_blk - tail
    scores(0, sa_sc)
    lax.fori_loop(0, n_pairs, pair_step, 0)
    lax.fori_loop(2 * n_pairs, t0, single_step, 0)

    @pl.when(tail == 2)
    def _():
        scores(t0 + 1, sb_sc)
        consume(t0, sa_sc, True)
        consume(t0 + 1, sb_sc, True)

    @pl.when(tail == 1)
    def _():
        consume(t0, sa_sc, True)

    for h in range(hp):
        acc = acc_sc[h]
        o_ref[0, :, h * MLA_V:(h + 1) * MLA_V] = (acc[:, :MLA_V] / acc[:, MLA_V:MLA_V + 1]).astype(BF16)


def _attention(qc, kc, vv, b, lq, lk, tq, tk, q_pos0, n_keys, hp):
    n_sub = 2 if tq % (2 * LANES) == 0 else 1
    kern = functools.partial(_attn_kernel, tq=tq, tk=tk, q_pos0=q_pos0, n_keys=n_keys, n_sub=n_sub, hp=hp)
    return pl.pallas_call(
        kern,
        grid=(b, MLA_HEADS // hp, lq // tq),
        in_specs=[pl.BlockSpec((1, tq, hp * HEAD_W), lambda bi, h, i: (bi, i, h)),
                  pl.BlockSpec((1, lk, hp * HEAD_W), lambda bi, h, i: (bi, 0, h)),
                  pl.BlockSpec((1, lk, hp * HEAD_W), lambda bi, h, i: (bi, 0, h))],
        out_specs=pl.BlockSpec((1, tq, hp * MLA_V), lambda bi, h, i: (bi, i, h)),
        out_shape=jax.ShapeDtypeStruct((b, lq, MLA_HEADS * MLA_V), BF16),
        scratch_shapes=[pltpu.VMEM((hp, tq, LANES), F32), pltpu.VMEM((hp, tq, HEAD_W), F32),
                        pltpu.VMEM((hp, tq, tk), F32), pltpu.VMEM((hp, tq, tk), F32)],
        compiler_params=_cparams(("parallel", "parallel", "arbitrary")),
        name="mla_attention",
    )(qc.reshape(b, lq, MLA_HEADS * HEAD_W), kc.reshape(b, lk, MLA_HEADS * HEAD_W),
      vv.reshape(b, lk, MLA_HEADS * HEAD_W))


def _rope_tables(pos, half):
    inv = ROPE_THETA ** (-jnp.arange(half, dtype=F32) / half)
    ang = pos.astype(F32)[:, None] * inv[None, :]
    return jnp.cos(ang), jnp.sin(ang)


def _mla_rope_tables(pos):
    cos, sin = _rope_tables(pos, MLA_ROPE // 2)
    z = jnp.zeros((pos.shape[0], LANES - MLA_ROPE), F32)
    return jnp.concatenate([cos, cos, z], axis=1), jnp.concatenate([-sin, sin, z], axis=1)


def _swap_halves(w):
    half = w.shape[-1] // 2
    return jnp.concatenate([w[..., half:], w[..., :half]], axis=-1)


def _pad_lanes(w):
    return jnp.pad(w, [(0, 0)] * (w.ndim - 1) + [(0, LANES - w.shape[-1])])


def _prep_weights(w_ret_in, w_ret_out, w_dq, q_norm_g, w_uq, w_mla_out, w_dkv, kv_norm_g, w_ukv,
                  w_route_group, b_route_group, w_route_expert, b_route_expert,
                  w_expert_gate_up, w_expert_down):
    p = {}
    p["w_ret_in"] = [w_ret_in[i].astype(BF16) for i in range(N_A)]
    p["w_ret_out"] = [w_ret_out[i].astype(BF16) for i in range(N_A)]
    p["w_dq"] = [w_dq[j].astype(BF16) for j in range(DEPTH - N_A)]
    p["q_norm_g"] = [q_norm_g[j][None, :] for j in range(DEPTH - N_A)]
    w_uq_ext = []
    for j in range(DEPTH - N_A):
        wq = w_uq[j].reshape(Q_LORA, MLA_HEADS, MLA_NOPE + MLA_ROPE)
        nope = wq[:, :, :MLA_NOPE].reshape(Q_LORA, MLA_HEADS * LANES)
        pe = wq[:, :, MLA_NOPE:]
        w_uq_ext.append(jnp.concatenate(
            [nope, _pad_lanes(pe).reshape(Q_LORA, MLA_HEADS * LANES),
             _pad_lanes(_swap_halves(pe)).reshape(Q_LORA, MLA_HEADS * LANES)], axis=1).astype(BF16))
    p["w_uq_ext"] = w_uq_ext
    p["w_mla_out"] = [w_mla_out[j].astype(BF16) for j in range(DEPTH - N_A)]
    kr = w_dkv[:, KV_LORA:]
    p["w_dkv_ext"] = jnp.concatenate([w_dkv[:, :KV_LORA], _pad_lanes(kr), _pad_lanes(_swap_halves(kr))],
                                     axis=1).astype(BF16)
    p["kv_norm_g"] = kv_norm_g[None, :]
    p["w_ukv_r"] = (w_ukv.reshape(KV_LORA, MLA_HEADS, 2, MLA_NOPE).transpose(0, 2, 1, 3)
                    .reshape(KV_LORA, 2 * MLA_HEADS * MLA_NOPE).astype(BF16))
    wr = jnp.concatenate([w_route_group, w_route_expert], axis=-1)
    wr = jnp.pad(wr, ((0, 0), (0, 0), (0, ROUTE_W - wr.shape[-1])))
    wr_hi = wr.astype(BF16)
    wr_lo = (wr - wr_hi.astype(F32)).astype(BF16)
    br = jnp.concatenate([b_route_group, b_route_expert], axis=-1)
    br = jnp.pad(br, ((0, 0), (0, ROUTE_W - br.shape[-1])))
    p["wr_hi"] = [wr_hi[l] for l in range(DEPTH)]
    p["wr_lo"] = [wr_lo[l] for l in range(DEPTH)]
    p["br"] = [br[l][None, :] for l in range(DEPTH)]
    p["w_gu"] = [w_expert_gate_up[l].astype(BF16) for l in range(DEPTH)]
    p["w_dn"] = [w_expert_down[l].astype(BF16) for l in range(DEPTH)]
    return p


def _trunk(x3, mod, pos0, ret_s0, past_latent, past_k_rope, ret_chunk, ret_block, ln_g, ln_b, p):
    b, l, _ = x3.shape
    tl = _Tiles(b, l)
    x = x3.reshape(tl.t, D_MODEL)
    pos = pos0 + jnp.arange(l, dtype=jnp.int32)
    new_ret = []
    latent_new = k_rope_new = None
    kc = vv = None
    lk = n_keys = None
    for layer in range(DEPTH):
        sh1, sc1, g1, sh2, sc2, g2 = [tl.mod(m) for m in jnp.split(mod[layer], 6, axis=-1)]
        lng = [ln_g[layer, s][None, :] for s in range(2)]
        lnb = [ln_b[layer, s][None, :] for s in range(2)]
        if layer < N_A:
            cos, sin = _rope_tables(pos, RET_DK // 2)
            q, k, v, sg = _ret_in(tl, x, sc1, sh1, p["w_ret_in"][layer], tl.table(cos), tl.table(sin))
            a, s_new = _retention(b, l, q, k, v, sg, ret_s0[layer], ret_chunk, ret_block)
            a = a.reshape(tl.t, RET_HEADS * RET_DV)
            new_ret.append(s_new)
            w_out = p["w_ret_out"][layer]
            name = "ret_out_ln_route"
        else:
            j = layer - N_A
            cos, sin = _mla_rope_tables(pos)
            cos, sin = tl.table(cos), tl.table(sin)
            if layer == N_A:
                latent_new, k_rope_new, krp = _kv_latent(tl, x, p["w_dkv_ext"], p["kv_norm_g"], cos, sin)
                lat_all = latent_new.reshape(b, l, KV_LORA)
                krp_all = krp.reshape(b, l, LANES)
                if past_latent is not None:
                    lat_all = jnp.concatenate([past_latent, lat_all], axis=1)
                    krp_all = jnp.concatenate([_pad_lanes(past_k_rope), krp_all], axis=1)
                n_keys = lat_all.shape[1]
                lk = -(-n_keys // LANES) * LANES
                if lk != n_keys:
                    lat_all = jnp.pad(lat_all, ((0, 0), (0, lk - n_keys), (0, 0)))
                    krp_all = jnp.pad(krp_all, ((0, 0), (0, lk - n_keys), (0, 0)))
                tk_rows = TOKEN_TILE if (b * lk) % TOKEN_TILE == 0 else 2 * LANES
                kc, vv = _kv_expand(lat_all.reshape(b * lk, KV_LORA), krp_all.reshape(b * lk, LANES),
                                    p["w_ukv_r"], tk_rows)
            qscale = float((MLA_NOPE + MLA_ROPE) ** -0.5 * math.log2(math.e))
            qc = _q_proj(tl, x, sc1, sh1, p["w_dq"][j], p["q_norm_g"][j], p["w_uq_ext"][j], cos, sin, qscale)
            if l % ATTN_TILE == 0:
                tq, tk, hp = ATTN_TILE, ATTN_TILE, 2
            else:
                tq, tk, hp = l, lk, 4
            a = _attention(qc, kc, vv, b, l, lk, tq, tk, pos0, n_keys, hp).reshape(tl.t, MLA_HEADS * MLA_V)
            w_out = p["w_mla_out"][j]
            name = "mla_out_ln_route"
        x1, gt, gi = _mix_out(tl, a, w_out, x, g1, lng[0], lnb[0], sc2, sh2,
                              p["wr_hi"][layer], p["wr_lo"][layer], p["br"][layer], name)
        x = _moe_ffn(tl, x1, sc2, sh2, gt, gi.reshape(tl.t), g2, lng[1], lnb[1],
                     p["w_gu"][layer], p["w_dn"][layer])
    return (x.reshape(b, l, D_MODEL), jnp.stack(new_ret), latent_new.reshape(b, l, KV_LORA),
            k_rope_new.reshape(b, l, MLA_ROPE))


def kernel(x_prompt, x_sample, state_retention, cache_kv_latent, cache_k_rope, c_prompt, c_sample,
           w_ada, b_ada, ln_g, ln_b, w_ret_in, w_ret_out, w_dq, q_norm_g, w_uq, w_mla_out,
           w_dkv, kv_norm_g, w_ukv, w_route_group, b_route_group, w_route_expert, b_route_expert,
           w_expert_gate_up, w_expert_down):
    bp, lp, _ = x_prompt.shape
    bs, ls, _ = x_sample.shape
    p = _prep_weights(w_ret_in, w_ret_out, w_dq, q_norm_g, w_uq, w_mla_out, w_dkv, kv_norm_g, w_ukv,
                      w_route_group, b_route_group, w_route_expert, b_route_expert,
                      w_expert_gate_up, w_expert_down)
    n_seq = bp + bs
    n_rows = -(-n_seq // 8) * 8
    c_all = jnp.concatenate([c_prompt, c_sample, jnp.zeros((n_rows - n_seq, D_MODEL), F32)], axis=0)
    mod = _ada(c_all, w_ada, b_ada)
    s0_p = jnp.zeros((N_A, bp, RET_HEADS, RET_DK, RET_DV), F32)
    ret_chunk_p = RET_CHUNK if lp % RET_CHUNK == 0 else CHUNK
    y_p, st_p, lat_p, kr_p = _trunk(x_prompt, mod[:, :bp], 0, s0_p, None, None,
                                    ret_chunk_p, max(ret_chunk_p, min(lp, TOKEN_TILE)), ln_g, ln_b, p)
    past = cache_kv_latent.shape[1]
    y_s, st_s, lat_s, kr_s = _trunk(x_sample, mod[:, bp:n_seq], past, state_retention,
                                    cache_kv_latent, cache_k_rope, ls, ls, ln_g, ln_b, p)
    return (y_p, y_s, st_p, st_s, lat_p, kr_p, lat_s, kr_s)
```

```python
import functools
import math

import numpy as np
import jax
import jax.numpy as jnp
from jax import lax
from jax.experimental import pallas as pl
from jax.experimental.pallas import tpu as pltpu

F32 = jnp.float32
BF16 = jnp.bfloat16

D_MODEL = 1024
DEPTH = 2
CHUNK = 64
LOG2_CHUNK = 6
N_A = DEPTH // 2
RET_HEADS = 4
RET_DK = 256
RET_DV = 512
MLA_HEADS = 8
MLA_NOPE = 128
MLA_ROPE = 64
MLA_V = 128
Q_LORA = 384
KV_LORA = 256
ROPE_THETA = 10000.0
N_GROUPS = 4
EXP_PER_GROUP = 4
N_EXPERTS = N_GROUPS * EXP_PER_GROUP
N_PAIRS = EXP_PER_GROUP * (EXP_PER_GROUP - 1) // 2
N_BUCKETS = N_GROUPS * N_PAIRS
PAIR_A = (0, 0, 0, 1, 1, 2)
PAIR_B = (1, 2, 3, 2, 3, 3)
EXPERT_FF = 512
LN_EPS = 1e-5
RMS_EPS = 1e-6
DN_ALPHA = (2 * DEPTH) ** 0.25
NEG_INF = -1e30

LANES = 128
ROUTE_W = LANES
TOKEN_TILE = 512
EXPERT_TILE = 256
RET_CHUNK = 256
ATTN_TILE = 512
ATTN_Q_TILE = 512
HEAD_W = 2 * LANES
VMEM_LIMIT = 56 * 1024 * 1024
ATTN_VMEM_LIMIT = 56 * 1024 * 1024


def _cparams(sem, vmem=VMEM_LIMIT):
    return pltpu.CompilerParams(dimension_semantics=sem, vmem_limit_bytes=vmem)


def _silu(x):
    return x * jax.nn.sigmoid(x)


def _layer_norm(z, g, b):
    mu = jnp.mean(z, axis=-1, keepdims=True)
    zc = z - mu
    var = jnp.mean(zc * zc, axis=-1, keepdims=True)
    return zc * lax.rsqrt(var + LN_EPS) * g + b


def _ada_kernel(c_ref, w_ref, b_ref, o_ref):
    s = _silu(c_ref[...]).astype(BF16)
    o_ref[0] = jnp.dot(s, w_ref[0].astype(BF16), preferred_element_type=F32) + b_ref[0]


def _ada(c_all, w_ada, b_ada):
    r = c_all.shape[0]
    n = w_ada.shape[-1]
    tn = 1536
    return pl.pallas_call(
        _ada_kernel,
        grid=(DEPTH, n // tn),
        in_specs=[pl.BlockSpec((r, D_MODEL), lambda l, j: (0, 0)),
                  pl.BlockSpec((1, D_MODEL, tn), lambda l, j: (l, 0, j)),
                  pl.BlockSpec((1, 1, tn), lambda l, j: (l, 0, j))],
        out_specs=pl.BlockSpec((1, r, tn), lambda l, j: (l, 0, j)),
        out_shape=jax.ShapeDtypeStruct((DEPTH, r, n), F32),
        compiler_params=_cparams(("parallel", "parallel")),
        name="ada_mod",
    )(c_all, w_ada, b_ada.reshape(DEPTH, 1, n))


class _Tiles:
    def __init__(self, b, l):
        self.b, self.l, self.t = b, l, b * l
        self.per_token = (l % TOKEN_TILE) != 0
        self.tm = self.t if self.per_token else TOKEN_TILE
        self.n = self.t // self.tm
        self.tiles_per_seq = 1 if self.per_token else l // self.tm

    def mod(self, m):
        if self.per_token:
            return jnp.repeat(m, self.l, axis=0)[None]
        return m[:, None, :]

    def mod_spec(self, arr):
        tps = self.tiles_per_seq
        return pl.BlockSpec((1,) + arr.shape[1:], lambda i: (i // tps, 0, 0))

    def table(self, tab):
        return jnp.tile(tab, (self.b, 1)) if self.per_token else tab

    def table_spec(self, tab):
        nt = tab.shape[0] // self.tm
        return pl.BlockSpec((self.tm, tab.shape[1]), lambda i: (i % nt, 0))

    def rows(self, w):
        return pl.BlockSpec((self.tm, w), lambda i: (i, 0))


def _const_spec(arr):
    nd = arr.ndim
    return pl.BlockSpec(arr.shape, lambda *_: (0,) * nd)


def _ret_in_kernel(x_ref, sc_ref, sh_ref, w_ref, cos_ref, sin_ref, q_ref, k_ref, v_ref, sg_ref):
    h = (x_ref[...] * (1.0 + sc_ref[0]) + sh_ref[0]).astype(BF16)
    cos = cos_ref[...]
    sin = sin_ref[...]
    qk_w = RET_HEADS * RET_DK
    half = RET_DK // 2

    def rope_store(r, out_ref, scale):
        for hh in range(RET_HEADS):
            a = r[:, hh * RET_DK:hh * RET_DK + half]
            b = r[:, hh * RET_DK + half:(hh + 1) * RET_DK]
            out_ref[:, hh * RET_DK:hh * RET_DK + half] = ((a * cos - b * sin) * scale).astype(BF16)
            out_ref[:, hh * RET_DK + half:(hh + 1) * RET_DK] = ((a * sin + b * cos) * scale).astype(BF16)

    r = jnp.dot(h, w_ref[:, 0:qk_w], preferred_element_type=F32)
    rope_store(r, q_ref, 1.0)
    r = jnp.dot(h, w_ref[:, qk_w:2 * qk_w], preferred_element_type=F32)
    rope_store(r, k_ref, RET_DK ** -0.5)
    for c in range(2):
        lo = 2 * qk_w + c * qk_w
        v_ref[:, c * qk_w:(c + 1) * qk_w] = jnp.dot(
            h, w_ref[:, lo:lo + qk_w], preferred_element_type=F32).astype(BF16)
    for c in range(2):
        lo = 4 * qk_w + c * qk_w
        g = jnp.dot(h, w_ref[:, lo:lo + qk_w], preferred_element_type=F32)
        sg_ref[:, c * qk_w:(c + 1) * qk_w] = _silu(g).astype(BF16)


def _ret_in(tl, x, sc, sh, w_in, cos, sin):
    qk_w, v_w = RET_HEADS * RET_DK, RET_HEADS * RET_DV
    return pl.pallas_call(
        _ret_in_kernel,
        grid=(tl.n,),
        in_specs=[tl.rows(D_MODEL), tl.mod_spec(sc), tl.mod_spec(sh), _const_spec(w_in),
                  tl.table_spec(cos), tl.table_spec(sin)],
        out_specs=[tl.rows(qk_w), tl.rows(qk_w), tl.rows(v_w), tl.rows(v_w)],
        out_shape=[jax.ShapeDtypeStruct((tl.t, qk_w), BF16), jax.ShapeDtypeStruct((tl.t, qk_w), BF16),
                   jax.ShapeDtypeStruct((tl.t, v_w), BF16), jax.ShapeDtypeStruct((tl.t, v_w), BF16)],
        compiler_params=_cparams(("parallel",)),
        name="ret_in_proj",
    )(x, sc, sh, w_in, cos, sin)


def _retention_tables(chunk):
    lg = np.log1p(-np.exp2(-5.0 - np.arange(RET_HEADS, dtype=np.float64)))
    idx = np.arange(chunk, dtype=np.float64)
    diff = idx[:, None] - idx[None, :]
    dmask = np.where(diff >= 0, np.exp(lg[:, None, None] * np.maximum(diff, 0.0)), 0.0)
    xi = np.exp(lg[:, None] * (idx[None, :] + 1.0))[:, :, None]
    zeta = np.exp(lg[:, None] * (chunk - 1.0 - idx[None, :]))[:, :, None]
    cdec = np.exp(lg * chunk)
    return (jnp.asarray(dmask, F32), jnp.asarray(xi, F32), jnp.asarray(zeta, F32),
            [float(np.float32(c)) for c in cdec])


def _retention_kernel(q_ref, k_ref, v_ref, sg_ref, s0_ref, dm_ref, xi_ref, zeta_ref,
                      o_ref, sout_ref, s_sc, *, chunk, n_chunks, cdec):
    j = pl.program_id(1)

    @pl.when(j == 0)
    def _():
        s_sc[...] = s0_ref[0]

    for c in range(n_chunks):
        rows = slice(c * chunk, (c + 1) * chunk)
        for hh in range(RET_HEADS):
            kcols = slice(hh * RET_DK, (hh + 1) * RET_DK)
            vcols = slice(hh * RET_DV, (hh + 1) * RET_DV)
            q = q_ref[0, rows, kcols]
            k = k_ref[0, rows, kcols]
            v = v_ref[0, rows, vcols]
            s_old = s_sc[hh]
            sc = lax.dot_general(q, k, (((1,), (1,)), ((), ())), preferred_element_type=F32) * dm_ref[hh]
            o = jnp.dot(sc.astype(BF16), v, preferred_element_type=F32)
            o = o + jnp.dot(q, s_old.astype(BF16), preferred_element_type=F32) * xi_ref[hh]
            kz_t = (k.astype(F32) * zeta_ref[hh]).T.astype(BF16)
            s_sc[hh] = s_old * cdec[hh] + jnp.dot(kz_t, v, preferred_element_type=F32)
            mu = jnp.mean(o, axis=-1, keepdims=True)
            oc = o - mu
            var = jnp.mean(oc * oc, axis=-1, keepdims=True)
            on = oc * lax.rsqrt(var + LN_EPS)
            o_ref[0, rows, vcols] = (sg_ref[0, rows, vcols].astype(F32) * on).astype(BF16)

    @pl.when(j == pl.num_programs(1) - 1)
    def _():
        sout_ref[0] = s_sc[...]


def _retention(b, l, q, k, v, sg, s0, chunk, block):
    qk_w, v_w = RET_HEADS * RET_DK, RET_HEADS * RET_DV
    dmask, xi, zeta, cdec = _retention_tables(chunk)
    kern = functools.partial(_retention_kernel, chunk=chunk, n_chunks=block // chunk, cdec=cdec)
    seq = lambda w: pl.BlockSpec((1, block, w), lambda bi, j: (bi, j, 0))
    st = pl.BlockSpec((1, RET_HEADS, RET_DK, RET_DV), lambda bi, j: (bi, 0, 0, 0))
    return pl.pallas_call(
        kern,
        grid=(b, l // block),
        in_specs=[seq(qk_w), seq(qk_w), seq(v_w), seq(v_w), st,
                  _const_spec(dmask), _const_spec(xi), _const_spec(zeta)],
        out_specs=[seq(v_w), st],
        out_shape=[jax.ShapeDtypeStruct((b, l, v_w), BF16),
                   jax.ShapeDtypeStruct((b, RET_HEADS, RET_DK, RET_DV), F32)],
        scratch_shapes=[pltpu.VMEM((RET_HEADS, RET_DK, RET_DV), F32)],
        compiler_params=_cparams(("parallel", "arbitrary")),
        name="retention_scan",
    )(q.reshape(b, l, qk_w), k.reshape(b, l, qk_w), v.reshape(b, l, v_w), sg.reshape(b, l, v_w),
      s0, dmask, xi, zeta)


def _route_rows(lt):
    g = [lt[i:i + 1] for i in range(N_GROUPS)]
    m = jnp.maximum(jnp.maximum(g[0], g[1]), jnp.maximum(g[2], g[3]))
    gi = jnp.where(g[0] == m, 0, jnp.where(g[1] == m, 1, jnp.where(g[2] == m, 2, 3))).astype(jnp.int32)
    denom = jnp.exp(g[0] - m) + jnp.exp(g[1] - m) + jnp.exp(g[2] - m) + jnp.exp(g[3] - m)
    p_group = 1.0 / denom
    le = []
    for e in range(EXP_PER_GROUP):
        r = [lt[N_GROUPS + gg * EXP_PER_GROUP + e:N_GROUPS + gg * EXP_PER_GROUP + e + 1]
             for gg in range(N_GROUPS)]
        le.append(jnp.where(gi == 0, r[0], jnp.where(gi == 1, r[1], jnp.where(gi == 2, r[2], r[3]))))
    me = jnp.maximum(jnp.maximum(le[0], le[1]), jnp.maximum(le[2], le[3]))
    ex = [jnp.exp(x - me) for x in le]

    def first_argmax(vals):
        mx = jnp.maximum(jnp.maximum(vals[0], vals[1]), jnp.maximum(vals[2], vals[3]))
        ix = jnp.where(vals[0] == mx, 0, jnp.where(vals[1] == mx, 1, jnp.where(vals[2] == mx, 2, 3)))
        return mx, ix.astype(jnp.int32)

    e1, i1 = first_argmax(ex)
    ex2 = [jnp.where(i1 == e, -1.0, ex[e]) for e in range(EXP_PER_GROUP)]
    e2, i2 = first_argmax(ex2)
    tot = e1 + e2
    w1 = e1 / tot * p_group
    w2 = e2 / tot * p_group
    first_low = i1 < i2
    a = jnp.where(first_low, i1, i2)
    b = jnp.where(first_low, i2, i1)
    pair = jnp.where(a == 0, b - 1, jnp.where(a == 1, b + 1, N_PAIRS - 1))
    bucket = gi * N_PAIRS + pair
    return bucket, jnp.where(first_low, w1, w2), jnp.where(first_low, w2, w1)


def _mix_out_kernel(a_ref, w_ref, x_ref, g1_ref, lng_ref, lnb_ref, sc2_ref, sh2_ref,
                    wrh_ref, wrl_ref, br_ref, x1_ref, gt_ref, gi_ref):
    y = jnp.dot(a_ref[...], w_ref[...], preferred_element_type=F32)
    x1 = _layer_norm(DN_ALPHA * x_ref[...] + g1_ref[0] * y, lng_ref[...], lnb_ref[...])
    x1_ref[...] = x1
    h2 = x1 * (1.0 + sc2_ref[0]) + sh2_ref[0]
    hi = h2.astype(BF16)
    lo = (h2 - hi.astype(F32)).astype(BF16)
    wrh = wrh_ref[...]
    logits = (jnp.dot(hi, wrh, preferred_element_type=F32)
              + (jnp.dot(hi, wrl_ref[...], preferred_element_type=F32)
                 + jnp.dot(lo, wrh, preferred_element_type=F32))) + br_ref[...]
    lt = logits.T
    bucket, gate_a, gate_b = _route_rows(lt)
    gi_ref[0] = bucket
    tm = lt.shape[1]
    row = lax.broadcasted_iota(jnp.int32, (8, tm), 0)
    g8 = jnp.where(row == 0, gate_a, jnp.where(row == 1, gate_b, 0.0))
    gt = jnp.concatenate([g8, jnp.zeros((ROUTE_W - 8, tm), F32)], axis=0)
    gt_ref[...] = gt.T


def _mix_out(tl, a, w, x, g1, lng, lnb, sc2, sh2, wrh, wrl, br, name):
    kd = a.shape[1]
    return pl.pallas_call(
        _mix_out_kernel,
        grid=(tl.n,),
        in_specs=[tl.rows(kd), _const_spec(w), tl.rows(D_MODEL), tl.mod_spec(g1),
                  _const_spec(lng), _const_spec(lnb), tl.mod_spec(sc2), tl.mod_spec(sh2),
                  _const_spec(wrh), _const_spec(wrl), _const_spec(br)],
        out_specs=[tl.rows(D_MODEL), tl.rows(ROUTE_W),
                   pl.BlockSpec((1, 1, tl.tm), lambda i: (i, 0, 0))],
        out_shape=[jax.ShapeDtypeStruct((tl.t, D_MODEL), F32),
                   jax.ShapeDtypeStruct((tl.t, ROUTE_W), F32),
                   jax.ShapeDtypeStruct((tl.n, 1, tl.tm), jnp.int32)],
        compiler_params=_cparams(("parallel",)),
        name=name,
    )(a, w, x, g1, lng, lnb, sc2, sh2, wrh, wrl, br)


TOK_SUB = D_MODEL // LANES


def _moe_sort_kernel(idx_ref, x1_ref, sc2_ref, sh2_ref, gt_ref, tok_ref, gs_ref, buf, gs_sc, sem, gsem, *, tm):
    i = pl.program_id(0)
    n = pl.num_programs(0)
    slot = i % 2

    def wait_slot(sl):
        pltpu.make_async_copy(buf.at[sl], tok_ref.at[pl.ds(0, tm * TOK_SUB)], sem.at[sl]).wait()

    @pl.when(i >= 2)
    def _():
        wait_slot(slot)

    h2 = x1_ref[...] * (1.0 + sc2_ref[0]) + sh2_ref[0]
    for s in range(TOK_SUB):
        buf[slot, pl.ds(s, tm, stride=TOK_SUB), :] = h2[:, s * LANES:(s + 1) * LANES]

    base = i * tm

    def issue(r, carry):
        d = idx_ref[base + r]
        pltpu.make_async_copy(buf.at[slot, pl.ds(pl.multiple_of(r * TOK_SUB, TOK_SUB), TOK_SUB)],
                              tok_ref.at[pl.ds(pl.multiple_of(d * TOK_SUB, TOK_SUB), TOK_SUB)],
                              sem.at[slot]).start()
        gs_sc[pl.ds(d, 1), :] = gt_ref[pl.ds(r, 1), :]
        return carry

    lax.fori_loop(0, tm, issue, 0, unroll=8)

    @pl.when(i == n - 1)
    def _():
        gates_out = pltpu.make_async_copy(gs_sc, gs_ref, gsem)
        gates_out.start()

        @pl.when(n >= 2)
        def _():
            wait_slot(1 - slot)
        wait_slot(slot)
        gates_out.wait()


def _moe_sort(tl, x1, sc2, sh2, gt, dst_idx, n_rows):
    tm = tl.tm
    n_steps = dst_idx.shape[0] // tm
    last = tl.n - 1
    tps = tl.tiles_per_seq
    kern = functools.partial(_moe_sort_kernel, tm=tm)
    rows = lambda w: pl.BlockSpec((tm, w), lambda i, idx: (jnp.minimum(i, last), 0))
    mod = lambda arr: pl.BlockSpec((1,) + arr.shape[1:], lambda i, idx: (jnp.minimum(i, last) // tps, 0, 0))
    return pl.pallas_call(
        kern,
        grid_spec=pltpu.PrefetchScalarGridSpec(
            num_scalar_prefetch=1,
            grid=(n_steps,),
            in_specs=[rows(D_MODEL), mod(sc2), mod(sh2), rows(ROUTE_W)],
            out_specs=[pl.BlockSpec(memory_space=pl.ANY), pl.BlockSpec(memory_space=pl.ANY)],
            scratch_shapes=[pltpu.VMEM((2, tm * TOK_SUB, LANES), F32), pltpu.VMEM((n_rows, ROUTE_W), F32),
                            pltpu.SemaphoreType.DMA((2,)), pltpu.SemaphoreType.DMA(())]),
        out_shape=[jax.ShapeDtypeStruct((n_rows * TOK_SUB, LANES), F32),
                   jax.ShapeDtypeStruct((n_rows, ROUTE_W), F32)],
        compiler_params=_cparams(("arbitrary",)),
        name="moe_sort_rows",
    )(dst_idx, x1, sc2, sh2, gt)


def _moe_kernel(ea_ref, eb_ref, nv_ref, tok_ref, gs_ref, wgu_a_ref, wgu_b_ref, wdn_a_ref, wdn_b_ref, o_ref,
                *, tme):
    i = pl.program_id(0)

    @pl.when(i < nv_ref[0])
    def _():
        hb = jnp.concatenate([tok_ref[pl.ds(s, tme, stride=TOK_SUB), :] for s in range(TOK_SUB)],
                             axis=1).astype(BF16)
        acc = None
        for lane, (wgu_ref, wdn_ref) in enumerate(((wgu_a_ref, wdn_a_ref), (wgu_b_ref, wdn_b_ref))):
            au = jnp.dot(hb, wgu_ref[0], preferred_element_type=F32)
            act = (_silu(au[:, :EXPERT_FF]) * au[:, EXPERT_FF:]).astype(BF16)
            y = jnp.dot(act, wdn_ref[0], preferred_element_type=F32)
            gy = gs_ref[:, lane:lane + 1] * y
            acc = gy if acc is None else acc + gy
        for s in range(TOK_SUB):
            o_ref[pl.ds(s, tme, stride=TOK_SUB), :] = acc[:, s * LANES:(s + 1) * LANES]

    @pl.when(i >= nv_ref[0])
    def _():
        o_ref[...] = jnp.zeros_like(o_ref)


def _moe_experts(tok, gs, tile_ea, tile_eb, n_valid, wgu, wdn, tme):
    n_tiles = gs.shape[0] // tme
    kern = functools.partial(_moe_kernel, tme=tme)
    gu = lambda pick: pl.BlockSpec((1, D_MODEL, 2 * EXPERT_FF), lambda i, ea, eb, nv: (pick(ea, eb)[i], 0, 0))
    dn = lambda pick: pl.BlockSpec((1, EXPERT_FF, D_MODEL), lambda i, ea, eb, nv: (pick(ea, eb)[i], 0, 0))
    first = lambda ea, eb: ea
    second = lambda ea, eb: eb
    return pl.pallas_call(
        kern,
        grid_spec=pltpu.PrefetchScalarGridSpec(
            num_scalar_prefetch=3,
            grid=(n_tiles,),
            in_specs=[pl.BlockSpec((tme * TOK_SUB, LANES), lambda i, ea, eb, nv: (i, 0)),
                      pl.BlockSpec((tme, ROUTE_W), lambda i, ea, eb, nv: (i, 0)),
                      gu(first), gu(second), dn(first), dn(second)],
            out_specs=pl.BlockSpec((tme * TOK_SUB, LANES), lambda i, ea, eb, nv: (i, 0))),
        out_shape=jax.ShapeDtypeStruct(tok.shape, F32),
        compiler_params=_cparams(("arbitrary",)),
        name="moe_experts",
    )(tile_ea, tile_eb, n_valid, tok, gs, wgu, wgu, wdn, wdn)


def _ln_res_kernel(idx_ref, fs_ref, x_ref, g_ref, lng_ref, lnb_ref, o_ref, buf, sem, *, tm):
    i = pl.program_id(0)
    n = pl.num_programs(0)
    slot = i % 2

    def issue(step, sl):
        base = step * tm

        def body(r, carry):
            d = idx_ref[base + r]
            pltpu.make_async_copy(fs_ref.at[pl.ds(pl.multiple_of(d * TOK_SUB, TOK_SUB), TOK_SUB)],
                                  buf.at[sl, pl.ds(pl.multiple_of(r * TOK_SUB, TOK_SUB), TOK_SUB)],
                                  sem.at[sl]).start()
            return carry

        lax.fori_loop(0, tm, body, 0, unroll=8)

    @pl.when(i == 0)
    def _():
        issue(0, 0)

    @pl.when(i + 1 < n)
    def _():
        issue(i + 1, 1 - slot)

    pltpu.make_async_copy(fs_ref.at[pl.ds(0, tm * TOK_SUB)], buf.at[slot], sem.at[slot]).wait()
    f = jnp.concatenate([buf[slot, pl.ds(s, tm, stride=TOK_SUB), :] for s in range(TOK_SUB)], axis=1)
    o_ref[...] = _layer_norm(DN_ALPHA * x_ref[...] + g_ref[0] * f, lng_ref[...], lnb_ref[...])


def _ln_res(tl, x, fs, dest, g2, lng, lnb):
    tm = tl.tm
    tps = tl.tiles_per_seq
    kern = functools.partial(_ln_res_kernel, tm=tm)
    const = lambda arr: pl.BlockSpec(arr.shape, lambda i, idx: (0,) * arr.ndim)
    return pl.pallas_call(
        kern,
        grid_spec=pltpu.PrefetchScalarGridSpec(
            num_scalar_prefetch=1,
            grid=(tl.n,),
            in_specs=[pl.BlockSpec(memory_space=pl.ANY),
                      pl.BlockSpec((tm, D_MODEL), lambda i, idx: (i, 0)),
                      pl.BlockSpec((1,) + g2.shape[1:], lambda i, idx: (i // tps, 0, 0)),
                      const(lng), const(lnb)],
            out_specs=pl.BlockSpec((tm, D_MODEL), lambda i, idx: (i, 0)),
            scratch_shapes=[pltpu.VMEM((2, tm * TOK_SUB, LANES), F32), pltpu.SemaphoreType.DMA((2,))]),
        out_shape=jax.ShapeDtypeStruct((tl.t, D_MODEL), F32),
        compiler_params=_cparams(("arbitrary",)),
        name="ffn_residual_ln",
    )(dest, fs, x, g2, lng, lnb)


def _moe_ffn(tl, x1, sc2, sh2, gt, gi, g2, lng, lnb, wgu, wdn, layer):
    t = tl.t
    tme = EXPERT_TILE if t % TOKEN_TILE == 0 else 64
    n_tiles = t // tme + N_BUCKETS
    n_pad = N_BUCKETS * tme
    assert n_pad % tl.tm == 0
    i32 = jnp.int32
    onehot = (gi[:, None] == jnp.arange(N_BUCKETS, dtype=i32)[None, :]).astype(i32)
    counts = jnp.sum(onehot, axis=0)
    padded = ((counts + tme - 1) // tme) * tme
    ends = jnp.cumsum(padded)
    starts = ends - padded
    dest = jnp.sum(onehot * (jnp.cumsum(onehot, axis=0) - onehot + starts[None, :]), axis=1).astype(i32)
    cpad = jnp.cumsum(padded - counts)
    k = jnp.arange(n_pad, dtype=i32)
    seg_base = jnp.concatenate([starts + counts, ends[-1:]])
    seg_first = jnp.concatenate([jnp.zeros((1,), cpad.dtype), cpad])
    seg_hot = ((k[:, None] >= seg_first[None, :])
               & (k[:, None] < jnp.concatenate([cpad, jnp.full((1,), n_pad, cpad.dtype)])[None, :])).astype(i32)
    pad_rows = (k + jnp.sum(seg_hot * (seg_base - seg_first)[None, :], axis=1)).astype(i32)
    tile_start = jnp.arange(n_tiles, dtype=i32) * tme
    tile_hot = ((tile_start[:, None] >= starts[None, :]) & (tile_start[:, None] < ends[None, :])).astype(i32)
    bucket_ids = jnp.arange(N_BUCKETS, dtype=i32)
    first_expert = layer * N_EXPERTS + (bucket_ids // N_PAIRS) * EXP_PER_GROUP
    pair_a = jnp.asarray(PAIR_A * N_GROUPS, i32)
    pair_b = jnp.asarray(PAIR_B * N_GROUPS, i32)
    last_expert = layer * N_EXPERTS + N_EXPERTS - 1
    in_use = jnp.sum(tile_hot, axis=1)
    tile_ea = (jnp.sum(tile_hot * (first_expert + pair_a)[None, :], axis=1) + (1 - in_use) * last_expert).astype(i32)
    tile_eb = (jnp.sum(tile_hot * (first_expert + pair_b)[None, :], axis=1) + (1 - in_use) * last_expert).astype(i32)
    n_valid = (ends[-1] // tme).astype(i32).reshape(1)
    tok, gs = _moe_sort(tl, x1, sc2, sh2, gt, jnp.concatenate([dest, pad_rows]), n_tiles * tme)
    fs = _moe_experts(tok, gs, tile_ea, tile_eb, n_valid, wgu, wdn, tme)
    return _ln_res(tl, x1, fs, dest, g2, lng, lnb)


def _kv_latent_kernel(x_ref, w_ref, g_ref, cos_ref, sin_ref, lat_ref, kr_ref, krp_ref):
    kv = jnp.dot(x_ref[...].astype(BF16), w_ref[...], preferred_element_type=F32)
    c = kv[:, :KV_LORA]
    lat_ref[...] = c * lax.rsqrt(jnp.mean(c * c, axis=-1, keepdims=True) + RMS_EPS) * g_ref[...]
    kr = kv[:, KV_LORA:KV_LORA + LANES] * cos_ref[...] + kv[:, KV_LORA + LANES:] * sin_ref[...]
    krp_ref[...] = kr
    kr_ref[...] = kr[:, :MLA_ROPE]


def _kv_latent(tl, x, w_dkv_ext, kv_g, cos, sin):
    return pl.pallas_call(
        _kv_latent_kernel,
        grid=(tl.n,),
        in_specs=[tl.rows(D_MODEL), _const_spec(w_dkv_ext), _const_spec(kv_g),
                  tl.table_spec(cos), tl.table_spec(sin)],
        out_specs=[tl.rows(KV_LORA), tl.rows(MLA_ROPE), tl.rows(LANES)],
        out_shape=[jax.ShapeDtypeStruct((tl.t, KV_LORA), F32), jax.ShapeDtypeStruct((tl.t, MLA_ROPE), F32),
                   jax.ShapeDtypeStruct((tl.t, LANES), F32)],
        compiler_params=_cparams(("parallel",)),
        name="mla_kv_latent",
    )(x, w_dkv_ext, kv_g, cos, sin)


def _kv_expand_kernel(lat_ref, krp_ref, w_ref, k_ref, v_ref):
    kvx = jnp.dot(lat_ref[...].astype(BF16), w_ref[...], preferred_element_type=F32)
    krb = krp_ref[...].astype(BF16)
    ones_col = jnp.where(lax.broadcasted_iota(jnp.int32, krb.shape, 1) == 0, 1.0, 0.0).astype(BF16)
    v0 = MLA_HEADS * MLA_NOPE
    for hh in range(MLA_HEADS):
        k_ref[:, hh * HEAD_W:hh * HEAD_W + MLA_NOPE] = kvx[:, hh * MLA_NOPE:(hh + 1) * MLA_NOPE].astype(BF16)
        k_ref[:, hh * HEAD_W + MLA_NOPE:(hh + 1) * HEAD_W] = krb
        v_ref[:, hh * HEAD_W:hh * HEAD_W + MLA_V] = kvx[:, v0 + hh * MLA_V:v0 + (hh + 1) * MLA_V].astype(BF16)
        v_ref[:, hh * HEAD_W + MLA_V:(hh + 1) * HEAD_W] = ones_col


def _kv_expand(lat, krp, w_ukv_r, tm):
    t = lat.shape[0]
    rows = lambda w: pl.BlockSpec((tm, w), lambda i: (i, 0))
    return pl.pallas_call(
        _kv_expand_kernel,
        grid=(t // tm,),
        in_specs=[rows(KV_LORA), rows(LANES), _const_spec(w_ukv_r)],
        out_specs=[rows(MLA_HEADS * HEAD_W), rows(MLA_HEADS * HEAD_W)],
        out_shape=[jax.ShapeDtypeStruct((t, MLA_HEADS * HEAD_W), BF16),
                   jax.ShapeDtypeStruct((t, MLA_HEADS * HEAD_W), BF16)],
        compiler_params=_cparams(("parallel",)),
        name="mla_kv_expand",
    )(lat, krp, w_ukv_r)


def _q_proj_kernel(x_ref, sc_ref, sh_ref, wdq_ref, g_ref, wuq_ref, cos_ref, sin_ref, q_ref, *, qscale):
    h = (x_ref[...] * (1.0 + sc_ref[0]) + sh_ref[0]).astype(BF16)
    cq = jnp.dot(h, wdq_ref[...], preferred_element_type=F32)
    qn = (cq * lax.rsqrt(jnp.mean(cq * cq, axis=-1, keepdims=True) + RMS_EPS) * g_ref[...]).astype(BF16)
    cos = cos_ref[...]
    sin = sin_ref[...]
    nw = MLA_HEADS * LANES
    qnope = jnp.dot(qn, wuq_ref[:, :nw], preferred_element_type=F32)
    qpe = jnp.dot(qn, wuq_ref[:, nw:2 * nw], preferred_element_type=F32)
    qpe_sw = jnp.dot(qn, wuq_ref[:, 2 * nw:], preferred_element_type=F32)
    for hh in range(MLA_HEADS):
        cols = slice(hh * LANES, (hh + 1) * LANES)
        q_ref[:, hh * HEAD_W:hh * HEAD_W + LANES] = (qnope[:, cols] * qscale).astype(BF16)
        pe = qpe[:, cols] * cos + qpe_sw[:, cols] * sin
        q_ref[:, hh * HEAD_W + LANES:(hh + 1) * HEAD_W] = (pe * qscale).astype(BF16)


def _q_proj(tl, x, sc, sh, w_dq, q_g, w_uq_ext, cos, sin, qscale):
    kern = functools.partial(_q_proj_kernel, qscale=qscale)
    return pl.pallas_call(
        kern,
        grid=(tl.n,),
        in_specs=[tl.rows(D_MODEL), tl.mod_spec(sc), tl.mod_spec(sh), _const_spec(w_dq), _const_spec(q_g),
                  _const_spec(w_uq_ext), tl.table_spec(cos), tl.table_spec(sin)],
        out_specs=tl.rows(MLA_HEADS * HEAD_W),
        out_shape=jax.ShapeDtypeStruct((tl.t, MLA_HEADS * HEAD_W), BF16),
        compiler_params=_cparams(("parallel",)),
        name="mla_q_proj",
    )(x, sc, sh, w_dq, q_g, w_uq_ext, cos, sin)


def _attn_kernel(q_ref, k_ref, v_ref, o_ref, m_sc, acc_sc, sa_sc, sb_sc, *, tq, tk, q_pos0, n_keys, n_sub, hp):
    i = pl.program_id(2)
    m_sc[...] = jnp.full(m_sc.shape, NEG_INF, F32)
    acc_sc[...] = jnp.zeros(acc_sc.shape, F32)
    row0 = q_pos0 + i * tq
    cq0 = row0 // CHUNK
    cq1 = (row0 + tq - 1) // CHUNK
    n_full = jnp.minimum(((cq0 + 1) * CHUNK) // tk, n_keys // tk)
    n_blk = jnp.minimum(((cq1 + 1) * CHUNK + tk - 1) // tk, (n_keys + tk - 1) // tk)
    rs = tq // n_sub

    def scores(j, s_ref):
        for h in range(hp):
            hc = slice(h * HEAD_W, (h + 1) * HEAD_W)
            k = k_ref[0, pl.ds(pl.multiple_of(j * tk, tk), tk), hc]
            s_ref[h] = lax.dot_general(q_ref[0, :, hc], k, (((1,), (1,)), ((), ())),
                                       preferred_element_type=F32)

    def consume(j, s_ref, masked):
        for h in range(hp):
            v = v_ref[0, pl.ds(pl.multiple_of(j * tk, tk), tk), h * HEAD_W:(h + 1) * HEAD_W]
            for u in range(n_sub):
                rows = slice(u * rs, (u + 1) * rs)
                s = s_ref[h, rows, :]
                if masked:
                    qpos = row0 + u * rs + lax.broadcasted_iota(jnp.int32, (rs, 1), 0)
                    kpos = j * tk + lax.broadcasted_iota(jnp.int32, (1, tk), 1)
                    allowed = ((kpos >> LOG2_CHUNK) <= (qpos >> LOG2_CHUNK)) & (kpos < n_keys)
                    s = jnp.where(allowed, s, NEG_INF)
                m_old = m_sc[h, rows, :]
                m_new = jnp.maximum(m_old, jnp.max(s, axis=1, keepdims=True))
                alpha = jnp.exp2(m_old - m_new)
                p = jnp.exp2(s - jnp.tile(m_new, (1, tk // LANES)))
                pv = jnp.dot(p.astype(BF16), v, preferred_element_type=F32)
                acc_sc[h, rows, :] = jnp.tile(alpha, (1, HEAD_W // LANES)) * acc_sc[h, rows, :] + pv
                m_sc[h, rows, :] = m_new

    def pair_step(jj, carry):
        j = 2 * jj
        scores(j + 1, sb_sc)
        consume(j, sa_sc, False)
        scores(j + 2, sa_sc)
        consume(j + 1, sb_sc, False)
        return carry

    def single_step(j, carry):
        consume(j, sa_sc, True)
        scores(j + 1, sa_sc)
        return carry

    n_pairs = jnp.minimum(n_full, n_blk - 1) // 2
    left = n_blk - 2 * n_pairs
    tail = 2 - left % 2
    t0 = n_blk - tail
    scores(0, sa_sc)
    lax.fori_loop(0, n_pairs, pair_step, 0)
    lax.fori_loop(2 * n_pairs, t0, single_step, 0)

    @pl.when(tail == 2)
    def _():
        scores(t0 + 1, sb_sc)
        consume(t0, sa_sc, True)
        consume(t0 + 1, sb_sc, True)

    @pl.when(tail == 1)
    def _():
        consume(t0, sa_sc, True)

    for h in range(hp):
        acc = acc_sc[h]
        o_ref[0, :, h * MLA_V:(h + 1) * MLA_V] = (acc[:, :MLA_V] / acc[:, MLA_V:MLA_V + 1]).astype(BF16)


def _attention(qc, kc, vv, b, lq, lk, tq, tk, q_pos0, n_keys, hp):
    n_sub = 2 if tq % (2 * LANES) == 0 else 1
    kern = functools.partial(_attn_kernel, tq=tq, tk=tk, q_pos0=q_pos0, n_keys=n_keys, n_sub=n_sub, hp=hp)
    return pl.pallas_call(
        kern,
        grid=(b, MLA_HEADS // hp, lq // tq),
        in_specs=[pl.BlockSpec((1, tq, hp * HEAD_W), lambda bi, h, i: (bi, i, h)),
                  pl.BlockSpec((1, lk, hp * HEAD_W), lambda bi, h, i: (bi, 0, h)),
                  pl.BlockSpec((1, lk, hp * HEAD_W), lambda bi, h, i: (bi, 0, h))],
        out_specs=pl.BlockSpec((1, tq, hp * MLA_V), lambda bi, h, i: (bi, i, h)),
        out_shape=jax.ShapeDtypeStruct((b, lq, MLA_HEADS * MLA_V), BF16),
        scratch_shapes=[pltpu.VMEM((hp, tq, LANES), F32), pltpu.VMEM((hp, tq, HEAD_W), F32),
                        pltpu.VMEM((hp, tq, tk), F32), pltpu.VMEM((hp, tq, tk), F32)],
        compiler_params=_cparams(("parallel", "parallel", "arbitrary"), ATTN_VMEM_LIMIT),
        name="mla_attention",
    )(qc.reshape(b, lq, MLA_HEADS * HEAD_W), kc.reshape(b, lk, MLA_HEADS * HEAD_W),
      vv.reshape(b, lk, MLA_HEADS * HEAD_W))


def _decode_attn_kernel(q_ref, lnew_ref, knew_ref, lpast_ref, kpast_ref, wukt_ref, wuv_ref, o_ref, qa_sc,
                        *, l, pos0, past):
    nope_w = KV_LORA
    for h in range(MLA_HEADS):
        rows = slice(h * l, (h + 1) * l)
        qn = q_ref[0, :, h * HEAD_W:h * HEAD_W + MLA_NOPE]
        qa_sc[rows, :nope_w] = jnp.dot(qn, wukt_ref[h], preferred_element_type=F32).astype(BF16)
        qa_sc[rows, nope_w:] = q_ref[0, :, h * HEAD_W + MLA_NOPE:(h + 1) * HEAD_W]
    q_lat = qa_sc[:, :nope_w]
    q_pe = qa_sc[:, nope_w:nope_w + MLA_ROPE]
    nt = (((1,), (1,)), ((), ()))
    lp = lpast_ref[0].astype(BF16)
    kp = kpast_ref[0].astype(BF16)
    ln = lnew_ref[0].astype(BF16)
    kn = knew_ref[0][:, :MLA_ROPE].astype(BF16)
    s_past = (lax.dot_general(q_lat, lp, nt, preferred_element_type=F32)
              + lax.dot_general(q_pe, kp, nt, preferred_element_type=F32))
    s_new = (lax.dot_general(q_lat, ln, nt, preferred_element_type=F32)
             + lax.dot_general(q_pe, kn, nt, preferred_element_type=F32))
    n_rows = MLA_HEADS * l
    qpos = pos0 + lax.rem(lax.broadcasted_iota(jnp.int32, (n_rows, 1), 0), l)
    kpos_new = pos0 + lax.broadcasted_iota(jnp.int32, (1, l), 1)
    s_new = jnp.where((kpos_new >> LOG2_CHUNK) <= (qpos >> LOG2_CHUNK), s_new, NEG_INF)
    kpos_past = lax.broadcasted_iota(jnp.int32, (1, past), 1)
    s_past = jnp.where((kpos_past >> LOG2_CHUNK) <= (qpos >> LOG2_CHUNK), s_past, NEG_INF)
    m = jnp.maximum(jnp.max(s_past, axis=1, keepdims=True), jnp.max(s_new, axis=1, keepdims=True))
    p_past = jnp.exp2(s_past - m)
    p_new = jnp.exp2(s_new - m)
    denom = jnp.sum(p_past, axis=1, keepdims=True) + jnp.sum(p_new, axis=1, keepdims=True)
    ctx = (jnp.dot(p_past.astype(BF16), lp, preferred_element_type=F32)
           + jnp.dot(p_new.astype(BF16), ln, preferred_element_type=F32)) / denom
    ctx = ctx.astype(BF16)
    for h in range(MLA_HEADS):
        o_ref[0, :, h * MLA_V:(h + 1) * MLA_V] = jnp.dot(
            ctx[h * l:(h + 1) * l], wuv_ref[h], preferred_element_type=F32).astype(BF16)


def _decode_attention(qc, latent_new, krp_new, past_latent, past_k_rope, w_uk_t, w_uv, b, l, pos0):
    past = past_latent.shape[1]
    kern = functools.partial(_decode_attn_kernel, l=l, pos0=pos0, past=past)
    per_b = lambda shape: pl.BlockSpec((1,) + shape, lambda bi: (bi, 0, 0))
    return pl.pallas_call(
        kern,
        grid=(b,),
        in_specs=[per_b((l, MLA_HEADS * HEAD_W)), per_b((l, KV_LORA)), per_b((l, LANES)),
                  per_b((past, KV_LORA)), per_b((past, MLA_ROPE)), _const_spec(w_uk_t), _const_spec(w_uv)],
        out_specs=per_b((l, MLA_HEADS * MLA_V)),
        out_shape=jax.ShapeDtypeStruct((b, l, MLA_HEADS * MLA_V), BF16),
        scratch_shapes=[pltpu.VMEM((MLA_HEADS * l, KV_LORA + LANES), BF16)],
        compiler_params=_cparams(("parallel",)),
        name="mla_decode_attention",
    )(qc.reshape(b, l, MLA_HEADS * HEAD_W), latent_new.reshape(b, l, KV_LORA), krp_new.reshape(b, l, LANES),
      past_latent, past_k_rope, w_uk_t, w_uv)


def _rope_tables(pos0, length, half):
    inv = ROPE_THETA ** (-np.arange(half, dtype=np.float64) / half)
    ang = (pos0 + np.arange(length, dtype=np.float64))[:, None] * inv[None, :]
    return np.cos(ang), np.sin(ang)


def _mla_rope_tables(pos0, length):
    cos, sin = _rope_tables(pos0, length, MLA_ROPE // 2)
    z = np.zeros((length, LANES - MLA_ROPE))
    return (jnp.asarray(np.concatenate([cos, cos, z], axis=1), F32),
            jnp.asarray(np.concatenate([-sin, sin, z], axis=1), F32))


def _swap_halves(w):
    half = w.shape[-1] // 2
    return jnp.concatenate([w[..., half:], w[..., :half]], axis=-1)


def _pad_lanes(w):
    return jnp.pad(w, [(0, 0)] * (w.ndim - 1) + [(0, LANES - w.shape[-1])])


def _prep_weights(w_ret_in, w_ret_out, w_dq, q_norm_g, w_uq, w_mla_out, w_dkv, kv_norm_g, w_ukv,
                  w_route_group, b_route_group, w_route_expert, b_route_expert,
                  w_expert_gate_up, w_expert_down):
    p = {}
    p["w_ret_in"] = [w_ret_in[i].astype(BF16) for i in range(N_A)]
    p["w_ret_out"] = [w_ret_out[i].astype(BF16) for i in range(N_A)]
    p["w_dq"] = [w_dq[j].astype(BF16) for j in range(DEPTH - N_A)]
    p["q_norm_g"] = [q_norm_g[j][None, :] for j in range(DEPTH - N_A)]
    w_uq_ext = []
    for j in range(DEPTH - N_A):
        wq = w_uq[j].reshape(Q_LORA, MLA_HEADS, MLA_NOPE + MLA_ROPE)
        nope = wq[:, :, :MLA_NOPE].reshape(Q_LORA, MLA_HEADS * LANES)
        pe = wq[:, :, MLA_NOPE:]
        w_uq_ext.append(jnp.concatenate(
            [nope, _pad_lanes(pe).reshape(Q_LORA, MLA_HEADS * LANES),
             _pad_lanes(_swap_halves(pe)).reshape(Q_LORA, MLA_HEADS * LANES)], axis=1).astype(BF16))
    p["w_uq_ext"] = w_uq_ext
    p["w_mla_out"] = [w_mla_out[j].astype(BF16) for j in range(DEPTH - N_A)]
    kr = w_dkv[:, KV_LORA:]
    p["w_dkv_ext"] = jnp.concatenate([w_dkv[:, :KV_LORA], _pad_lanes(kr), _pad_lanes(_swap_halves(kr))],
                                     axis=1).astype(BF16)
    p["kv_norm_g"] = kv_norm_g[None, :]
    w_ukv4 = w_ukv.reshape(KV_LORA, MLA_HEADS, 2, MLA_NOPE)
    p["w_ukv_r"] = w_ukv4.transpose(0, 2, 1, 3).reshape(KV_LORA, 2 * MLA_HEADS * MLA_NOPE).astype(BF16)
    p["w_uk_t"] = w_ukv4[:, :, 0, :].transpose(1, 2, 0).astype(BF16)
    p["w_uv"] = w_ukv4[:, :, 1, :].transpose(1, 0, 2).astype(BF16)
    wr = jnp.concatenate([w_route_group, w_route_expert], axis=-1)
    wr = jnp.pad(wr, ((0, 0), (0, 0), (0, ROUTE_W - wr.shape[-1])))
    wr_hi = wr.astype(BF16)
    wr_lo = (wr - wr_hi.astype(F32)).astype(BF16)
    br = jnp.concatenate([b_route_group, b_route_expert], axis=-1)
    br = jnp.pad(br, ((0, 0), (0, ROUTE_W - br.shape[-1])))
    p["wr_hi"] = [wr_hi[l] for l in range(DEPTH)]
    p["wr_lo"] = [wr_lo[l] for l in range(DEPTH)]
    p["br"] = [br[l][None, :] for l in range(DEPTH)]
    p["w_gu"] = w_expert_gate_up.astype(BF16).reshape(DEPTH * N_EXPERTS, D_MODEL, 2 * EXPERT_FF)
    p["w_dn"] = w_expert_down.astype(BF16).reshape(DEPTH * N_EXPERTS, EXPERT_FF, D_MODEL)
    return p


def _trunk(x3, mod, pos0, ret_s0, past_latent, past_k_rope, ret_chunk, ret_block, ln_g, ln_b, p):
    b, l, _ = x3.shape
    tl = _Tiles(b, l)
    x = x3.reshape(tl.t, D_MODEL)
    new_ret = []
    latent_new = k_rope_new = krp = None
    kc = vv = None
    for layer in range(DEPTH):
        sh1, sc1, g1, sh2, sc2, g2 = [tl.mod(m) for m in jnp.split(mod[layer], 6, axis=-1)]
        lng = [ln_g[layer, s][None, :] for s in range(2)]
        lnb = [ln_b[layer, s][None, :] for s in range(2)]
        if layer < N_A:
            cos, sin = [jnp.asarray(tab, F32) for tab in _rope_tables(pos0, l, RET_DK // 2)]
            q, k, v, sg = _ret_in(tl, x, sc1, sh1, p["w_ret_in"][layer], tl.table(cos), tl.table(sin))
            a, s_new = _retention(b, l, q, k, v, sg, ret_s0[layer], ret_chunk, ret_block)
            a = a.reshape(tl.t, RET_HEADS * RET_DV)
            new_ret.append(s_new)
            w_out = p["w_ret_out"][layer]
            name = "ret_out_ln_route"
        else:
            j = layer - N_A
            cos, sin = _mla_rope_tables(pos0, l)
            cos, sin = tl.table(cos), tl.table(sin)
            if layer == N_A:
                latent_new, k_rope_new, krp = _kv_latent(tl, x, p["w_dkv_ext"], p["kv_norm_g"], cos, sin)
                if past_latent is None:
                    kc, vv = _kv_expand(latent_new, krp, p["w_ukv_r"], TOKEN_TILE)
            qscale = float((MLA_NOPE + MLA_ROPE) ** -0.5 * math.log2(math.e))
            qc = _q_proj(tl, x, sc1, sh1, p["w_dq"][j], p["q_norm_g"][j], p["w_uq_ext"][j], cos, sin, qscale)
            if past_latent is None:
                a = _attention(qc, kc, vv, b, l, l, ATTN_Q_TILE, ATTN_TILE, pos0, l, 2)
            else:
                a = _decode_attention(qc, latent_new, krp, past_latent, past_k_rope, p["w_uk_t"], p["w_uv"],
                                      b, l, pos0)
            a = a.reshape(tl.t, MLA_HEADS * MLA_V)
            w_out = p["w_mla_out"][j]
            name = "mla_out_ln_route"
        x1, gt, gi = _mix_out(tl, a, w_out, x, g1, lng[0], lnb[0], sc2, sh2,
                              p["wr_hi"][layer], p["wr_lo"][layer], p["br"][layer], name)
        x = _moe_ffn(tl, x1, sc2, sh2, gt, gi.reshape(tl.t), g2, lng[1], lnb[1],
                     p["w_gu"], p["w_dn"], layer)
    return (x.reshape(b, l, D_MODEL), jnp.stack(new_ret), latent_new.reshape(b, l, KV_LORA),
            k_rope_new.reshape(b, l, MLA_ROPE))


def kernel(x_prompt, x_sample, state_retention, cache_kv_latent, cache_k_rope, c_prompt, c_sample,
           w_ada, b_ada, ln_g, ln_b, w_ret_in, w_ret_out, w_dq, q_norm_g, w_uq, w_mla_out,
           w_dkv, kv_norm_g, w_ukv, w_route_group, b_route_group, w_route_expert, b_route_expert,
           w_expert_gate_up, w_expert_down):
    bp, lp, _ = x_prompt.shape
    bs, ls, _ = x_sample.shape
    p = _prep_weights(w_ret_in, w_ret_out, w_dq, q_norm_g, w_uq, w_mla_out, w_dkv, kv_norm_g, w_ukv,
                      w_route_group, b_route_group, w_route_expert, b_route_expert,
                      w_expert_gate_up, w_expert_down)
    n_seq = bp + bs
    n_rows = -(-n_seq // 8) * 8
    c_all = jnp.concatenate([c_prompt, c_sample, jnp.zeros((n_rows - n_seq, D_MODEL), F32)], axis=0)
    mod = _ada(c_all, w_ada, b_ada)
    s0_p = jnp.zeros((N_A, bp, RET_HEADS, RET_DK, RET_DV), F32)
    ret_chunk_p = RET_CHUNK if lp % RET_CHUNK == 0 else CHUNK
    y_p, st_p, lat_p, kr_p = _trunk(x_prompt, mod[:, :bp], 0, s0_p, None, None,
                                    ret_chunk_p, max(ret_chunk_p, min(lp, TOKEN_TILE)), ln_g, ln_b, p)
    past = cache_kv_latent.shape[1]
    y_s, st_s, lat_s, kr_s = _trunk(x_sample, mod[:, bp:n_seq], past, state_retention,
                                    cache_kv_latent, cache_k_rope, ls, ls, ln_g, ln_b, p)
    return (y_p, y_s, st_p, st_s, lat_p, kr_p, lat_s, kr_s)
```

```python
import functools
import math

import numpy as np
import jax
import jax.numpy as jnp
from jax import lax
from jax.experimental import pallas as pl
from jax.experimental.pallas import tpu as pltpu

F32 = jnp.float32
BF16 = jnp.bfloat16

D_MODEL = 1024
DEPTH = 2
CHUNK = 64
LOG2_CHUNK = 6
N_A = DEPTH // 2
RET_HEADS = 4
RET_DK = 256
RET_DV = 512
MLA_HEADS = 8
MLA_NOPE = 128
MLA_ROPE = 64
MLA_V = 128
Q_LORA = 384
KV_LORA = 256
ROPE_THETA = 10000.0
N_GROUPS = 4
EXP_PER_GROUP = 4
N_EXPERTS = N_GROUPS * EXP_PER_GROUP
N_PAIRS = EXP_PER_GROUP * (EXP_PER_GROUP - 1) // 2
N_BUCKETS = N_GROUPS * N_PAIRS
PAIR_A = (0, 0, 0, 1, 1, 2)
PAIR_B = (1, 2, 3, 2, 3, 3)
EXPERT_FF = 512
LN_EPS = 1e-5
RMS_EPS = 1e-6
DN_ALPHA = (2 * DEPTH) ** 0.25
NEG_INF = -1e30

LANES = 128
ROUTE_W = LANES
TOKEN_TILE = 512
EXPERT_TILE = 256
RET_CHUNK = 256
ATTN_TILE = 512
HEAD_W = 2 * LANES
VMEM_LIMIT = 56 * 1024 * 1024
ATTN_VMEM_LIMIT = 56 * 1024 * 1024


def _cparams(sem, vmem=VMEM_LIMIT):
    return pltpu.CompilerParams(dimension_semantics=sem, vmem_limit_bytes=vmem)


def _silu(x):
    return x * jax.nn.sigmoid(x)


def _layer_norm(z, g, b):
    mu = jnp.mean(z, axis=-1, keepdims=True)
    zc = z - mu
    var = jnp.mean(zc * zc, axis=-1, keepdims=True)
    return zc * lax.rsqrt(var + LN_EPS) * g + b


def _ada_kernel(c_ref, w_ref, b_ref, o_ref):
    s = _silu(c_ref[...]).astype(BF16)
    o_ref[0] = jnp.dot(s, w_ref[0].astype(BF16), preferred_element_type=F32) + b_ref[0]


def _ada(c_all, w_ada, b_ada):
    r = c_all.shape[0]
    n = w_ada.shape[-1]
    tn = 1536
    return pl.pallas_call(
        _ada_kernel,
        grid=(DEPTH, n // tn),
        in_specs=[pl.BlockSpec((r, D_MODEL), lambda l, j: (0, 0)),
                  pl.BlockSpec((1, D_MODEL, tn), lambda l, j: (l, 0, j)),
                  pl.BlockSpec((1, 1, tn), lambda l, j: (l, 0, j))],
        out_specs=pl.BlockSpec((1, r, tn), lambda l, j: (l, 0, j)),
        out_shape=jax.ShapeDtypeStruct((DEPTH, r, n), F32),
        compiler_params=_cparams(("parallel", "parallel")),
        name="ada_mod",
    )(c_all, w_ada, b_ada.reshape(DEPTH, 1, n))


class _Tiles:
    def __init__(self, b, l):
        self.b, self.l, self.t = b, l, b * l
        self.per_token = (l % TOKEN_TILE) != 0
        self.tm = self.t if self.per_token else TOKEN_TILE
        self.n = self.t // self.tm
        self.tiles_per_seq = 1 if self.per_token else l // self.tm

    def mod(self, m):
        if self.per_token:
            return jnp.repeat(m, self.l, axis=0)[None]
        return m[:, None, :]

    def mod_spec(self, arr):
        tps = self.tiles_per_seq
        return pl.BlockSpec((1,) + arr.shape[1:], lambda i: (i // tps, 0, 0))

    def table(self, tab):
        return jnp.tile(tab, (self.b, 1)) if self.per_token else tab

    def table_spec(self, tab):
        nt = tab.shape[0] // self.tm
        return pl.BlockSpec((self.tm, tab.shape[1]), lambda i: (i % nt, 0))

    def rows(self, w):
        return pl.BlockSpec((self.tm, w), lambda i: (i, 0))


def _const_spec(arr):
    nd = arr.ndim
    return pl.BlockSpec(arr.shape, lambda *_: (0,) * nd)


def _ret_in_kernel(x_ref, sc_ref, sh_ref, w_ref, cos_ref, sin_ref, q_ref, k_ref, v_ref, sg_ref):
    h = (x_ref[...] * (1.0 + sc_ref[0]) + sh_ref[0]).astype(BF16)
    cos = cos_ref[...]
    sin = sin_ref[...]
    qk_w = RET_HEADS * RET_DK
    half = RET_DK // 2

    def rope_store(r, out_ref, scale):
        for hh in range(RET_HEADS):
            a = r[:, hh * RET_DK:hh * RET_DK + half]
            b = r[:, hh * RET_DK + half:(hh + 1) * RET_DK]
            out_ref[:, hh * RET_DK:hh * RET_DK + half] = ((a * cos - b * sin) * scale).astype(BF16)
            out_ref[:, hh * RET_DK + half:(hh + 1) * RET_DK] = ((a * sin + b * cos) * scale).astype(BF16)

    r = jnp.dot(h, w_ref[:, 0:qk_w], preferred_element_type=F32)
    rope_store(r, q_ref, 1.0)
    r = jnp.dot(h, w_ref[:, qk_w:2 * qk_w], preferred_element_type=F32)
    rope_store(r, k_ref, RET_DK ** -0.5)
    for c in range(2):
        lo = 2 * qk_w + c * qk_w
        v_ref[:, c * qk_w:(c + 1) * qk_w] = jnp.dot(
            h, w_ref[:, lo:lo + qk_w], preferred_element_type=F32).astype(BF16)
    for c in range(2):
        lo = 4 * qk_w + c * qk_w
        g = jnp.dot(h, w_ref[:, lo:lo + qk_w], preferred_element_type=F32)
        sg_ref[:, c * qk_w:(c + 1) * qk_w] = _silu(g).astype(BF16)


def _ret_in(tl, x, sc, sh, w_in, cos, sin):
    qk_w, v_w = RET_HEADS * RET_DK, RET_HEADS * RET_DV
    return pl.pallas_call(
        _ret_in_kernel,
        grid=(tl.n,),
        in_specs=[tl.rows(D_MODEL), tl.mod_spec(sc), tl.mod_spec(sh), _const_spec(w_in),
                  tl.table_spec(cos), tl.table_spec(sin)],
        out_specs=[tl.rows(qk_w), tl.rows(qk_w), tl.rows(v_w), tl.rows(v_w)],
        out_shape=[jax.ShapeDtypeStruct((tl.t, qk_w), BF16), jax.ShapeDtypeStruct((tl.t, qk_w), BF16),
                   jax.ShapeDtypeStruct((tl.t, v_w), BF16), jax.ShapeDtypeStruct((tl.t, v_w), BF16)],
        compiler_params=_cparams(("parallel",)),
        name="ret_in_proj",
    )(x, sc, sh, w_in, cos, sin)


def _retention_tables(chunk):
    lg = np.log1p(-np.exp2(-5.0 - np.arange(RET_HEADS, dtype=np.float64)))
    idx = np.arange(chunk, dtype=np.float64)
    diff = idx[:, None] - idx[None, :]
    dmask = np.where(diff >= 0, np.exp(lg[:, None, None] * np.maximum(diff, 0.0)), 0.0)
    xi = np.exp(lg[:, None] * (idx[None, :] + 1.0))[:, :, None]
    zeta = np.exp(lg[:, None] * (chunk - 1.0 - idx[None, :]))[:, :, None]
    cdec = np.exp(lg * chunk)
    return (jnp.asarray(dmask, F32), jnp.asarray(xi, F32), jnp.asarray(zeta, F32),
            [float(np.float32(c)) for c in cdec])


def _retention_kernel(q_ref, k_ref, v_ref, sg_ref, s0_ref, dm_ref, xi_ref, zeta_ref,
                      o_ref, sout_ref, s_sc, *, chunk, n_chunks, cdec):
    j = pl.program_id(1)

    @pl.when(j == 0)
    def _():
        s_sc[...] = s0_ref[0]

    for c in range(n_chunks):
        rows = slice(c * chunk, (c + 1) * chunk)
        for hh in range(RET_HEADS):
            kcols = slice(hh * RET_DK, (hh + 1) * RET_DK)
            vcols = slice(hh * RET_DV, (hh + 1) * RET_DV)
            q = q_ref[0, rows, kcols]
            k = k_ref[0, rows, kcols]
            v = v_ref[0, rows, vcols]
            s_old = s_sc[hh]
            sc = lax.dot_general(q, k, (((1,), (1,)), ((), ())), preferred_element_type=F32) * dm_ref[hh]
            o = jnp.dot(sc.astype(BF16), v, preferred_element_type=F32)
            o = o + jnp.dot(q, s_old.astype(BF16), preferred_element_type=F32) * xi_ref[hh]
            kz_t = (k.astype(F32) * zeta_ref[hh]).T.astype(BF16)
            s_sc[hh] = s_old * cdec[hh] + jnp.dot(kz_t, v, preferred_element_type=F32)
            mu = jnp.mean(o, axis=-1, keepdims=True)
            oc = o - mu
            var = jnp.mean(oc * oc, axis=-1, keepdims=True)
            on = oc * lax.rsqrt(var + LN_EPS)
            o_ref[0, rows, vcols] = (sg_ref[0, rows, vcols].astype(F32) * on).astype(BF16)

    @pl.when(j == pl.num_programs(1) - 1)
    def _():
        sout_ref[0] = s_sc[...]


def _retention(b, l, q, k, v, sg, s0, chunk, block):
    qk_w, v_w = RET_HEADS * RET_DK, RET_HEADS * RET_DV
    dmask, xi, zeta, cdec = _retention_tables(chunk)
    kern = functools.partial(_retention_kernel, chunk=chunk, n_chunks=block // chunk, cdec=cdec)
    seq = lambda w: pl.BlockSpec((1, block, w), lambda bi, j: (bi, j, 0))
    st = pl.BlockSpec((1, RET_HEADS, RET_DK, RET_DV), lambda bi, j: (bi, 0, 0, 0))
    return pl.pallas_call(
        kern,
        grid=(b, l // block),
        in_specs=[seq(qk_w), seq(qk_w), seq(v_w), seq(v_w), st,
                  _const_spec(dmask), _const_spec(xi), _const_spec(zeta)],
        out_specs=[seq(v_w), st],
        out_shape=[jax.ShapeDtypeStruct((b, l, v_w), BF16),
                   jax.ShapeDtypeStruct((b, RET_HEADS, RET_DK, RET_DV), F32)],
        scratch_shapes=[pltpu.VMEM((RET_HEADS, RET_DK, RET_DV), F32)],
        compiler_params=_cparams(("parallel", "arbitrary")),
        name="retention_scan",
    )(q.reshape(b, l, qk_w), k.reshape(b, l, qk_w), v.reshape(b, l, v_w), sg.reshape(b, l, v_w),
      s0, dmask, xi, zeta)


def _route_rows(lt):
    g = [lt[i:i + 1] for i in range(N_GROUPS)]
    m = jnp.maximum(jnp.maximum(g[0], g[1]), jnp.maximum(g[2], g[3]))
    gi = jnp.where(g[0] == m, 0, jnp.where(g[1] == m, 1, jnp.where(g[2] == m, 2, 3))).astype(jnp.int32)
    denom = jnp.exp(g[0] - m) + jnp.exp(g[1] - m) + jnp.exp(g[2] - m) + jnp.exp(g[3] - m)
    p_group = 1.0 / denom
    le = []
    for e in range(EXP_PER_GROUP):
        r = [lt[N_GROUPS + gg * EXP_PER_GROUP + e:N_GROUPS + gg * EXP_PER_GROUP + e + 1]
             for gg in range(N_GROUPS)]
        le.append(jnp.where(gi == 0, r[0], jnp.where(gi == 1, r[1], jnp.where(gi == 2, r[2], r[3]))))
    me = jnp.maximum(jnp.maximum(le[0], le[1]), jnp.maximum(le[2], le[3]))
    ex = [jnp.exp(x - me) for x in le]

    def first_argmax(vals):
        mx = jnp.maximum(jnp.maximum(vals[0], vals[1]), jnp.maximum(vals[2], vals[3]))
        ix = jnp.where(vals[0] == mx, 0, jnp.where(vals[1] == mx, 1, jnp.where(vals[2] == mx, 2, 3)))
        return mx, ix.astype(jnp.int32)

    e1, i1 = first_argmax(ex)
    ex2 = [jnp.where(i1 == e, -1.0, ex[e]) for e in range(EXP_PER_GROUP)]
    e2, i2 = first_argmax(ex2)
    tot = e1 + e2
    w1 = e1 / tot * p_group
    w2 = e2 / tot * p_group
    first_low = i1 < i2
    a = jnp.where(first_low, i1, i2)
    b = jnp.where(first_low, i2, i1)
    pair = jnp.where(a == 0, b - 1, jnp.where(a == 1, b + 1, N_PAIRS - 1))
    bucket = gi * N_PAIRS + pair
    return bucket, jnp.where(first_low, w1, w2), jnp.where(first_low, w2, w1)


def _mix_out_kernel(a_ref, w_ref, x_ref, g1_ref, lng_ref, lnb_ref, sc2_ref, sh2_ref,
                    wrh_ref, wrl_ref, br_ref, x1_ref, gt_ref, gi_ref):
    y = jnp.dot(a_ref[...], w_ref[...], preferred_element_type=F32)
    x1 = _layer_norm(DN_ALPHA * x_ref[...] + g1_ref[0] * y, lng_ref[...], lnb_ref[...])
    x1_ref[...] = x1
    h2 = x1 * (1.0 + sc2_ref[0]) + sh2_ref[0]
    hi = h2.astype(BF16)
    lo = (h2 - hi.astype(F32)).astype(BF16)
    wrh = wrh_ref[...]
    logits = (jnp.dot(hi, wrh, preferred_element_type=F32)
              + (jnp.dot(hi, wrl_ref[...], preferred_element_type=F32)
                 + jnp.dot(lo, wrh, preferred_element_type=F32))) + br_ref[...]
    lt = logits.T
    bucket, gate_a, gate_b = _route_rows(lt)
    gi_ref[0] = bucket
    tm = lt.shape[1]
    row = lax.broadcasted_iota(jnp.int32, (8, tm), 0)
    g8 = jnp.where(row == 0, gate_a, jnp.where(row == 1, gate_b, 0.0))
    gt = jnp.concatenate([g8, jnp.zeros((ROUTE_W - 8, tm), F32)], axis=0)
    gt_ref[...] = gt.T


def _mix_out(tl, a, w, x, g1, lng, lnb, sc2, sh2, wrh, wrl, br, name):
    kd = a.shape[1]
    return pl.pallas_call(
        _mix_out_kernel,
        grid=(tl.n,),
        in_specs=[tl.rows(kd), _const_spec(w), tl.rows(D_MODEL), tl.mod_spec(g1),
                  _const_spec(lng), _const_spec(lnb), tl.mod_spec(sc2), tl.mod_spec(sh2),
                  _const_spec(wrh), _const_spec(wrl), _const_spec(br)],
        out_specs=[tl.rows(D_MODEL), tl.rows(ROUTE_W),
                   pl.BlockSpec((1, 1, tl.tm), lambda i: (i, 0, 0))],
        out_shape=[jax.ShapeDtypeStruct((tl.t, D_MODEL), F32),
                   jax.ShapeDtypeStruct((tl.t, ROUTE_W), F32),
                   jax.ShapeDtypeStruct((tl.n, 1, tl.tm), jnp.int32)],
        compiler_params=_cparams(("parallel",)),
        name=name,
    )(a, w, x, g1, lng, lnb, sc2, sh2, wrh, wrl, br)


TOK_SUB = D_MODEL // LANES


def _moe_sort_kernel(idx_ref, x1_ref, sc2_ref, sh2_ref, gt_ref, tok_ref, gs_ref, buf, gs_sc, sem, gsem, *, tm):
    i = pl.program_id(0)
    n = pl.num_programs(0)
    slot = i % 2

    def wait_slot(sl):
        pltpu.make_async_copy(buf.at[sl], tok_ref.at[pl.ds(0, tm * TOK_SUB)], sem.at[sl]).wait()

    @pl.when(i >= 2)
    def _():
        wait_slot(slot)

    h2 = x1_ref[...] * (1.0 + sc2_ref[0]) + sh2_ref[0]
    for s in range(TOK_SUB):
        buf[slot, pl.ds(s, tm, stride=TOK_SUB), :] = h2[:, s * LANES:(s + 1) * LANES]

    base = i * tm

    def issue(r, carry):
        d = idx_ref[base + r]
        pltpu.make_async_copy(buf.at[slot, pl.ds(pl.multiple_of(r * TOK_SUB, TOK_SUB), TOK_SUB)],
                              tok_ref.at[pl.ds(pl.multiple_of(d * TOK_SUB, TOK_SUB), TOK_SUB)],
                              sem.at[slot]).start()
        gs_sc[pl.ds(d, 1), :] = gt_ref[pl.ds(r, 1), :]
        return carry

    lax.fori_loop(0, tm, issue, 0, unroll=8)

    @pl.when(i == n - 1)
    def _():
        gates_out = pltpu.make_async_copy(gs_sc, gs_ref, gsem)
        gates_out.start()

        @pl.when(n >= 2)
        def _():
            wait_slot(1 - slot)
        wait_slot(slot)
        gates_out.wait()


def _moe_sort(tl, x1, sc2, sh2, gt, dst_idx, n_rows):
    tm = tl.tm
    n_steps = dst_idx.shape[0] // tm
    last = tl.n - 1
    tps = tl.tiles_per_seq
    kern = functools.partial(_moe_sort_kernel, tm=tm)
    rows = lambda w: pl.BlockSpec((tm, w), lambda i, idx: (jnp.minimum(i, last), 0))
    mod = lambda arr: pl.BlockSpec((1,) + arr.shape[1:], lambda i, idx: (jnp.minimum(i, last) // tps, 0, 0))
    return pl.pallas_call(
        kern,
        grid_spec=pltpu.PrefetchScalarGridSpec(
            num_scalar_prefetch=1,
            grid=(n_steps,),
            in_specs=[rows(D_MODEL), mod(sc2), mod(sh2), rows(ROUTE_W)],
            out_specs=[pl.BlockSpec(memory_space=pl.ANY), pl.BlockSpec(memory_space=pl.ANY)],
            scratch_shapes=[pltpu.VMEM((2, tm * TOK_SUB, LANES), F32), pltpu.VMEM((n_rows, ROUTE_W), F32),
                            pltpu.SemaphoreType.DMA((2,)), pltpu.SemaphoreType.DMA(())]),
        out_shape=[jax.ShapeDtypeStruct((n_rows * TOK_SUB, LANES), F32),
                   jax.ShapeDtypeStruct((n_rows, ROUTE_W), F32)],
        compiler_params=_cparams(("arbitrary",)),
        name="moe_sort_rows",
    )(dst_idx, x1, sc2, sh2, gt)


def _moe_kernel(ea_ref, eb_ref, nv_ref, tok_ref, gs_ref, wgu_a_ref, wgu_b_ref, wdn_a_ref, wdn_b_ref, o_ref,
                *, tme):
    i = pl.program_id(0)

    @pl.when(i < nv_ref[0])
    def _():
        hb = jnp.concatenate([tok_ref[pl.ds(s, tme, stride=TOK_SUB), :] for s in range(TOK_SUB)],
                             axis=1).astype(BF16)
        acc = None
        for lane, (wgu_ref, wdn_ref) in enumerate(((wgu_a_ref, wdn_a_ref), (wgu_b_ref, wdn_b_ref))):
            au = jnp.dot(hb, wgu_ref[0], preferred_element_type=F32)
            act = (_silu(au[:, :EXPERT_FF]) * au[:, EXPERT_FF:]).astype(BF16)
            y = jnp.dot(act, wdn_ref[0], preferred_element_type=F32)
            gy = gs_ref[:, lane:lane + 1] * y
            acc = gy if acc is None else acc + gy
        for s in range(TOK_SUB):
            o_ref[pl.ds(s, tme, stride=TOK_SUB), :] = acc[:, s * LANES:(s + 1) * LANES]

    @pl.when(i >= nv_ref[0])
    def _():
        o_ref[...] = jnp.zeros_like(o_ref)


def _moe_experts(tok, gs, tile_ea, tile_eb, n_valid, wgu, wdn, tme):
    n_tiles = gs.shape[0] // tme
    kern = functools.partial(_moe_kernel, tme=tme)
    gu = lambda pick: pl.BlockSpec((1, D_MODEL, 2 * EXPERT_FF), lambda i, ea, eb, nv: (pick(ea, eb)[i], 0, 0))
    dn = lambda pick: pl.BlockSpec((1, EXPERT_FF, D_MODEL), lambda i, ea, eb, nv: (pick(ea, eb)[i], 0, 0))
    first = lambda ea, eb: ea
    second = lambda ea, eb: eb
    return pl.pallas_call(
        kern,
        grid_spec=pltpu.PrefetchScalarGridSpec(
            num_scalar_prefetch=3,
            grid=(n_tiles,),
            in_specs=[pl.BlockSpec((tme * TOK_SUB, LANES), lambda i, ea, eb, nv: (i, 0)),
                      pl.BlockSpec((tme, ROUTE_W), lambda i, ea, eb, nv: (i, 0)),
                      gu(first), gu(second), dn(first), dn(second)],
            out_specs=pl.BlockSpec((tme * TOK_SUB, LANES), lambda i, ea, eb, nv: (i, 0))),
        out_shape=jax.ShapeDtypeStruct(tok.shape, F32),
        compiler_params=_cparams(("arbitrary",)),
        name="moe_experts",
    )(tile_ea, tile_eb, n_valid, tok, gs, wgu, wgu, wdn, wdn)


def _ln_res_kernel(idx_ref, fs_ref, x_ref, g_ref, lng_ref, lnb_ref, o_ref, buf0, buf1, sem, *, tm):
    i = pl.program_id(0)
    n = pl.num_programs(0)
    bufs = (buf0, buf1)

    def row_copy(d, sl, row_off):
        return pltpu.make_async_copy(fs_ref.at[pl.ds(pl.multiple_of(d * TOK_SUB, TOK_SUB), TOK_SUB)],
                                     bufs[sl].at[pl.ds(row_off, TOK_SUB)], sem.at[sl])

    def wait_rows(sl):
        pltpu.make_async_copy(fs_ref.at[pl.ds(0, tm * TOK_SUB)], bufs[sl], sem.at[sl]).wait()

    @pl.when(i == 0)
    def _():
        def body(r, carry):
            row_copy(idx_ref[r], 0, pl.multiple_of(r * TOK_SUB, TOK_SUB)).start()
            return carry

        lax.fori_loop(0, tm, body, 0, unroll=8)

    nxt = jnp.minimum(i + 1, n - 1) * tm

    def step(sl):
        wait_rows(sl)
        f = jnp.concatenate([bufs[sl][pl.ds(s, tm, stride=TOK_SUB), :] for s in range(TOK_SUB)], axis=1)
        o_ref[...] = _layer_norm(DN_ALPHA * x_ref[...] + g_ref[0] * f, lng_ref[...], lnb_ref[...])
        for r in range(tm):
            row_copy(idx_ref[nxt + r], 1 - sl, r * TOK_SUB).start()

    @pl.when(i % 2 == 0)
    def _():
        step(0)

    @pl.when(i % 2 == 1)
    def _():
        step(1)

    @pl.when(i == n - 1)
    def _():
        @pl.when(i % 2 == 0)
        def _():
            wait_rows(1)

        @pl.when(i % 2 == 1)
        def _():
            wait_rows(0)


def _ln_res(tl, x, fs, dest, g2, lng, lnb):
    tm = tl.tm
    tps = tl.tiles_per_seq
    kern = functools.partial(_ln_res_kernel, tm=tm)
    const = lambda arr: pl.BlockSpec(arr.shape, lambda i, idx: (0,) * arr.ndim)
    return pl.pallas_call(
        kern,
        grid_spec=pltpu.PrefetchScalarGridSpec(
            num_scalar_prefetch=1,
            grid=(tl.n,),
            in_specs=[pl.BlockSpec(memory_space=pl.ANY),
                      pl.BlockSpec((tm, D_MODEL), lambda i, idx: (i, 0)),
                      pl.BlockSpec((1,) + g2.shape[1:], lambda i, idx: (i // tps, 0, 0)),
                      const(lng), const(lnb)],
            out_specs=pl.BlockSpec((tm, D_MODEL), lambda i, idx: (i, 0)),
            scratch_shapes=[pltpu.VMEM((tm * TOK_SUB, LANES), F32), pltpu.VMEM((tm * TOK_SUB, LANES), F32),
                            pltpu.SemaphoreType.DMA((2,))]),
        out_shape=jax.ShapeDtypeStruct((tl.t, D_MODEL), F32),
        compiler_params=_cparams(("arbitrary",)),
        name="ffn_residual_ln",
    )(dest, fs, x, g2, lng, lnb)


def _moe_ffn(tl, x1, sc2, sh2, gt, gi, g2, lng, lnb, wgu, wdn, layer):
    t = tl.t
    tme = EXPERT_TILE if t % TOKEN_TILE == 0 else 64
    n_tiles = t // tme + N_BUCKETS
    n_pad = N_BUCKETS * tme
    assert n_pad % tl.tm == 0
    i32 = jnp.int32
    onehot = (gi[:, None] == jnp.arange(N_BUCKETS, dtype=i32)[None, :]).astype(i32)
    counts = jnp.sum(onehot, axis=0)
    padded = ((counts + tme - 1) // tme) * tme
    ends = jnp.cumsum(padded)
    starts = ends - padded
    dest = jnp.sum(onehot * (jnp.cumsum(onehot, axis=0) - onehot + starts[None, :]), axis=1).astype(i32)
    cpad = jnp.cumsum(padded - counts)
    k = jnp.arange(n_pad, dtype=i32)
    seg_base = jnp.concatenate([starts + counts, ends[-1:]])
    seg_first = jnp.concatenate([jnp.zeros((1,), cpad.dtype), cpad])
    seg_hot = ((k[:, None] >= seg_first[None, :])
               & (k[:, None] < jnp.concatenate([cpad, jnp.full((1,), n_pad, cpad.dtype)])[None, :])).astype(i32)
    pad_rows = (k + jnp.sum(seg_hot * (seg_base - seg_first)[None, :], axis=1)).astype(i32)
    tile_start = jnp.arange(n_tiles, dtype=i32) * tme
    tile_hot = ((tile_start[:, None] >= starts[None, :]) & (tile_start[:, None] < ends[None, :])).astype(i32)
    bucket_ids = jnp.arange(N_BUCKETS, dtype=i32)
    first_expert = layer * N_EXPERTS + (bucket_ids // N_PAIRS) * EXP_PER_GROUP
    pair_a = jnp.asarray(PAIR_A * N_GROUPS, i32)
    pair_b = jnp.asarray(PAIR_B * N_GROUPS, i32)
    last_expert = layer * N_EXPERTS + N_EXPERTS - 1
    in_use = jnp.sum(tile_hot, axis=1)
    tile_ea = (jnp.sum(tile_hot * (first_expert + pair_a)[None, :], axis=1) + (1 - in_use) * last_expert).astype(i32)
    tile_eb = (jnp.sum(tile_hot * (first_expert + pair_b)[None, :], axis=1) + (1 - in_use) * last_expert).astype(i32)
    n_valid = (ends[-1] // tme).astype(i32).reshape(1)
    tok, gs = _moe_sort(tl, x1, sc2, sh2, gt, jnp.concatenate([dest, pad_rows]), n_tiles * tme)
    fs = _moe_experts(tok, gs, tile_ea, tile_eb, n_valid, wgu, wdn, tme)
    return _ln_res(tl, x1, fs, dest, g2, lng, lnb)


def _kv_latent_kernel(x_ref, w_ref, g_ref, cos_ref, sin_ref, lat_ref, kr_ref, krp_ref):
    kv = jnp.dot(x_ref[...].astype(BF16), w_ref[...], preferred_element_type=F32)
    c = kv[:, :KV_LORA]
    lat_ref[...] = c * lax.rsqrt(jnp.mean(c * c, axis=-1, keepdims=True) + RMS_EPS) * g_ref[...]
    kr = kv[:, KV_LORA:KV_LORA + LANES] * cos_ref[...] + kv[:, KV_LORA + LANES:] * sin_ref[...]
    krp_ref[...] = kr
    kr_ref[...] = kr[:, :MLA_ROPE]


def _kv_latent(tl, x, w_dkv_ext, kv_g, cos, sin):
    return pl.pallas_call(
        _kv_latent_kernel,
        grid=(tl.n,),
        in_specs=[tl.rows(D_MODEL), _const_spec(w_dkv_ext), _const_spec(kv_g),
                  tl.table_spec(cos), tl.table_spec(sin)],
        out_specs=[tl.rows(KV_LORA), tl.rows(MLA_ROPE), tl.rows(LANES)],
        out_shape=[jax.ShapeDtypeStruct((tl.t, KV_LORA), F32), jax.ShapeDtypeStruct((tl.t, MLA_ROPE), F32),
                   jax.ShapeDtypeStruct((tl.t, LANES), F32)],
        compiler_params=_cparams(("parallel",)),
        name="mla_kv_latent",
    )(x, w_dkv_ext, kv_g, cos, sin)


def _kv_expand_kernel(lat_ref, krp_ref, w_ref, k_ref, v_ref):
    kvx = jnp.dot(lat_ref[...].astype(BF16), w_ref[...], preferred_element_type=F32)
    krb = krp_ref[...].astype(BF16)
    ones_col = jnp.where(lax.broadcasted_iota(jnp.int32, krb.shape, 1) == 0, 1.0, 0.0).astype(BF16)
    v0 = MLA_HEADS * MLA_NOPE
    for hh in range(MLA_HEADS):
        k_ref[:, hh * HEAD_W:hh * HEAD_W + MLA_NOPE] = kvx[:, hh * MLA_NOPE:(hh + 1) * MLA_NOPE].astype(BF16)
        k_ref[:, hh * HEAD_W + MLA_NOPE:(hh + 1) * HEAD_W] = krb
        v_ref[:, hh * HEAD_W:hh * HEAD_W + MLA_V] = kvx[:, v0 + hh * MLA_V:v0 + (hh + 1) * MLA_V].astype(BF16)
        v_ref[:, hh * HEAD_W + MLA_V:(hh + 1) * HEAD_W] = ones_col


def _kv_expand(lat, krp, w_ukv_r, tm):
    t = lat.shape[0]
    rows = lambda w: pl.BlockSpec((tm, w), lambda i: (i, 0))
    return pl.pallas_call(
        _kv_expand_kernel,
        grid=(t // tm,),
        in_specs=[rows(KV_LORA), rows(LANES), _const_spec(w_ukv_r)],
        out_specs=[rows(MLA_HEADS * HEAD_W), rows(MLA_HEADS * HEAD_W)],
        out_shape=[jax.ShapeDtypeStruct((t, MLA_HEADS * HEAD_W), BF16),
                   jax.ShapeDtypeStruct((t, MLA_HEADS * HEAD_W), BF16)],
        compiler_params=_cparams(("parallel",)),
        name="mla_kv_expand",
    )(lat, krp, w_ukv_r)


def _q_proj_kernel(x_ref, sc_ref, sh_ref, wdq_ref, g_ref, wuq_ref, cos_ref, sin_ref, q_ref, *, qscale):
    h = (x_ref[...] * (1.0 + sc_ref[0]) + sh_ref[0]).astype(BF16)
    cq = jnp.dot(h, wdq_ref[...], preferred_element_type=F32)
    qn = (cq * lax.rsqrt(jnp.mean(cq * cq, axis=-1, keepdims=True) + RMS_EPS) * g_ref[...]).astype(BF16)
    cos = cos_ref[...]
    sin = sin_ref[...]
    nw = MLA_HEADS * LANES
    qnope = jnp.dot(qn, wuq_ref[:, :nw], preferred_element_type=F32)
    qpe = jnp.dot(qn, wuq_ref[:, nw:2 * nw], preferred_element_type=F32)
    qpe_sw = jnp.dot(qn, wuq_ref[:, 2 * nw:], preferred_element_type=F32)
    for hh in range(MLA_HEADS):
        cols = slice(hh * LANES, (hh + 1) * LANES)
        q_ref[:, hh * HEAD_W:hh * HEAD_W + LANES] = (qnope[:, cols] * qscale).astype(BF16)
        pe = qpe[:, cols] * cos + qpe_sw[:, cols] * sin
        q_ref[:, hh * HEAD_W + LANES:(hh + 1) * HEAD_W] = (pe * qscale).astype(BF16)


def _q_proj(tl, x, sc, sh, w_dq, q_g, w_uq_ext, cos, sin, qscale):
    kern = functools.partial(_q_proj_kernel, qscale=qscale)
    return pl.pallas_call(
        kern,
        grid=(tl.n,),
        in_specs=[tl.rows(D_MODEL), tl.mod_spec(sc), tl.mod_spec(sh), _const_spec(w_dq), _const_spec(q_g),
                  _const_spec(w_uq_ext), tl.table_spec(cos), tl.table_spec(sin)],
        out_specs=tl.rows(MLA_HEADS * HEAD_W),
        out_shape=jax.ShapeDtypeStruct((tl.t, MLA_HEADS * HEAD_W), BF16),
        compiler_params=_cparams(("parallel",)),
        name="mla_q_proj",
    )(x, sc, sh, w_dq, q_g, w_uq_ext, cos, sin)


def _attn_kernel(q_ref, qn_ref, k_ref, v_ref, o_ref, m_sc, acc_sc, sa_sc, sb_sc, sc_sc, *, t, n_sub, hp):
    i = pl.program_id(2)
    m_sc[...] = jnp.full(m_sc.shape, NEG_INF, F32)
    acc_sc[...] = jnp.zeros(acc_sc.shape, F32)
    rs = t // n_sub

    def scores(j, s_ref, qr=q_ref):
        for h in range(hp):
            hc = slice(h * HEAD_W, (h + 1) * HEAD_W)
            k = k_ref[0, pl.ds(pl.multiple_of(j * t, t), t), hc]
            s_ref[h] = lax.dot_general(qr[0, :, hc], k, (((1,), (1,)), ((), ())), preferred_element_type=F32)

    def fold(j, s_ref, diagonal):
        for h in range(hp):
            v = v_ref[0, pl.ds(pl.multiple_of(j * t, t), t), h * HEAD_W:(h + 1) * HEAD_W]
            for u in range(n_sub):
                rows = slice(u * rs, (u + 1) * rs)
                s = s_ref[h, rows, :]
                if diagonal:
                    qpos = u * rs + lax.broadcasted_iota(jnp.int32, (rs, 1), 0)
                    kpos = lax.broadcasted_iota(jnp.int32, (1, t), 1)
                    s = jnp.where((kpos >> LOG2_CHUNK) <= (qpos >> LOG2_CHUNK), s, NEG_INF)
                m_old = m_sc[h, rows, :]
                m_new = jnp.maximum(m_old, jnp.max(s, axis=1, keepdims=True))
                alpha = jnp.exp2(m_old - m_new)
                p = jnp.exp2(s - jnp.tile(m_new, (1, t // LANES)))
                pv = jnp.dot(p.astype(BF16), v, preferred_element_type=F32)
                acc_sc[h, rows, :] = jnp.tile(alpha, (1, HEAD_W // LANES)) * acc_sc[h, rows, :] + pv
                m_sc[h, rows, :] = m_new

    def next_tile_scores():
        scores(0, sc_sc, qn_ref)

    @pl.when(i == 0)
    def _():
        scores(0, sc_sc)
        fold(0, sc_sc, True)
        next_tile_scores()

    @pl.when(i == 1)
    def _():
        scores(1, sb_sc)
        fold(0, sc_sc, False)
        next_tile_scores()
        fold(1, sb_sc, True)

    @pl.when(i >= 2)
    def _():
        scores(1, sb_sc)
        fold(0, sc_sc, False)
        scores(2, sa_sc)
        fold(1, sb_sc, False)

        def pair_step(jj, carry):
            j = 2 * jj
            scores(j + 1, sb_sc)
            fold(j, sa_sc, False)
            scores(j + 2, sa_sc)
            fold(j + 1, sb_sc, False)
            return carry

        lax.fori_loop(1, i // 2, pair_step, 0)

        @pl.when(i % 2 == 0)
        def _():
            next_tile_scores()
            fold(i, sa_sc, True)

        @pl.when(i % 2 == 1)
        def _():
            scores(i, sb_sc)
            fold(i - 1, sa_sc, False)
            next_tile_scores()
            fold(i, sb_sc, True)

    for h in range(hp):
        acc = acc_sc[h]
        o_ref[0, :, h * MLA_V:(h + 1) * MLA_V] = (acc[:, :MLA_V] / acc[:, MLA_V:MLA_V + 1]).astype(BF16)


def _attention(qc, kc, vv, b, l, t, hp):
    assert l % t == 0 and t % CHUNK == 0
    n_sub = 2 if t % (2 * LANES) == 0 else 1
    nq = l // t
    kern = functools.partial(_attn_kernel, t=t, n_sub=n_sub, hp=hp)
    scores_buf = pltpu.VMEM((hp, t, t), F32)
    return pl.pallas_call(
        kern,
        grid=(b, MLA_HEADS // hp, nq),
        in_specs=[pl.BlockSpec((1, t, hp * HEAD_W), lambda bi, h, i: (bi, i, h)),
                  pl.BlockSpec((1, t, hp * HEAD_W), lambda bi, h, i: (bi, jnp.minimum(i + 1, nq - 1), h)),
                  pl.BlockSpec((1, l, hp * HEAD_W), lambda bi, h, i: (bi, 0, h)),
                  pl.BlockSpec((1, l, hp * HEAD_W), lambda bi, h, i: (bi, 0, h))],
        out_specs=pl.BlockSpec((1, t, hp * MLA_V), lambda bi, h, i: (bi, i, h)),
        out_shape=jax.ShapeDtypeStruct((b, l, MLA_HEADS * MLA_V), BF16),
        scratch_shapes=[pltpu.VMEM((hp, t, LANES), F32), pltpu.VMEM((hp, t, HEAD_W), F32),
                        scores_buf, scores_buf, scores_buf],
        compiler_params=_cparams(("parallel", "parallel", "arbitrary"), ATTN_VMEM_LIMIT),
        name="mla_attention",
    )(qc.reshape(b, l, MLA_HEADS * HEAD_W), qc.reshape(b, l, MLA_HEADS * HEAD_W),
      kc.reshape(b, l, MLA_HEADS * HEAD_W), vv.reshape(b, l, MLA_HEADS * HEAD_W))


def _decode_attn_kernel(q_ref, lnew_ref, knew_ref, lpast_ref, kpast_ref, wukt_ref, wuv_ref, o_ref, qa_sc,
                        *, l, pos0, past):
    nope_w = KV_LORA
    for h in range(MLA_HEADS):
        rows = slice(h * l, (h + 1) * l)
        qn = q_ref[0, :, h * HEAD_W:h * HEAD_W + MLA_NOPE]
        qa_sc[rows, :nope_w] = jnp.dot(qn, wukt_ref[h], preferred_element_type=F32).astype(BF16)
        qa_sc[rows, nope_w:] = q_ref[0, :, h * HEAD_W + MLA_NOPE:(h + 1) * HEAD_W]
    q_lat = qa_sc[:, :nope_w]
    q_pe = qa_sc[:, nope_w:nope_w + MLA_ROPE]
    nt = (((1,), (1,)), ((), ()))
    lp = lpast_ref[0].astype(BF16)
    kp = kpast_ref[0].astype(BF16)
    ln = lnew_ref[0].astype(BF16)
    kn = knew_ref[0][:, :MLA_ROPE].astype(BF16)
    s_past = (lax.dot_general(q_lat, lp, nt, preferred_element_type=F32)
              + lax.dot_general(q_pe, kp, nt, preferred_element_type=F32))
    s_new = (lax.dot_general(q_lat, ln, nt, preferred_element_type=F32)
             + lax.dot_general(q_pe, kn, nt, preferred_element_type=F32))
    n_rows = MLA_HEADS * l
    qpos = pos0 + lax.rem(lax.broadcasted_iota(jnp.int32, (n_rows, 1), 0), l)
    kpos_new = pos0 + lax.broadcasted_iota(jnp.int32, (1, l), 1)
    s_new = jnp.where((kpos_new >> LOG2_CHUNK) <= (qpos >> LOG2_CHUNK), s_new, NEG_INF)
    kpos_past = lax.broadcasted_iota(jnp.int32, (1, past), 1)
    s_past = jnp.where((kpos_past >> LOG2_CHUNK) <= (qpos >> LOG2_CHUNK), s_past, NEG_INF)
    m = jnp.maximum(jnp.max(s_past, axis=1, keepdims=True), jnp.max(s_new, axis=1, keepdims=True))
    p_past = jnp.exp2(s_past - m)
    p_new = jnp.exp2(s_new - m)
    denom = jnp.sum(p_past, axis=1, keepdims=True) + jnp.sum(p_new, axis=1, keepdims=True)
    ctx = (jnp.dot(p_past.astype(BF16), lp, preferred_element_type=F32)
           + jnp.dot(p_new.astype(BF16), ln, preferred_element_type=F32)) / denom
    ctx = ctx.astype(BF16)
    for h in range(MLA_HEADS):
        o_ref[0, :, h * MLA_V:(h + 1) * MLA_V] = jnp.dot(
            ctx[h * l:(h + 1) * l], wuv_ref[h], preferred_element_type=F32).astype(BF16)


def _decode_attention(qc, latent_new, krp_new, past_latent, past_k_rope, w_uk_t, w_uv, b, l, pos0):
    past = past_latent.shape[1]
    kern = functools.partial(_decode_attn_kernel, l=l, pos0=pos0, past=past)
    per_b = lambda shape: pl.BlockSpec((1,) + shape, lambda bi: (bi, 0, 0))
    return pl.pallas_call(
        kern,
        grid=(b,),
        in_specs=[per_b((l, MLA_HEADS * HEAD_W)), per_b((l, KV_LORA)), per_b((l, LANES)),
                  per_b((past, KV_LORA)), per_b((past, MLA_ROPE)), _const_spec(w_uk_t), _const_spec(w_uv)],
        out_specs=per_b((l, MLA_HEADS * MLA_V)),
        out_shape=jax.ShapeDtypeStruct((b, l, MLA_HEADS * MLA_V), BF16),
        scratch_shapes=[pltpu.VMEM((MLA_HEADS * l, KV_LORA + LANES), BF16)],
        compiler_params=_cparams(("parallel",)),
        name="mla_decode_attention",
    )(qc.reshape(b, l, MLA_HEADS * HEAD_W), latent_new.reshape(b, l, KV_LORA), krp_new.reshape(b, l, LANES),
      past_latent, past_k_rope, w_uk_t, w_uv)


def _rope_tables(pos0, length, half):
    inv = ROPE_THETA ** (-np.arange(half, dtype=np.float64) / half)
    ang = (pos0 + np.arange(length, dtype=np.float64))[:, None] * inv[None, :]
    return np.cos(ang), np.sin(ang)


def _mla_rope_tables(pos0, length):
    cos, sin = _rope_tables(pos0, length, MLA_ROPE // 2)
    z = np.zeros((length, LANES - MLA_ROPE))
    return (jnp.asarray(np.concatenate([cos, cos, z], axis=1), F32),
            jnp.asarray(np.concatenate([-sin, sin, z], axis=1), F32))


def _swap_halves(w):
    half = w.shape[-1] // 2
    return jnp.concatenate([w[..., half:], w[..., :half]], axis=-1)


def _pad_lanes(w):
    return jnp.pad(w, [(0, 0)] * (w.ndim - 1) + [(0, LANES - w.shape[-1])])


def _prep_weights(w_ret_in, w_ret_out, w_dq, q_norm_g, w_uq, w_mla_out, w_dkv, kv_norm_g, w_ukv,
                  w_route_group, b_route_group, w_route_expert, b_route_expert,
                  w_expert_gate_up, w_expert_down):
    p = {}
    p["w_ret_in"] = [w_ret_in[i].astype(BF16) for i in range(N_A)]
    p["w_ret_out"] = [w_ret_out[i].astype(BF16) for i in range(N_A)]
    p["w_dq"] = [w_dq[j].astype(BF16) for j in range(DEPTH - N_A)]
    p["q_norm_g"] = [q_norm_g[j][None, :] for j in range(DEPTH - N_A)]
    w_uq_ext = []
    for j in range(DEPTH - N_A):
        wq = w_uq[j].reshape(Q_LORA, MLA_HEADS, MLA_NOPE + MLA_ROPE)
        nope = wq[:, :, :MLA_NOPE].reshape(Q_LORA, MLA_HEADS * LANES)
        pe = wq[:, :, MLA_NOPE:]
        w_uq_ext.append(jnp.concatenate(
            [nope, _pad_lanes(pe).reshape(Q_LORA, MLA_HEADS * LANES),
             _pad_lanes(_swap_halves(pe)).reshape(Q_LORA, MLA_HEADS * LANES)], axis=1).astype(BF16))
    p["w_uq_ext"] = w_uq_ext
    p["w_mla_out"] = [w_mla_out[j].astype(BF16) for j in range(DEPTH - N_A)]
    kr = w_dkv[:, KV_LORA:]
    p["w_dkv_ext"] = jnp.concatenate([w_dkv[:, :KV_LORA], _pad_lanes(kr), _pad_lanes(_swap_halves(kr))],
                                     axis=1).astype(BF16)
    p["kv_norm_g"] = kv_norm_g[None, :]
    w_ukv4 = w_ukv.reshape(KV_LORA, MLA_HEADS, 2, MLA_NOPE)
    p["w_ukv_r"] = w_ukv4.transpose(0, 2, 1, 3).reshape(KV_LORA, 2 * MLA_HEADS * MLA_NOPE).astype(BF16)
    p["w_uk_t"] = w_ukv4[:, :, 0, :].transpose(1, 2, 0).astype(BF16)
    p["w_uv"] = w_ukv4[:, :, 1, :].transpose(1, 0, 2).astype(BF16)
    wr = jnp.concatenate([w_route_group, w_route_expert], axis=-1)
    wr = jnp.pad(wr, ((0, 0), (0, 0), (0, ROUTE_W - wr.shape[-1])))
    wr_hi = wr.astype(BF16)
    wr_lo = (wr - wr_hi.astype(F32)).astype(BF16)
    br = jnp.concatenate([b_route_group, b_route_expert], axis=-1)
    br = jnp.pad(br, ((0, 0), (0, ROUTE_W - br.shape[-1])))
    p["wr_hi"] = [wr_hi[l] for l in range(DEPTH)]
    p["wr_lo"] = [wr_lo[l] for l in range(DEPTH)]
    p["br"] = [br[l][None, :] for l in range(DEPTH)]
    p["w_gu"] = w_expert_gate_up.astype(BF16).reshape(DEPTH * N_EXPERTS, D_MODEL, 2 * EXPERT_FF)
    p["w_dn"] = w_expert_down.astype(BF16).reshape(DEPTH * N_EXPERTS, EXPERT_FF, D_MODEL)
    return p


def _trunk(x3, mod, pos0, ret_s0, past_latent, past_k_rope, ret_chunk, ret_block, ln_g, ln_b, p):
    b, l, _ = x3.shape
    tl = _Tiles(b, l)
    x = x3.reshape(tl.t, D_MODEL)
    new_ret = []
    latent_new = k_rope_new = krp = None
    kc = vv = None
    for layer in range(DEPTH):
        sh1, sc1, g1, sh2, sc2, g2 = [tl.mod(m) for m in jnp.split(mod[layer], 6, axis=-1)]
        lng = [ln_g[layer, s][None, :] for s in range(2)]
        lnb = [ln_b[layer, s][None, :] for s in range(2)]
        if layer < N_A:
            cos, sin = [jnp.asarray(tab, F32) for tab in _rope_tables(pos0, l, RET_DK // 2)]
            q, k, v, sg = _ret_in(tl, x, sc1, sh1, p["w_ret_in"][layer], tl.table(cos), tl.table(sin))
            a, s_new = _retention(b, l, q, k, v, sg, ret_s0[layer], ret_chunk, ret_block)
            a = a.reshape(tl.t, RET_HEADS * RET_DV)
            new_ret.append(s_new)
            w_out = p["w_ret_out"][layer]
            name = "ret_out_ln_route"
        else:
            j = layer - N_A
            cos, sin = _mla_rope_tables(pos0, l)
            cos, sin = tl.table(cos), tl.table(sin)
            if layer == N_A:
                latent_new, k_rope_new, krp = _kv_latent(tl, x, p["w_dkv_ext"], p["kv_norm_g"], cos, sin)
                if past_latent is None:
                    kc, vv = _kv_expand(latent_new, krp, p["w_ukv_r"], TOKEN_TILE)
            qscale = float((MLA_NOPE + MLA_ROPE) ** -0.5 * math.log2(math.e))
            qc = _q_proj(tl, x, sc1, sh1, p["w_dq"][j], p["q_norm_g"][j], p["w_uq_ext"][j], cos, sin, qscale)
            if past_latent is None:
                assert pos0 == 0
                a = _attention(qc, kc, vv, b, l, ATTN_TILE, 2)
            else:
                a = _decode_attention(qc, latent_new, krp, past_latent, past_k_rope, p["w_uk_t"], p["w_uv"],
                                      b, l, pos0)
            a = a.reshape(tl.t, MLA_HEADS * MLA_V)
            w_out = p["w_mla_out"][j]
            name = "mla_out_ln_route"
        x1, gt, gi = _mix_out(tl, a, w_out, x, g1, lng[0], lnb[0], sc2, sh2,
                              p["wr_hi"][layer], p["wr_lo"][layer], p["br"][layer], name)
        x = _moe_ffn(tl, x1, sc2, sh2, gt, gi.reshape(tl.t), g2, lng[1], lnb[1],
                     p["w_gu"], p["w_dn"], layer)
    return (x.reshape(b, l, D_MODEL), jnp.stack(new_ret), latent_new.reshape(b, l, KV_LORA),
            k_rope_new.reshape(b, l, MLA_ROPE))


def kernel(x_prompt, x_sample, state_retention, cache_kv_latent, cache_k_rope, c_prompt, c_sample,
           w_ada, b_ada, ln_g, ln_b, w_ret_in, w_ret_out, w_dq, q_norm_g, w_uq, w_mla_out,
           w_dkv, kv_norm_g, w_ukv, w_route_group, b_route_group, w_route_expert, b_route_expert,
           w_expert_gate_up, w_expert_down):
    bp, lp, _ = x_prompt.shape
    bs, ls, _ = x_sample.shape
    p = _prep_weights(w_ret_in, w_ret_out, w_dq, q_norm_g, w_uq, w_mla_out, w_dkv, kv_norm_g, w_ukv,
                      w_route_group, b_route_group, w_route_expert, b_route_expert,
                      w_expert_gate_up, w_expert_down)
    n_seq = bp + bs
    n_rows = -(-n_seq // 8) * 8
    c_all = jnp.concatenate([c_prompt, c_sample, jnp.zeros((n_rows - n_seq, D_MODEL), F32)], axis=0)
    mod = _ada(c_all, w_ada, b_ada)
    s0_p = jnp.zeros((N_A, bp, RET_HEADS, RET_DK, RET_DV), F32)
    ret_chunk_p = RET_CHUNK if lp % RET_CHUNK == 0 else CHUNK
    y_p, st_p, lat_p, kr_p = _trunk(x_prompt, mod[:, :bp], 0, s0_p, None, None,
                                    ret_chunk_p, max(ret_chunk_p, min(lp, TOKEN_TILE)), ln_g, ln_b, p)
    past = cache_kv_latent.shape[1]
    y_s, st_s, lat_s, kr_s = _trunk(x_sample, mod[:, bp:n_seq], past, state_retention,
                                    cache_kv_latent, cache_k_rope, ls, ls, ln_g, ln_b, p)
    return (y_p, y_s, st_p, st_s, lat_p, kr_p, lat_s, kr_s)
```

```python
import functools
import math

import numpy as np
import jax
import jax.numpy as jnp
from jax import lax
from jax.experimental import pallas as pl
from jax.experimental.pallas import tpu as pltpu

F32 = jnp.float32
BF16 = jnp.bfloat16

D_MODEL = 1024
DEPTH = 2
CHUNK = 64
LOG2_CHUNK = 6
N_A = DEPTH // 2
RET_HEADS = 4
RET_DK = 256
RET_DV = 512
MLA_HEADS = 8
MLA_NOPE = 128
MLA_ROPE = 64
MLA_V = 128
Q_LORA = 384
KV_LORA = 256
ROPE_THETA = 10000.0
N_GROUPS = 4
EXP_PER_GROUP = 4
N_EXPERTS = N_GROUPS * EXP_PER_GROUP
N_PAIRS = EXP_PER_GROUP * (EXP_PER_GROUP - 1) // 2
N_BUCKETS = N_GROUPS * N_PAIRS
PAIR_A = (0, 0, 0, 1, 1, 2)
PAIR_B = (1, 2, 3, 2, 3, 3)
EXPERT_FF = 512
LN_EPS = 1e-5
RMS_EPS = 1e-6
DN_ALPHA = (2 * DEPTH) ** 0.25
NEG_INF = -1e30

LANES = 128
ROUTE_W = LANES
TOKEN_TILE = 512
EXPERT_TILE = 256
RET_CHUNK = 256
ATTN_TILE = 512
HEAD_W = 2 * LANES
VMEM_LIMIT = 56 * 1024 * 1024
ATTN_VMEM_LIMIT = 56 * 1024 * 1024


def _cparams(sem, vmem=VMEM_LIMIT):
    return pltpu.CompilerParams(dimension_semantics=sem, vmem_limit_bytes=vmem)


def _silu(x):
    return x * jax.nn.sigmoid(x)


def _layer_norm(z, g, b):
    mu = jnp.mean(z, axis=-1, keepdims=True)
    zc = z - mu
    var = jnp.mean(zc * zc, axis=-1, keepdims=True)
    return zc * lax.rsqrt(var + LN_EPS) * g + b


def _ada_kernel(c_ref, w_ref, b_ref, o_ref):
    s = _silu(c_ref[...]).astype(BF16)
    o_ref[0] = jnp.dot(s, w_ref[0].astype(BF16), preferred_element_type=F32) + b_ref[0]


def _ada(c_all, w_ada, b_ada):
    r = c_all.shape[0]
    n = w_ada.shape[-1]
    tn = 1536
    return pl.pallas_call(
        _ada_kernel,
        grid=(DEPTH, n // tn),
        in_specs=[pl.BlockSpec((r, D_MODEL), lambda l, j: (0, 0)),
                  pl.BlockSpec((1, D_MODEL, tn), lambda l, j: (l, 0, j)),
                  pl.BlockSpec((1, 1, tn), lambda l, j: (l, 0, j))],
        out_specs=pl.BlockSpec((1, r, tn), lambda l, j: (l, 0, j)),
        out_shape=jax.ShapeDtypeStruct((DEPTH, r, n), F32),
        compiler_params=_cparams(("parallel", "parallel")),
        name="ada_mod",
    )(c_all, w_ada, b_ada.reshape(DEPTH, 1, n))


class _Tiles:
    def __init__(self, b, l):
        self.b, self.l, self.t = b, l, b * l
        self.per_token = (l % TOKEN_TILE) != 0
        self.tm = self.t if self.per_token else TOKEN_TILE
        self.n = self.t // self.tm
        self.tiles_per_seq = 1 if self.per_token else l // self.tm

    def mod(self, m):
        if self.per_token:
            return jnp.repeat(m, self.l, axis=0)[None]
        return m[:, None, :]

    def mod_spec(self, arr):
        tps = self.tiles_per_seq
        return pl.BlockSpec((1,) + arr.shape[1:], lambda i: (i // tps, 0, 0))

    def table(self, tab):
        return jnp.tile(tab, (self.b, 1)) if self.per_token else tab

    def table_spec(self, tab):
        nt = tab.shape[0] // self.tm
        return pl.BlockSpec((self.tm, tab.shape[1]), lambda i: (i % nt, 0))

    def rows(self, w):
        return pl.BlockSpec((self.tm, w), lambda i: (i, 0))


def _const_spec(arr):
    nd = arr.ndim
    return pl.BlockSpec(arr.shape, lambda *_: (0,) * nd)


def _ret_in_kernel(x_ref, sc_ref, sh_ref, w_ref, cos_ref, sin_ref, q_ref, k_ref, v_ref, sg_ref):
    h = (x_ref[...] * (1.0 + sc_ref[0]) + sh_ref[0]).astype(BF16)
    cos = cos_ref[...]
    sin = sin_ref[...]
    qk_w = RET_HEADS * RET_DK
    half = RET_DK // 2

    def rope_store(r, out_ref, scale):
        for hh in range(RET_HEADS):
            a = r[:, hh * RET_DK:hh * RET_DK + half]
            b = r[:, hh * RET_DK + half:(hh + 1) * RET_DK]
            out_ref[:, hh * RET_DK:hh * RET_DK + half] = ((a * cos - b * sin) * scale).astype(BF16)
            out_ref[:, hh * RET_DK + half:(hh + 1) * RET_DK] = ((a * sin + b * cos) * scale).astype(BF16)

    r = jnp.dot(h, w_ref[:, 0:qk_w], preferred_element_type=F32)
    rope_store(r, q_ref, 1.0)
    r = jnp.dot(h, w_ref[:, qk_w:2 * qk_w], preferred_element_type=F32)
    rope_store(r, k_ref, RET_DK ** -0.5)
    for c in range(2):
        lo = 2 * qk_w + c * qk_w
        v_ref[:, c * qk_w:(c + 1) * qk_w] = jnp.dot(
            h, w_ref[:, lo:lo + qk_w], preferred_element_type=F32).astype(BF16)
    for c in range(2):
        lo = 4 * qk_w + c * qk_w
        g = jnp.dot(h, w_ref[:, lo:lo + qk_w], preferred_element_type=F32)
        sg_ref[:, c * qk_w:(c + 1) * qk_w] = _silu(g).astype(BF16)


def _ret_in(tl, x, sc, sh, w_in, cos, sin):
    qk_w, v_w = RET_HEADS * RET_DK, RET_HEADS * RET_DV
    return pl.pallas_call(
        _ret_in_kernel,
        grid=(tl.n,),
        in_specs=[tl.rows(D_MODEL), tl.mod_spec(sc), tl.mod_spec(sh), _const_spec(w_in),
                  tl.table_spec(cos), tl.table_spec(sin)],
        out_specs=[tl.rows(qk_w), tl.rows(qk_w), tl.rows(v_w), tl.rows(v_w)],
        out_shape=[jax.ShapeDtypeStruct((tl.t, qk_w), BF16), jax.ShapeDtypeStruct((tl.t, qk_w), BF16),
                   jax.ShapeDtypeStruct((tl.t, v_w), BF16), jax.ShapeDtypeStruct((tl.t, v_w), BF16)],
        compiler_params=_cparams(("parallel",)),
        name="ret_in_proj",
    )(x, sc, sh, w_in, cos, sin)


def _retention_tables(chunk):
    lg = np.log1p(-np.exp2(-5.0 - np.arange(RET_HEADS, dtype=np.float64)))
    idx = np.arange(chunk, dtype=np.float64)
    diff = idx[:, None] - idx[None, :]
    dmask = np.where(diff >= 0, np.exp(lg[:, None, None] * np.maximum(diff, 0.0)), 0.0)
    xi = np.exp(lg[:, None] * (idx[None, :] + 1.0))[:, :, None]
    zeta = np.exp(lg[:, None] * (chunk - 1.0 - idx[None, :]))[:, :, None]
    cdec = np.exp(lg * chunk)
    return (jnp.asarray(dmask, F32), jnp.asarray(xi, F32), jnp.asarray(zeta, F32),
            [float(np.float32(c)) for c in cdec])


def _retention_kernel(q_ref, k_ref, v_ref, sg_ref, s0_ref, dm_ref, xi_ref, zeta_ref,
                      o_ref, sout_ref, s_sc, *, chunk, n_chunks, cdec):
    j = pl.program_id(1)

    @pl.when(j == 0)
    def _():
        s_sc[...] = s0_ref[0]

    for c in range(n_chunks):
        rows = slice(c * chunk, (c + 1) * chunk)
        for hh in range(RET_HEADS):
            kcols = slice(hh * RET_DK, (hh + 1) * RET_DK)
            vcols = slice(hh * RET_DV, (hh + 1) * RET_DV)
            q = q_ref[0, rows, kcols]
            k = k_ref[0, rows, kcols]
            v = v_ref[0, rows, vcols]
            s_old = s_sc[hh]
            sc = lax.dot_general(q, k, (((1,), (1,)), ((), ())), preferred_element_type=F32) * dm_ref[hh]
            o = jnp.dot(sc.astype(BF16), v, preferred_element_type=F32)
            o = o + jnp.dot(q, s_old.astype(BF16), preferred_element_type=F32) * xi_ref[hh]
            kz_t = (k.astype(F32) * zeta_ref[hh]).T.astype(BF16)
            s_sc[hh] = s_old * cdec[hh] + jnp.dot(kz_t, v, preferred_element_type=F32)
            mu = jnp.mean(o, axis=-1, keepdims=True)
            oc = o - mu
            var = jnp.mean(oc * oc, axis=-1, keepdims=True)
            on = oc * lax.rsqrt(var + LN_EPS)
            o_ref[0, rows, vcols] = (sg_ref[0, rows, vcols].astype(F32) * on).astype(BF16)

    @pl.when(j == pl.num_programs(1) - 1)
    def _():
        sout_ref[0] = s_sc[...]


def _retention(b, l, q, k, v, sg, s0, chunk, block):
    qk_w, v_w = RET_HEADS * RET_DK, RET_HEADS * RET_DV
    dmask, xi, zeta, cdec = _retention_tables(chunk)
    kern = functools.partial(_retention_kernel, chunk=chunk, n_chunks=block // chunk, cdec=cdec)
    seq = lambda w: pl.BlockSpec((1, block, w), lambda bi, j: (bi, j, 0))
    st = pl.BlockSpec((1, RET_HEADS, RET_DK, RET_DV), lambda bi, j: (bi, 0, 0, 0))
    return pl.pallas_call(
        kern,
        grid=(b, l // block),
        in_specs=[seq(qk_w), seq(qk_w), seq(v_w), seq(v_w), st,
                  _const_spec(dmask), _const_spec(xi), _const_spec(zeta)],
        out_specs=[seq(v_w), st],
        out_shape=[jax.ShapeDtypeStruct((b, l, v_w), BF16),
                   jax.ShapeDtypeStruct((b, RET_HEADS, RET_DK, RET_DV), F32)],
        scratch_shapes=[pltpu.VMEM((RET_HEADS, RET_DK, RET_DV), F32)],
        compiler_params=_cparams(("parallel", "arbitrary")),
        name="retention_scan",
    )(q.reshape(b, l, qk_w), k.reshape(b, l, qk_w), v.reshape(b, l, v_w), sg.reshape(b, l, v_w),
      s0, dmask, xi, zeta)


def _route_rows(lt):
    g = [lt[i:i + 1] for i in range(N_GROUPS)]
    m = jnp.maximum(jnp.maximum(g[0], g[1]), jnp.maximum(g[2], g[3]))
    gi = jnp.where(g[0] == m, 0, jnp.where(g[1] == m, 1, jnp.where(g[2] == m, 2, 3))).astype(jnp.int32)
    denom = jnp.exp(g[0] - m) + jnp.exp(g[1] - m) + jnp.exp(g[2] - m) + jnp.exp(g[3] - m)
    p_group = 1.0 / denom
    le = []
    for e in range(EXP_PER_GROUP):
        r = [lt[N_GROUPS + gg * EXP_PER_GROUP + e:N_GROUPS + gg * EXP_PER_GROUP + e + 1]
             for gg in range(N_GROUPS)]
        le.append(jnp.where(gi == 0, r[0], jnp.where(gi == 1, r[1], jnp.where(gi == 2, r[2], r[3]))))
    me = jnp.maximum(jnp.maximum(le[0], le[1]), jnp.maximum(le[2], le[3]))
    ex = [jnp.exp(x - me) for x in le]

    def first_argmax(vals):
        mx = jnp.maximum(jnp.maximum(vals[0], vals[1]), jnp.maximum(vals[2], vals[3]))
        ix = jnp.where(vals[0] == mx, 0, jnp.where(vals[1] == mx, 1, jnp.where(vals[2] == mx, 2, 3)))
        return mx, ix.astype(jnp.int32)

    e1, i1 = first_argmax(ex)
    ex2 = [jnp.where(i1 == e, -1.0, ex[e]) for e in range(EXP_PER_GROUP)]
    e2, i2 = first_argmax(ex2)
    tot = e1 + e2
    w1 = e1 / tot * p_group
    w2 = e2 / tot * p_group
    first_low = i1 < i2
    a = jnp.where(first_low, i1, i2)
    b = jnp.where(first_low, i2, i1)
    pair = jnp.where(a == 0, b - 1, jnp.where(a == 1, b + 1, N_PAIRS - 1))
    bucket = gi * N_PAIRS + pair
    return bucket, jnp.where(first_low, w1, w2), jnp.where(first_low, w2, w1)


def _mix_out_kernel(a_ref, w_ref, x_ref, g1_ref, lng_ref, lnb_ref, sc2_ref, sh2_ref,
                    wrh_ref, wrl_ref, br_ref, x1_ref, gt_ref, gi_ref):
    y = jnp.dot(a_ref[...], w_ref[...], preferred_element_type=F32)
    x1 = _layer_norm(DN_ALPHA * x_ref[...] + g1_ref[0] * y, lng_ref[...], lnb_ref[...])
    x1_ref[...] = x1
    h2 = x1 * (1.0 + sc2_ref[0]) + sh2_ref[0]
    hi = h2.astype(BF16)
    lo = (h2 - hi.astype(F32)).astype(BF16)
    wrh = wrh_ref[...]
    logits = (jnp.dot(hi, wrh, preferred_element_type=F32)
              + (jnp.dot(hi, wrl_ref[...], preferred_element_type=F32)
                 + jnp.dot(lo, wrh, preferred_element_type=F32))) + br_ref[...]
    lt = logits.T
    bucket, gate_a, gate_b = _route_rows(lt)
    gi_ref[0] = bucket
    tm = lt.shape[1]
    row = lax.broadcasted_iota(jnp.int32, (8, tm), 0)
    g8 = jnp.where(row == 0, gate_a, jnp.where(row == 1, gate_b, 0.0))
    gt = jnp.concatenate([g8, jnp.zeros((ROUTE_W - 8, tm), F32)], axis=0)
    gt_ref[...] = gt.T


def _mix_out(tl, a, w, x, g1, lng, lnb, sc2, sh2, wrh, wrl, br, name):
    kd = a.shape[1]
    return pl.pallas_call(
        _mix_out_kernel,
        grid=(tl.n,),
        in_specs=[tl.rows(kd), _const_spec(w), tl.rows(D_MODEL), tl.mod_spec(g1),
                  _const_spec(lng), _const_spec(lnb), tl.mod_spec(sc2), tl.mod_spec(sh2),
                  _const_spec(wrh), _const_spec(wrl), _const_spec(br)],
        out_specs=[tl.rows(D_MODEL), tl.rows(ROUTE_W),
                   pl.BlockSpec((1, 1, tl.tm), lambda i: (i, 0, 0))],
        out_shape=[jax.ShapeDtypeStruct((tl.t, D_MODEL), F32),
                   jax.ShapeDtypeStruct((tl.t, ROUTE_W), F32),
                   jax.ShapeDtypeStruct((tl.n, 1, tl.tm), jnp.int32)],
        compiler_params=_cparams(("parallel",)),
        name=name,
    )(a, w, x, g1, lng, lnb, sc2, sh2, wrh, wrl, br)


TOK_SUB = D_MODEL // LANES


def _moe_sort_kernel(idx_ref, x1_ref, sc2_ref, sh2_ref, gt_ref, tok_ref, gs_ref, buf, gs_sc, sem, gsem, *, tm):
    i = pl.program_id(0)
    n = pl.num_programs(0)
    slot = i % 2

    def wait_slot(sl):
        pltpu.make_async_copy(buf.at[sl], tok_ref.at[pl.ds(0, tm * TOK_SUB)], sem.at[sl]).wait()

    @pl.when(i >= 2)
    def _():
        wait_slot(slot)

    h2 = x1_ref[...] * (1.0 + sc2_ref[0]) + sh2_ref[0]
    for s in range(TOK_SUB):
        buf[slot, pl.ds(s, tm, stride=TOK_SUB), :] = h2[:, s * LANES:(s + 1) * LANES]

    base = i * tm

    def issue(r, carry):
        d = idx_ref[base + r]
        pltpu.make_async_copy(buf.at[slot, pl.ds(pl.multiple_of(r * TOK_SUB, TOK_SUB), TOK_SUB)],
                              tok_ref.at[pl.ds(pl.multiple_of(d * TOK_SUB, TOK_SUB), TOK_SUB)],
                              sem.at[slot]).start()
        gs_sc[pl.ds(d, 1), :] = gt_ref[pl.ds(r, 1), :]
        return carry

    lax.fori_loop(0, tm, issue, 0, unroll=8)

    @pl.when(i == n - 1)
    def _():
        gates_out = pltpu.make_async_copy(gs_sc, gs_ref, gsem)
        gates_out.start()

        @pl.when(n >= 2)
        def _():
            wait_slot(1 - slot)
        wait_slot(slot)
        gates_out.wait()


def _moe_sort(tl, x1, sc2, sh2, gt, dst_idx, n_rows):
    tm = tl.tm
    n_steps = dst_idx.shape[0] // tm
    last = tl.n - 1
    tps = tl.tiles_per_seq
    kern = functools.partial(_moe_sort_kernel, tm=tm)
    rows = lambda w: pl.BlockSpec((tm, w), lambda i, idx: (jnp.minimum(i, last), 0))
    mod = lambda arr: pl.BlockSpec((1,) + arr.shape[1:], lambda i, idx: (jnp.minimum(i, last) // tps, 0, 0))
    return pl.pallas_call(
        kern,
        grid_spec=pltpu.PrefetchScalarGridSpec(
            num_scalar_prefetch=1,
            grid=(n_steps,),
            in_specs=[rows(D_MODEL), mod(sc2), mod(sh2), rows(ROUTE_W)],
            out_specs=[pl.BlockSpec(memory_space=pl.ANY), pl.BlockSpec(memory_space=pl.ANY)],
            scratch_shapes=[pltpu.VMEM((2, tm * TOK_SUB, LANES), F32), pltpu.VMEM((n_rows, ROUTE_W), F32),
                            pltpu.SemaphoreType.DMA((2,)), pltpu.SemaphoreType.DMA(())]),
        out_shape=[jax.ShapeDtypeStruct((n_rows * TOK_SUB, LANES), F32),
                   jax.ShapeDtypeStruct((n_rows, ROUTE_W), F32)],
        compiler_params=_cparams(("arbitrary",)),
        name="moe_sort_rows",
    )(dst_idx, x1, sc2, sh2, gt)


def _moe_kernel(ea_ref, eb_ref, nv_ref, tok_ref, gs_ref, wgu_a_ref, wgu_b_ref, wdn_a_ref, wdn_b_ref, o_ref,
                *, tme):
    i = pl.program_id(0)

    @pl.when(i < nv_ref[0])
    def _():
        hb = jnp.concatenate([tok_ref[pl.ds(s, tme, stride=TOK_SUB), :] for s in range(TOK_SUB)],
                             axis=1).astype(BF16)
        acc = None
        for lane, (wgu_ref, wdn_ref) in enumerate(((wgu_a_ref, wdn_a_ref), (wgu_b_ref, wdn_b_ref))):
            au = jnp.dot(hb, wgu_ref[0], preferred_element_type=F32)
            act = (_silu(au[:, :EXPERT_FF]) * au[:, EXPERT_FF:]).astype(BF16)
            y = jnp.dot(act, wdn_ref[0], preferred_element_type=F32)
            gy = gs_ref[:, lane:lane + 1] * y
            acc = gy if acc is None else acc + gy
        for s in range(TOK_SUB):
            o_ref[pl.ds(s, tme, stride=TOK_SUB), :] = acc[:, s * LANES:(s + 1) * LANES]

    @pl.when(i >= nv_ref[0])
    def _():
        o_ref[...] = jnp.zeros_like(o_ref)


def _moe_experts(tok, gs, tile_ea, tile_eb, n_valid, wgu, wdn, tme):
    n_tiles = gs.shape[0] // tme
    kern = functools.partial(_moe_kernel, tme=tme)
    gu = lambda pick: pl.BlockSpec((1, D_MODEL, 2 * EXPERT_FF), lambda i, ea, eb, nv: (pick(ea, eb)[i], 0, 0))
    dn = lambda pick: pl.BlockSpec((1, EXPERT_FF, D_MODEL), lambda i, ea, eb, nv: (pick(ea, eb)[i], 0, 0))
    first = lambda ea, eb: ea
    second = lambda ea, eb: eb
    return pl.pallas_call(
        kern,
        grid_spec=pltpu.PrefetchScalarGridSpec(
            num_scalar_prefetch=3,
            grid=(n_tiles,),
            in_specs=[pl.BlockSpec((tme * TOK_SUB, LANES), lambda i, ea, eb, nv: (i, 0)),
                      pl.BlockSpec((tme, ROUTE_W), lambda i, ea, eb, nv: (i, 0)),
                      gu(first), gu(second), dn(first), dn(second)],
            out_specs=pl.BlockSpec((tme * TOK_SUB, LANES), lambda i, ea, eb, nv: (i, 0))),
        out_shape=jax.ShapeDtypeStruct(tok.shape, F32),
        compiler_params=_cparams(("arbitrary",)),
        name="moe_experts",
    )(tile_ea, tile_eb, n_valid, tok, gs, wgu, wgu, wdn, wdn)


def _ln_res_kernel(idx_ref, fs_ref, x_ref, g_ref, lng_ref, lnb_ref, o_ref, buf0, buf1, buf2, sem, *, tm):
    i = pl.program_id(0)
    n = pl.num_programs(0)
    bufs = (buf0, buf1, buf2)
    n_buf = len(bufs)

    def row_copy(d, sl, row_off):
        return pltpu.make_async_copy(fs_ref.at[pl.ds(pl.multiple_of(d * TOK_SUB, TOK_SUB), TOK_SUB)],
                                     bufs[sl].at[pl.ds(row_off, TOK_SUB)], sem.at[sl])

    def wait_rows(sl):
        pltpu.make_async_copy(fs_ref.at[pl.ds(0, tm * TOK_SUB)], bufs[sl], sem.at[sl]).wait()

    def step_base(k):
        return jnp.minimum(k, n - 1) * tm

    @pl.when(i == 0)
    def _():
        for sl in range(n_buf - 1):
            base = step_base(sl)

            def body(r, carry):
                row_copy(idx_ref[base + r], sl, pl.multiple_of(r * TOK_SUB, TOK_SUB)).start()
                return carry

            lax.fori_loop(0, tm, body, 0, unroll=8)

    ahead = step_base(i + n_buf - 1)

    def step(sl):
        wait_rows(sl)
        f = jnp.concatenate([bufs[sl][pl.ds(s, tm, stride=TOK_SUB), :] for s in range(TOK_SUB)], axis=1)
        o_ref[...] = _layer_norm(DN_ALPHA * x_ref[...] + g_ref[0] * f, lng_ref[...], lnb_ref[...])
        for r in range(tm):
            row_copy(idx_ref[ahead + r], (sl + n_buf - 1) % n_buf, r * TOK_SUB).start()

    for sl in range(n_buf):
        @pl.when(i % n_buf == sl)
        def _():
            step(sl)

            @pl.when(i == n - 1)
            def _():
                for extra in range(1, n_buf):
                    wait_rows((sl + extra) % n_buf)


def _ln_res(tl, x, fs, dest, g2, lng, lnb):
    tm = tl.tm
    tps = tl.tiles_per_seq
    kern = functools.partial(_ln_res_kernel, tm=tm)
    const = lambda arr: pl.BlockSpec(arr.shape, lambda i, idx: (0,) * arr.ndim)
    return pl.pallas_call(
        kern,
        grid_spec=pltpu.PrefetchScalarGridSpec(
            num_scalar_prefetch=1,
            grid=(tl.n,),
            in_specs=[pl.BlockSpec(memory_space=pl.ANY),
                      pl.BlockSpec((tm, D_MODEL), lambda i, idx: (i, 0)),
                      pl.BlockSpec((1,) + g2.shape[1:], lambda i, idx: (i // tps, 0, 0)),
                      const(lng), const(lnb)],
            out_specs=pl.BlockSpec((tm, D_MODEL), lambda i, idx: (i, 0)),
            scratch_shapes=[pltpu.VMEM((tm * TOK_SUB, LANES), F32)] * 3 + [pltpu.SemaphoreType.DMA((3,))]),
        out_shape=jax.ShapeDtypeStruct((tl.t, D_MODEL), F32),
        compiler_params=_cparams(("arbitrary",)),
        name="ffn_residual_ln",
    )(dest, fs, x, g2, lng, lnb)


def _moe_ffn(tl, x1, sc2, sh2, gt, gi, g2, lng, lnb, wgu, wdn, layer):
    t = tl.t
    tme = EXPERT_TILE if t % TOKEN_TILE == 0 else 64
    n_tiles = t // tme + N_BUCKETS
    n_pad = N_BUCKETS * tme
    assert n_pad % tl.tm == 0
    i32 = jnp.int32
    onehot = (gi[:, None] == jnp.arange(N_BUCKETS, dtype=i32)[None, :]).astype(i32)
    counts = jnp.sum(onehot, axis=0)
    padded = ((counts + tme - 1) // tme) * tme
    ends = jnp.cumsum(padded)
    starts = ends - padded
    dest = jnp.sum(onehot * (jnp.cumsum(onehot, axis=0) - onehot + starts[None, :]), axis=1).astype(i32)
    cpad = jnp.cumsum(padded - counts)
    k = jnp.arange(n_pad, dtype=i32)
    seg_base = jnp.concatenate([starts + counts, ends[-1:]])
    seg_first = jnp.concatenate([jnp.zeros((1,), cpad.dtype), cpad])
    seg_hot = ((k[:, None] >= seg_first[None, :])
               & (k[:, None] < jnp.concatenate([cpad, jnp.full((1,), n_pad, cpad.dtype)])[None, :])).astype(i32)
    pad_rows = (k + jnp.sum(seg_hot * (seg_base - seg_first)[None, :], axis=1)).astype(i32)
    tile_start = jnp.arange(n_tiles, dtype=i32) * tme
    tile_hot = ((tile_start[:, None] >= starts[None, :]) & (tile_start[:, None] < ends[None, :])).astype(i32)
    bucket_ids = jnp.arange(N_BUCKETS, dtype=i32)
    first_expert = layer * N_EXPERTS + (bucket_ids // N_PAIRS) * EXP_PER_GROUP
    pair_a = jnp.asarray(PAIR_A * N_GROUPS, i32)
    pair_b = jnp.asarray(PAIR_B * N_GROUPS, i32)
    last_expert = layer * N_EXPERTS + N_EXPERTS - 1
    in_use = jnp.sum(tile_hot, axis=1)
    tile_ea = (jnp.sum(tile_hot * (first_expert + pair_a)[None, :], axis=1) + (1 - in_use) * last_expert).astype(i32)
    tile_eb = (jnp.sum(tile_hot * (first_expert + pair_b)[None, :], axis=1) + (1 - in_use) * last_expert).astype(i32)
    n_valid = (ends[-1] // tme).astype(i32).reshape(1)
    tok, gs = _moe_sort(tl, x1, sc2, sh2, gt, jnp.concatenate([dest, pad_rows]), n_tiles * tme)
    fs = _moe_experts(tok, gs, tile_ea, tile_eb, n_valid, wgu, wdn, tme)
    return _ln_res(tl, x1, fs, dest, g2, lng, lnb)


def _mla_in_kernel(x_ref, sc_ref, sh_ref, wdkv_ref, kvg_ref, wukv_ref, wdq_ref, qg_ref, wuq_ref, cos_ref, sin_ref,
                   lat_ref, kr_ref, q_ref, *kv_refs, qscale, expand):
    x = x_ref[...]
    cos = cos_ref[...]
    sin = sin_ref[...]
    kv = jnp.dot(x.astype(BF16), wdkv_ref[...], preferred_element_type=F32)
    c = kv[:, :KV_LORA]
    lat = c * lax.rsqrt(jnp.mean(c * c, axis=-1, keepdims=True) + RMS_EPS) * kvg_ref[...]
    lat_ref[...] = lat
    kr = kv[:, KV_LORA:KV_LORA + LANES] * cos + kv[:, KV_LORA + LANES:] * sin
    kr_ref[...] = kr[:, :MLA_ROPE]
    if expand:
        k_ref, v_ref = kv_refs
        kvx = jnp.dot(lat.astype(BF16), wukv_ref[...], preferred_element_type=F32)
        krb = kr.astype(BF16)
        ones_col = jnp.where(lax.broadcasted_iota(jnp.int32, krb.shape, 1) == 0, 1.0, 0.0).astype(BF16)
        v0 = MLA_HEADS * MLA_NOPE
        for hh in range(MLA_HEADS):
            k_ref[:, hh * HEAD_W:hh * HEAD_W + MLA_NOPE] = kvx[:, hh * MLA_NOPE:(hh + 1) * MLA_NOPE].astype(BF16)
            k_ref[:, hh * HEAD_W + MLA_NOPE:(hh + 1) * HEAD_W] = krb
            v_ref[:, hh * HEAD_W:hh * HEAD_W + MLA_V] = kvx[:, v0 + hh * MLA_V:v0 + (hh + 1) * MLA_V].astype(BF16)
            v_ref[:, hh * HEAD_W + MLA_V:(hh + 1) * HEAD_W] = ones_col
    else:
        (krp_ref,) = kv_refs
        krp_ref[...] = kr
    h = (x * (1.0 + sc_ref[0]) + sh_ref[0]).astype(BF16)
    cq = jnp.dot(h, wdq_ref[...], preferred_element_type=F32)
    qn = (cq * lax.rsqrt(jnp.mean(cq * cq, axis=-1, keepdims=True) + RMS_EPS) * qg_ref[...]).astype(BF16)
    nw = MLA_HEADS * LANES
    qnope = jnp.dot(qn, wuq_ref[:, :nw], preferred_element_type=F32)
    qpe = jnp.dot(qn, wuq_ref[:, nw:2 * nw], preferred_element_type=F32)
    qpe_sw = jnp.dot(qn, wuq_ref[:, 2 * nw:], preferred_element_type=F32)
    for hh in range(MLA_HEADS):
        cols = slice(hh * LANES, (hh + 1) * LANES)
        q_ref[:, hh * HEAD_W:hh * HEAD_W + LANES] = (qnope[:, cols] * qscale).astype(BF16)
        pe = qpe[:, cols] * cos + qpe_sw[:, cols] * sin
        q_ref[:, hh * HEAD_W + LANES:(hh + 1) * HEAD_W] = (pe * qscale).astype(BF16)


def _mla_in(tl, x, sc, sh, p, j, cos, sin, qscale, expand):
    kern = functools.partial(_mla_in_kernel, qscale=qscale, expand=expand)
    weights = [p["w_dkv_ext"], p["kv_norm_g"], p["w_ukv_r"], p["w_dq"][j], p["q_norm_g"][j], p["w_uq_ext"][j]]
    out_w = [(KV_LORA, F32), (MLA_ROPE, F32), (MLA_HEADS * HEAD_W, BF16)]
    out_w += [(MLA_HEADS * HEAD_W, BF16)] * 2 if expand else [(LANES, F32)]
    return pl.pallas_call(
        kern,
        grid=(tl.n,),
        in_specs=[tl.rows(D_MODEL), tl.mod_spec(sc), tl.mod_spec(sh)] + [_const_spec(w) for w in weights]
                 + [tl.table_spec(cos), tl.table_spec(sin)],
        out_specs=[tl.rows(w) for w, _ in out_w],
        out_shape=[jax.ShapeDtypeStruct((tl.t, w), dt) for w, dt in out_w],
        compiler_params=_cparams(("parallel",)),
        name="mla_in_proj",
    )(x, sc, sh, *weights, cos, sin)


def _attn_kernel(q_ref, qn_ref, k_ref, v_ref, o_ref, m_sc, acc_sc, sa_sc, sb_sc, sc_sc, *, t, n_sub, hp):
    i = pl.program_id(2)
    m_sc[...] = jnp.full(m_sc.shape, NEG_INF, F32)
    acc_sc[...] = jnp.zeros(acc_sc.shape, F32)
    rs = t // n_sub

    def scores(j, s_ref, qr=q_ref):
        for h in range(hp):
            hc = slice(h * HEAD_W, (h + 1) * HEAD_W)
            k = k_ref[0, pl.ds(pl.multiple_of(j * t, t), t), hc]
            s_ref[h] = lax.dot_general(qr[0, :, hc], k, (((1,), (1,)), ((), ())), preferred_element_type=F32)

    def fold(j, s_ref, diagonal):
        for h in range(hp):
            v = v_ref[0, pl.ds(pl.multiple_of(j * t, t), t), h * HEAD_W:(h + 1) * HEAD_W]
            for u in range(n_sub):
                rows = slice(u * rs, (u + 1) * rs)
                s = s_ref[h, rows, :]
                if diagonal:
                    qpos = u * rs + lax.broadcasted_iota(jnp.int32, (rs, 1), 0)
                    kpos = lax.broadcasted_iota(jnp.int32, (1, t), 1)
                    s = jnp.where((kpos >> LOG2_CHUNK) <= (qpos >> LOG2_CHUNK), s, NEG_INF)
                m_old = m_sc[h, rows, :]
                m_new = jnp.maximum(m_old, jnp.max(s, axis=1, keepdims=True))
                alpha = jnp.exp2(m_old - m_new)
                p = jnp.exp2(s - jnp.tile(m_new, (1, t // LANES)))
                pv = jnp.dot(p.astype(BF16), v, preferred_element_type=F32)
                acc_sc[h, rows, :] = jnp.tile(alpha, (1, HEAD_W // LANES)) * acc_sc[h, rows, :] + pv
                m_sc[h, rows, :] = m_new

    def next_tile_scores():
        scores(0, sc_sc, qn_ref)

    @pl.when(i == 0)
    def _():
        scores(0, sc_sc)
        fold(0, sc_sc, True)
        next_tile_scores()

    @pl.when(i == 1)
    def _():
        scores(1, sb_sc)
        fold(0, sc_sc, False)
        next_tile_scores()
        fold(1, sb_sc, True)

    @pl.when(i >= 2)
    def _():
        scores(1, sb_sc)
        fold(0, sc_sc, False)
        scores(2, sa_sc)
        fold(1, sb_sc, False)

        def pair_step(jj, carry):
            j = 2 * jj
            scores(j + 1, sb_sc)
            fold(j, sa_sc, False)
            scores(j + 2, sa_sc)
            fold(j + 1, sb_sc, False)
            return carry

        lax.fori_loop(1, i // 2, pair_step, 0)

        @pl.when(i % 2 == 0)
        def _():
            next_tile_scores()
            fold(i, sa_sc, True)

        @pl.when(i % 2 == 1)
        def _():
            scores(i, sb_sc)
            fold(i - 1, sa_sc, False)
            next_tile_scores()
            fold(i, sb_sc, True)

    for h in range(hp):
        acc = acc_sc[h]
        o_ref[0, :, h * MLA_V:(h + 1) * MLA_V] = (acc[:, :MLA_V] / acc[:, MLA_V:MLA_V + 1]).astype(BF16)


def _attention(qc, kc, vv, b, l, t, hp):
    assert l % t == 0 and t % CHUNK == 0
    n_sub = 2 if t % (2 * LANES) == 0 else 1
    nq = l // t
    kern = functools.partial(_attn_kernel, t=t, n_sub=n_sub, hp=hp)
    scores_buf = pltpu.VMEM((hp, t, t), F32)
    return pl.pallas_call(
        kern,
        grid=(b, MLA_HEADS // hp, nq),
        in_specs=[pl.BlockSpec((1, t, hp * HEAD_W), lambda bi, h, i: (bi, i, h)),
                  pl.BlockSpec((1, t, hp * HEAD_W), lambda bi, h, i: (bi, jnp.minimum(i + 1, nq - 1), h)),
                  pl.BlockSpec((1, l, hp * HEAD_W), lambda bi, h, i: (bi, 0, h)),
                  pl.BlockSpec((1, l, hp * HEAD_W), lambda bi, h, i: (bi, 0, h))],
        out_specs=pl.BlockSpec((1, t, hp * MLA_V), lambda bi, h, i: (bi, i, h)),
        out_shape=jax.ShapeDtypeStruct((b, l, MLA_HEADS * MLA_V), BF16),
        scratch_shapes=[pltpu.VMEM((hp, t, LANES), F32), pltpu.VMEM((hp, t, HEAD_W), F32),
                        scores_buf, scores_buf, scores_buf],
        compiler_params=_cparams(("parallel", "parallel", "arbitrary"), ATTN_VMEM_LIMIT),
        name="mla_attention",
    )(qc.reshape(b, l, MLA_HEADS * HEAD_W), qc.reshape(b, l, MLA_HEADS * HEAD_W),
      kc.reshape(b, l, MLA_HEADS * HEAD_W), vv.reshape(b, l, MLA_HEADS * HEAD_W))


def _decode_attn_kernel(q_ref, lnew_ref, knew_ref, lpast_ref, kpast_ref, wukt_ref, wuv_ref, o_ref, qa_sc,
                        *, l, pos0, past):
    nope_w = KV_LORA
    for h in range(MLA_HEADS):
        rows = slice(h * l, (h + 1) * l)
        qn = q_ref[0, :, h * HEAD_W:h * HEAD_W + MLA_NOPE]
        qa_sc[rows, :nope_w] = jnp.dot(qn, wukt_ref[h], preferred_element_type=F32).astype(BF16)
        qa_sc[rows, nope_w:] = q_ref[0, :, h * HEAD_W + MLA_NOPE:(h + 1) * HEAD_W]
    q_lat = qa_sc[:, :nope_w]
    q_pe = qa_sc[:, nope_w:nope_w + MLA_ROPE]
    nt = (((1,), (1,)), ((), ()))
    lp = lpast_ref[0].astype(BF16)
    kp = kpast_ref[0].astype(BF16)
    ln = lnew_ref[0].astype(BF16)
    kn = knew_ref[0][:, :MLA_ROPE].astype(BF16)
    s_past = (lax.dot_general(q_lat, lp, nt, preferred_element_type=F32)
              + lax.dot_general(q_pe, kp, nt, preferred_element_type=F32))
    s_new = (lax.dot_general(q_lat, ln, nt, preferred_element_type=F32)
             + lax.dot_general(q_pe, kn, nt, preferred_element_type=F32))
    n_rows = MLA_HEADS * l
    qpos = pos0 + lax.rem(lax.broadcasted_iota(jnp.int32, (n_rows, 1), 0), l)
    kpos_new = pos0 + lax.broadcasted_iota(jnp.int32, (1, l), 1)
    s_new = jnp.where((kpos_new >> LOG2_CHUNK) <= (qpos >> LOG2_CHUNK), s_new, NEG_INF)
    kpos_past = lax.broadcasted_iota(jnp.int32, (1, past), 1)
    s_past = jnp.where((kpos_past >> LOG2_CHUNK) <= (qpos >> LOG2_CHUNK), s_past, NEG_INF)
    m = jnp.maximum(jnp.max(s_past, axis=1, keepdims=True), jnp.max(s_new, axis=1, keepdims=True))
    p_past = jnp.exp2(s_past - m)
    p_new = jnp.exp2(s_new - m)
    denom = jnp.sum(p_past, axis=1, keepdims=True) + jnp.sum(p_new, axis=1, keepdims=True)
    ctx = (jnp.dot(p_past.astype(BF16), lp, preferred_element_type=F32)
           + jnp.dot(p_new.astype(BF16), ln, preferred_element_type=F32)) / denom
    ctx = ctx.astype(BF16)
    for h in range(MLA_HEADS):
        o_ref[0, :, h * MLA_V:(h + 1) * MLA_V] = jnp.dot(
            ctx[h * l:(h + 1) * l], wuv_ref[h], preferred_element_type=F32).astype(BF16)


def _decode_attention(qc, latent_new, krp_new, past_latent, past_k_rope, w_uk_t, w_uv, b, l, pos0):
    past = past_latent.shape[1]
    kern = functools.partial(_decode_attn_kernel, l=l, pos0=pos0, past=past)
    per_b = lambda shape: pl.BlockSpec((1,) + shape, lambda bi: (bi, 0, 0))
    return pl.pallas_call(
        kern,
        grid=(b,),
        in_specs=[per_b((l, MLA_HEADS * HEAD_W)), per_b((l, KV_LORA)), per_b((l, LANES)),
                  per_b((past, KV_LORA)), per_b((past, MLA_ROPE)), _const_spec(w_uk_t), _const_spec(w_uv)],
        out_specs=per_b((l, MLA_HEADS * MLA_V)),
        out_shape=jax.ShapeDtypeStruct((b, l, MLA_HEADS * MLA_V), BF16),
        scratch_shapes=[pltpu.VMEM((MLA_HEADS * l, KV_LORA + LANES), BF16)],
        compiler_params=_cparams(("parallel",)),
        name="mla_decode_attention",
    )(qc.reshape(b, l, MLA_HEADS * HEAD_W), latent_new.reshape(b, l, KV_LORA), krp_new.reshape(b, l, LANES),
      past_latent, past_k_rope, w_uk_t, w_uv)


def _rope_tables(pos0, length, half):
    inv = ROPE_THETA ** (-np.arange(half, dtype=np.float64) / half)
    ang = (pos0 + np.arange(length, dtype=np.float64))[:, None] * inv[None, :]
    return np.cos(ang), np.sin(ang)


def _mla_rope_tables(pos0, length):
    cos, sin = _rope_tables(pos0, length, MLA_ROPE // 2)
    z = np.zeros((length, LANES - MLA_ROPE))
    return (jnp.asarray(np.concatenate([cos, cos, z], axis=1), F32),
            jnp.asarray(np.concatenate([-sin, sin, z], axis=1), F32))


def _swap_halves(w):
    half = w.shape[-1] // 2
    return jnp.concatenate([w[..., half:], w[..., :half]], axis=-1)


def _pad_lanes(w):
    return jnp.pad(w, [(0, 0)] * (w.ndim - 1) + [(0, LANES - w.shape[-1])])


def _prep_weights(w_ret_in, w_ret_out, w_dq, q_norm_g, w_uq, w_mla_out, w_dkv, kv_norm_g, w_ukv,
                  w_route_group, b_route_group, w_route_expert, b_route_expert,
                  w_expert_gate_up, w_expert_down):
    p = {}
    p["w_ret_in"] = [w_ret_in[i].astype(BF16) for i in range(N_A)]
    p["w_ret_out"] = [w_ret_out[i].astype(BF16) for i in range(N_A)]
    p["w_dq"] = [w_dq[j].astype(BF16) for j in range(DEPTH - N_A)]
    p["q_norm_g"] = [q_norm_g[j][None, :] for j in range(DEPTH - N_A)]
    w_uq_ext = []
    for j in range(DEPTH - N_A):
        wq = w_uq[j].reshape(Q_LORA, MLA_HEADS, MLA_NOPE + MLA_ROPE)
        nope = wq[:, :, :MLA_NOPE].reshape(Q_LORA, MLA_HEADS * LANES)
        pe = wq[:, :, MLA_NOPE:]
        w_uq_ext.append(jnp.concatenate(
            [nope, _pad_lanes(pe).reshape(Q_LORA, MLA_HEADS * LANES),
             _pad_lanes(_swap_halves(pe)).reshape(Q_LORA, MLA_HEADS * LANES)], axis=1).astype(BF16))
    p["w_uq_ext"] = w_uq_ext
    p["w_mla_out"] = [w_mla_out[j].astype(BF16) for j in range(DEPTH - N_A)]
    kr = w_dkv[:, KV_LORA:]
    p["w_dkv_ext"] = jnp.concatenate([w_dkv[:, :KV_LORA], _pad_lanes(kr), _pad_lanes(_swap_halves(kr))],
                                     axis=1).astype(BF16)
    p["kv_norm_g"] = kv_norm_g[None, :]
    w_ukv4 = w_ukv.reshape(KV_LORA, MLA_HEADS, 2, MLA_NOPE)
    p["w_ukv_r"] = w_ukv4.transpose(0, 2, 1, 3).reshape(KV_LORA, 2 * MLA_HEADS * MLA_NOPE).astype(BF16)
    p["w_uk_t"] = w_ukv4[:, :, 0, :].transpose(1, 2, 0).astype(BF16)
    p["w_uv"] = w_ukv4[:, :, 1, :].transpose(1, 0, 2).astype(BF16)
    wr = jnp.concatenate([w_route_group, w_route_expert], axis=-1)
    wr = jnp.pad(wr, ((0, 0), (0, 0), (0, ROUTE_W - wr.shape[-1])))
    wr_hi = wr.astype(BF16)
    wr_lo = (wr - wr_hi.astype(F32)).astype(BF16)
    br = jnp.concatenate([b_route_group, b_route_expert], axis=-1)
    br = jnp.pad(br, ((0, 0), (0, ROUTE_W - br.shape[-1])))
    p["wr_hi"] = [wr_hi[l] for l in range(DEPTH)]
    p["wr_lo"] = [wr_lo[l] for l in range(DEPTH)]
    p["br"] = [br[l][None, :] for l in range(DEPTH)]
    p["w_gu"] = w_expert_gate_up.astype(BF16).reshape(DEPTH * N_EXPERTS, D_MODEL, 2 * EXPERT_FF)
    p["w_dn"] = w_expert_down.astype(BF16).reshape(DEPTH * N_EXPERTS, EXPERT_FF, D_MODEL)
    return p


def _trunk(x3, mod, pos0, ret_s0, past_latent, past_k_rope, ret_chunk, ret_block, ln_g, ln_b, p):
    b, l, _ = x3.shape
    tl = _Tiles(b, l)
    x = x3.reshape(tl.t, D_MODEL)
    new_ret = []
    latent_new = k_rope_new = None
    for layer in range(DEPTH):
        sh1, sc1, g1, sh2, sc2, g2 = [tl.mod(m) for m in jnp.split(mod[layer], 6, axis=-1)]
        lng = [ln_g[layer, s][None, :] for s in range(2)]
        lnb = [ln_b[layer, s][None, :] for s in range(2)]
        if layer < N_A:
            cos, sin = [jnp.asarray(tab, F32) for tab in _rope_tables(pos0, l, RET_DK // 2)]
            q, k, v, sg = _ret_in(tl, x, sc1, sh1, p["w_ret_in"][layer], tl.table(cos), tl.table(sin))
            a, s_new = _retention(b, l, q, k, v, sg, ret_s0[layer], ret_chunk, ret_block)
            a = a.reshape(tl.t, RET_HEADS * RET_DV)
            new_ret.append(s_new)
            w_out = p["w_ret_out"][layer]
            name = "ret_out_ln_route"
        else:
            j = layer - N_A
            cos, sin = _mla_rope_tables(pos0, l)
            cos, sin = tl.table(cos), tl.table(sin)
            assert layer == N_A == DEPTH - 1
            qscale = float((MLA_NOPE + MLA_ROPE) ** -0.5 * math.log2(math.e))
            if past_latent is None:
                assert pos0 == 0
                latent_new, k_rope_new, qc, kc, vv = _mla_in(tl, x, sc1, sh1, p, j, cos, sin, qscale, True)
                a = _attention(qc, kc, vv, b, l, ATTN_TILE, 2)
            else:
                latent_new, k_rope_new, qc, krp = _mla_in(tl, x, sc1, sh1, p, j, cos, sin, qscale, False)
                a = _decode_attention(qc, latent_new, krp, past_latent, past_k_rope, p["w_uk_t"], p["w_uv"],
                                      b, l, pos0)
            a = a.reshape(tl.t, MLA_HEADS * MLA_V)
            w_out = p["w_mla_out"][j]
            name = "mla_out_ln_route"
        x1, gt, gi = _mix_out(tl, a, w_out, x, g1, lng[0], lnb[0], sc2, sh2,
                              p["wr_hi"][layer], p["wr_lo"][layer], p["br"][layer], name)
        x = _moe_ffn(tl, x1, sc2, sh2, gt, gi.reshape(tl.t), g2, lng[1], lnb[1],
                     p["w_gu"], p["w_dn"], layer)
    return (x.reshape(b, l, D_MODEL), jnp.stack(new_ret), latent_new.reshape(b, l, KV_LORA),
            k_rope_new.reshape(b, l, MLA_ROPE))


def kernel(x_prompt, x_sample, state_retention, cache_kv_latent, cache_k_rope, c_prompt, c_sample,
           w_ada, b_ada, ln_g, ln_b, w_ret_in, w_ret_out, w_dq, q_norm_g, w_uq, w_mla_out,
           w_dkv, kv_norm_g, w_ukv, w_route_group, b_route_group, w_route_expert, b_route_expert,
           w_expert_gate_up, w_expert_down):
    bp, lp, _ = x_prompt.shape
    bs, ls, _ = x_sample.shape
    p = _prep_weights(w_ret_in, w_ret_out, w_dq, q_norm_g, w_uq, w_mla_out, w_dkv, kv_norm_g, w_ukv,
                      w_route_group, b_route_group, w_route_expert, b_route_expert,
                      w_expert_gate_up, w_expert_down)
    n_seq = bp + bs
    n_rows = -(-n_seq // 8) * 8
    c_all = jnp.concatenate([c_prompt, c_sample, jnp.zeros((n_rows - n_seq, D_MODEL), F32)], axis=0)
    mod = _ada(c_all, w_ada, b_ada)
    s0_p = jnp.zeros((N_A, bp, RET_HEADS, RET_DK, RET_DV), F32)
    ret_chunk_p = RET_CHUNK if lp % RET_CHUNK == 0 else CHUNK
    y_p, st_p, lat_p, kr_p = _trunk(x_prompt, mod[:, :bp], 0, s0_p, None, None,
                                    ret_chunk_p, max(ret_chunk_p, min(lp, TOKEN_TILE)), ln_g, ln_b, p)
    past = cache_kv_latent.shape[1]
    y_s, st_s, lat_s, kr_s = _trunk(x_sample, mod[:, bp:n_seq], past, state_retention,
                                    cache_kv_latent, cache_k_rope, ls, ls, ln_g, ln_b, p)
    return (y_p, y_s, st_p, st_s, lat_p, kr_p, lat_s, kr_s)
```

```python
import functools
import math

import numpy as np
import jax
import jax.numpy as jnp
from jax import lax
from jax.experimental import pallas as pl
from jax.experimental.pallas import tpu as pltpu

F32 = jnp.float32
BF16 = jnp.bfloat16

D_MODEL = 1024
DEPTH = 2
CHUNK = 64
LOG2_CHUNK = 6
N_A = DEPTH // 2
RET_HEADS = 4
RET_DK = 256
RET_DV = 512
MLA_HEADS = 8
MLA_NOPE = 128
MLA_ROPE = 64
MLA_V = 128
Q_LORA = 384
KV_LORA = 256
ROPE_THETA = 10000.0
N_GROUPS = 4
EXP_PER_GROUP = 4
N_EXPERTS = N_GROUPS * EXP_PER_GROUP
N_PAIRS = EXP_PER_GROUP * (EXP_PER_GROUP - 1) // 2
N_BUCKETS = N_GROUPS * N_PAIRS
PAIR_A = (0, 0, 0, 1, 1, 2)
PAIR_B = (1, 2, 3, 2, 3, 3)
EXPERT_FF = 512
LN_EPS = 1e-5
RMS_EPS = 1e-6
DN_ALPHA = (2 * DEPTH) ** 0.25
NEG_INF = -1e30

LANES = 128
ROUTE_W = LANES
TOKEN_TILE = 512
EXPERT_TILE = 256
RET_CHUNK = 256
ATTN_TILE = 512
HEAD_W = 2 * LANES
VMEM_LIMIT = 56 * 1024 * 1024
ATTN_VMEM_LIMIT = 56 * 1024 * 1024


def _cparams(sem, vmem=VMEM_LIMIT):
    return pltpu.CompilerParams(dimension_semantics=sem, vmem_limit_bytes=vmem)


def _silu(x):
    return x * jax.nn.sigmoid(x)


def _layer_norm(z, g, b):
    mu = jnp.mean(z, axis=-1, keepdims=True)
    zc = z - mu
    var = jnp.mean(zc * zc, axis=-1, keepdims=True)
    return zc * lax.rsqrt(var + LN_EPS) * g + b


def _ada_kernel(c_ref, w_ref, b_ref, o_ref):
    s = _silu(c_ref[...]).astype(BF16)
    o_ref[0] = jnp.dot(s, w_ref[0].astype(BF16), preferred_element_type=F32) + b_ref[0]


def _ada(c_all, w_ada, b_ada):
    r = c_all.shape[0]
    n = w_ada.shape[-1]
    tn = 1536
    return pl.pallas_call(
        _ada_kernel,
        grid=(DEPTH, n // tn),
        in_specs=[pl.BlockSpec((r, D_MODEL), lambda l, j: (0, 0)),
                  pl.BlockSpec((1, D_MODEL, tn), lambda l, j: (l, 0, j)),
                  pl.BlockSpec((1, 1, tn), lambda l, j: (l, 0, j))],
        out_specs=pl.BlockSpec((1, r, tn), lambda l, j: (l, 0, j)),
        out_shape=jax.ShapeDtypeStruct((DEPTH, r, n), F32),
        compiler_params=_cparams(("parallel", "parallel")),
        name="ada_mod",
    )(c_all, w_ada, b_ada.reshape(DEPTH, 1, n))


class _Tiles:
    def __init__(self, b, l):
        self.b, self.l, self.t = b, l, b * l
        self.per_token = (l % TOKEN_TILE) != 0
        self.tm = self.t if self.per_token else TOKEN_TILE
        self.n = self.t // self.tm
        self.tiles_per_seq = 1 if self.per_token else l // self.tm

    def mod(self, m):
        if self.per_token:
            return jnp.repeat(m, self.l, axis=0)[None]
        return m[:, None, :]

    def mod_spec(self, arr):
        tps = self.tiles_per_seq
        return pl.BlockSpec((1,) + arr.shape[1:], lambda i: (i // tps, 0, 0))

    def table(self, tab):
        return jnp.tile(tab, (self.b, 1)) if self.per_token else tab

    def table_spec(self, tab):
        nt = tab.shape[0] // self.tm
        return pl.BlockSpec((self.tm, tab.shape[1]), lambda i: (i % nt, 0))

    def rows(self, w):
        return pl.BlockSpec((self.tm, w), lambda i: (i, 0))


def _const_spec(arr):
    nd = arr.ndim
    return pl.BlockSpec(arr.shape, lambda *_: (0,) * nd)


def _ret_in_kernel(x_ref, sc_ref, sh_ref, w_ref, cos_ref, sin_ref, q_ref, k_ref, v_ref, sg_ref):
    h = (x_ref[...] * (1.0 + sc_ref[0]) + sh_ref[0]).astype(BF16)
    cos = cos_ref[...]
    sin = sin_ref[...]
    qk_w = RET_HEADS * RET_DK
    half = RET_DK // 2

    def rope_store(r, out_ref, scale):
        for hh in range(RET_HEADS):
            a = r[:, hh * RET_DK:hh * RET_DK + half]
            b = r[:, hh * RET_DK + half:(hh + 1) * RET_DK]
            out_ref[:, hh * RET_DK:hh * RET_DK + half] = ((a * cos - b * sin) * scale).astype(BF16)
            out_ref[:, hh * RET_DK + half:(hh + 1) * RET_DK] = ((a * sin + b * cos) * scale).astype(BF16)

    r = jnp.dot(h, w_ref[:, 0:qk_w], preferred_element_type=F32)
    rope_store(r, q_ref, 1.0)
    r = jnp.dot(h, w_ref[:, qk_w:2 * qk_w], preferred_element_type=F32)
    rope_store(r, k_ref, RET_DK ** -0.5)
    for c in range(2):
        lo = 2 * qk_w + c * qk_w
        v_ref[:, c * qk_w:(c + 1) * qk_w] = jnp.dot(
            h, w_ref[:, lo:lo + qk_w], preferred_element_type=F32).astype(BF16)
    for c in range(2):
        lo = 4 * qk_w + c * qk_w
        g = jnp.dot(h, w_ref[:, lo:lo + qk_w], preferred_element_type=F32)
        sg_ref[:, c * qk_w:(c + 1) * qk_w] = _silu(g).astype(BF16)


def _ret_in(tl, x, sc, sh, w_in, cos, sin):
    qk_w, v_w = RET_HEADS * RET_DK, RET_HEADS * RET_DV
    return pl.pallas_call(
        _ret_in_kernel,
        grid=(tl.n,),
        in_specs=[tl.rows(D_MODEL), tl.mod_spec(sc), tl.mod_spec(sh), _const_spec(w_in),
                  tl.table_spec(cos), tl.table_spec(sin)],
        out_specs=[tl.rows(qk_w), tl.rows(qk_w), tl.rows(v_w), tl.rows(v_w)],
        out_shape=[jax.ShapeDtypeStruct((tl.t, qk_w), BF16), jax.ShapeDtypeStruct((tl.t, qk_w), BF16),
                   jax.ShapeDtypeStruct((tl.t, v_w), BF16), jax.ShapeDtypeStruct((tl.t, v_w), BF16)],
        compiler_params=_cparams(("parallel",)),
        name="ret_in_proj",
    )(x, sc, sh, w_in, cos, sin)


def _retention_tables(chunk):
    lg = np.log1p(-np.exp2(-5.0 - np.arange(RET_HEADS, dtype=np.float64)))
    idx = np.arange(chunk, dtype=np.float64)
    diff = idx[:, None] - idx[None, :]
    dmask = np.where(diff >= 0, np.exp(lg[:, None, None] * np.maximum(diff, 0.0)), 0.0)
    xi = np.exp(lg[:, None] * (idx[None, :] + 1.0))[:, :, None]
    zeta = np.exp(lg[:, None] * (chunk - 1.0 - idx[None, :]))[:, :, None]
    cdec = np.exp(lg * chunk)
    return (jnp.asarray(dmask, F32), jnp.asarray(xi, F32), jnp.asarray(zeta, F32),
            [float(np.float32(c)) for c in cdec])


def _retention_kernel(q_ref, k_ref, v_ref, sg_ref, s0_ref, dm_ref, xi_ref, zeta_ref,
                      o_ref, sout_ref, s_sc, *, chunk, n_chunks, cdec):
    j = pl.program_id(1)

    @pl.when(j == 0)
    def _():
        s_sc[...] = s0_ref[0]

    for c in range(n_chunks):
        rows = slice(c * chunk, (c + 1) * chunk)
        for hh in range(RET_HEADS):
            kcols = slice(hh * RET_DK, (hh + 1) * RET_DK)
            vcols = slice(hh * RET_DV, (hh + 1) * RET_DV)
            q = q_ref[0, rows, kcols]
            k = k_ref[0, rows, kcols]
            v = v_ref[0, rows, vcols]
            s_old = s_sc[hh]
            sc = lax.dot_general(q, k, (((1,), (1,)), ((), ())), preferred_element_type=F32) * dm_ref[hh]
            o = jnp.dot(sc.astype(BF16), v, preferred_element_type=F32)
            o = o + jnp.dot(q, s_old.astype(BF16), preferred_element_type=F32) * xi_ref[hh]
            kz_t = (k.astype(F32) * zeta_ref[hh]).T.astype(BF16)
            s_sc[hh] = s_old * cdec[hh] + jnp.dot(kz_t, v, preferred_element_type=F32)
            mu = jnp.mean(o, axis=-1, keepdims=True)
            oc = o - mu
            var = jnp.mean(oc * oc, axis=-1, keepdims=True)
            on = oc * lax.rsqrt(var + LN_EPS)
            o_ref[0, rows, vcols] = (sg_ref[0, rows, vcols].astype(F32) * on).astype(BF16)

    @pl.when(j == pl.num_programs(1) - 1)
    def _():
        sout_ref[0] = s_sc[...]


def _retention(b, l, q, k, v, sg, s0, chunk, block):
    qk_w, v_w = RET_HEADS * RET_DK, RET_HEADS * RET_DV
    dmask, xi, zeta, cdec = _retention_tables(chunk)
    kern = functools.partial(_retention_kernel, chunk=chunk, n_chunks=block // chunk, cdec=cdec)
    seq = lambda w: pl.BlockSpec((1, block, w), lambda bi, j: (bi, j, 0))
    st = pl.BlockSpec((1, RET_HEADS, RET_DK, RET_DV), lambda bi, j: (bi, 0, 0, 0))
    return pl.pallas_call(
        kern,
        grid=(b, l // block),
        in_specs=[seq(qk_w), seq(qk_w), seq(v_w), seq(v_w), st,
                  _const_spec(dmask), _const_spec(xi), _const_spec(zeta)],
        out_specs=[seq(v_w), st],
        out_shape=[jax.ShapeDtypeStruct((b, l, v_w), BF16),
                   jax.ShapeDtypeStruct((b, RET_HEADS, RET_DK, RET_DV), F32)],
        scratch_shapes=[pltpu.VMEM((RET_HEADS, RET_DK, RET_DV), F32)],
        compiler_params=_cparams(("parallel", "arbitrary")),
        name="retention_scan",
    )(q.reshape(b, l, qk_w), k.reshape(b, l, qk_w), v.reshape(b, l, v_w), sg.reshape(b, l, v_w),
      s0, dmask, xi, zeta)


def _route_rows(lt):
    g = [lt[i:i + 1] for i in range(N_GROUPS)]
    m = jnp.maximum(jnp.maximum(g[0], g[1]), jnp.maximum(g[2], g[3]))
    gi = jnp.where(g[0] == m, 0, jnp.where(g[1] == m, 1, jnp.where(g[2] == m, 2, 3))).astype(jnp.int32)
    denom = jnp.exp(g[0] - m) + jnp.exp(g[1] - m) + jnp.exp(g[2] - m) + jnp.exp(g[3] - m)
    p_group = 1.0 / denom
    le = []
    for e in range(EXP_PER_GROUP):
        r = [lt[N_GROUPS + gg * EXP_PER_GROUP + e:N_GROUPS + gg * EXP_PER_GROUP + e + 1]
             for gg in range(N_GROUPS)]
        le.append(jnp.where(gi == 0, r[0], jnp.where(gi == 1, r[1], jnp.where(gi == 2, r[2], r[3]))))
    me = jnp.maximum(jnp.maximum(le[0], le[1]), jnp.maximum(le[2], le[3]))
    ex = [jnp.exp(x - me) for x in le]

    def first_argmax(vals):
        mx = jnp.maximum(jnp.maximum(vals[0], vals[1]), jnp.maximum(vals[2], vals[3]))
        ix = jnp.where(vals[0] == mx, 0, jnp.where(vals[1] == mx, 1, jnp.where(vals[2] == mx, 2, 3)))
        return mx, ix.astype(jnp.int32)

    e1, i1 = first_argmax(ex)
    ex2 = [jnp.where(i1 == e, -1.0, ex[e]) for e in range(EXP_PER_GROUP)]
    e2, i2 = first_argmax(ex2)
    tot = e1 + e2
    w1 = e1 / tot * p_group
    w2 = e2 / tot * p_group
    first_low = i1 < i2
    a = jnp.where(first_low, i1, i2)
    b = jnp.where(first_low, i2, i1)
    pair = jnp.where(a == 0, b - 1, jnp.where(a == 1, b + 1, N_PAIRS - 1))
    bucket = gi * N_PAIRS + pair
    return bucket, jnp.where(first_low, w1, w2), jnp.where(first_low, w2, w1)


def _mix_out_kernel(a_ref, w_ref, x_ref, g1_ref, lng_ref, lnb_ref, sc2_ref, sh2_ref,
                    wr_ref, br_ref, x1_ref, gt_ref, gi_ref):
    y = jnp.dot(a_ref[...], w_ref[...], preferred_element_type=F32)
    x1 = _layer_norm(DN_ALPHA * x_ref[...] + g1_ref[0] * y, lng_ref[...], lnb_ref[...])
    x1_ref[...] = x1
    h2 = x1 * (1.0 + sc2_ref[0]) + sh2_ref[0]
    hi = h2.astype(BF16)
    lo = (h2 - hi.astype(F32)).astype(BF16)
    hw = jnp.dot(hi, wr_ref[...], preferred_element_type=F32)
    logits = (hw[:, :ROUTE_W] + (hw[:, ROUTE_W:] + jnp.dot(lo, wr_ref[:, :ROUTE_W], preferred_element_type=F32))
              + br_ref[...])
    lt = logits.T
    bucket, gate_a, gate_b = _route_rows(lt)
    gi_ref[0] = bucket
    tm = lt.shape[1]
    row = lax.broadcasted_iota(jnp.int32, (8, tm), 0)
    g8 = jnp.where(row == 0, gate_a, jnp.where(row == 1, gate_b, 0.0))
    gt = jnp.concatenate([g8, jnp.zeros((ROUTE_W - 8, tm), F32)], axis=0)
    gt_ref[...] = gt.T


def _mix_out(tl, a, w, x, g1, lng, lnb, sc2, sh2, wr, br, name):
    kd = a.shape[1]
    return pl.pallas_call(
        _mix_out_kernel,
        grid=(tl.n,),
        in_specs=[tl.rows(kd), _const_spec(w), tl.rows(D_MODEL), tl.mod_spec(g1),
                  _const_spec(lng), _const_spec(lnb), tl.mod_spec(sc2), tl.mod_spec(sh2),
                  _const_spec(wr), _const_spec(br)],
        out_specs=[tl.rows(D_MODEL), tl.rows(ROUTE_W),
                   pl.BlockSpec((1, 1, tl.tm), lambda i: (i, 0, 0))],
        out_shape=[jax.ShapeDtypeStruct((tl.t, D_MODEL), F32),
                   jax.ShapeDtypeStruct((tl.t, ROUTE_W), F32),
                   jax.ShapeDtypeStruct((tl.n, 1, tl.tm), jnp.int32)],
        compiler_params=_cparams(("parallel",)),
        name=name,
    )(a, w, x, g1, lng, lnb, sc2, sh2, wr, br)


TOK_SUB = D_MODEL // LANES
DMA_PRIORITIES = 2


def _moe_sort_kernel(idx_ref, x1_ref, sc2_ref, sh2_ref, gt_ref, tok_ref, gs_ref, buf, gs_sc, sem, gsem, *, tm):
    i = pl.program_id(0)
    n = pl.num_programs(0)
    slot = i % 2

    def wait_slot(sl):
        pltpu.make_async_copy(buf.at[sl], tok_ref.at[pl.ds(0, tm * TOK_SUB)], sem.at[sl]).wait()

    @pl.when(i >= 2)
    def _():
        wait_slot(slot)

    h2 = x1_ref[...] * (1.0 + sc2_ref[0]) + sh2_ref[0]
    for s in range(TOK_SUB):
        buf[slot, pl.ds(s, tm, stride=TOK_SUB), :] = h2[:, s * LANES:(s + 1) * LANES]

    base = i * tm

    def issue(rr, carry):
        for prio in range(DMA_PRIORITIES):
            r = rr * DMA_PRIORITIES + prio
            d = idx_ref[base + r]
            pltpu.make_async_copy(buf.at[slot, pl.ds(pl.multiple_of(r * TOK_SUB, TOK_SUB), TOK_SUB)],
                                  tok_ref.at[pl.ds(pl.multiple_of(d * TOK_SUB, TOK_SUB), TOK_SUB)],
                                  sem.at[slot]).start(priority=prio)
            gs_sc[pl.ds(d, 1), :] = gt_ref[pl.ds(r, 1), :]
        return carry

    lax.fori_loop(0, tm // DMA_PRIORITIES, issue, 0, unroll=4)

    @pl.when(i == n - 1)
    def _():
        gates_out = pltpu.make_async_copy(gs_sc, gs_ref, gsem)
        gates_out.start()

        @pl.when(n >= 2)
        def _():
            wait_slot(1 - slot)
        wait_slot(slot)
        gates_out.wait()


def _moe_sort(tl, x1, sc2, sh2, gt, dst_idx, n_rows):
    tm = tl.tm
    n_steps = dst_idx.shape[0] // tm
    last = tl.n - 1
    tps = tl.tiles_per_seq
    kern = functools.partial(_moe_sort_kernel, tm=tm)
    rows = lambda w: pl.BlockSpec((tm, w), lambda i, idx: (jnp.minimum(i, last), 0))
    mod = lambda arr: pl.BlockSpec((1,) + arr.shape[1:], lambda i, idx: (jnp.minimum(i, last) // tps, 0, 0))
    return pl.pallas_call(
        kern,
        grid_spec=pltpu.PrefetchScalarGridSpec(
            num_scalar_prefetch=1,
            grid=(n_steps,),
            in_specs=[rows(D_MODEL), mod(sc2), mod(sh2), rows(ROUTE_W)],
            out_specs=[pl.BlockSpec(memory_space=pl.ANY), pl.BlockSpec(memory_space=pl.ANY)],
            scratch_shapes=[pltpu.VMEM((2, tm * TOK_SUB, LANES), F32), pltpu.VMEM((n_rows, ROUTE_W), F32),
                            pltpu.SemaphoreType.DMA((2,)), pltpu.SemaphoreType.DMA(())]),
        out_shape=[jax.ShapeDtypeStruct((n_rows * TOK_SUB, LANES), F32),
                   jax.ShapeDtypeStruct((n_rows, ROUTE_W), F32)],
        compiler_params=_cparams(("arbitrary",)),
        name="moe_sort_rows",
    )(dst_idx, x1, sc2, sh2, gt)


def _moe_kernel(ea_ref, eb_ref, nv_ref, tok_ref, gs_ref, wgu_a_ref, wgu_b_ref, wdn_a_ref, wdn_b_ref, o_ref,
                *, tme):
    i = pl.program_id(0)

    @pl.when(i < nv_ref[0])
    def _():
        hb = jnp.concatenate([tok_ref[pl.ds(s, tme, stride=TOK_SUB), :] for s in range(TOK_SUB)],
                             axis=1).astype(BF16)
        acc = None
        for lane, (wgu_ref, wdn_ref) in enumerate(((wgu_a_ref, wdn_a_ref), (wgu_b_ref, wdn_b_ref))):
            au = jnp.dot(hb, wgu_ref[0], preferred_element_type=F32)
            act = (_silu(au[:, :EXPERT_FF]) * au[:, EXPERT_FF:]).astype(BF16)
            y = jnp.dot(act, wdn_ref[0], preferred_element_type=F32)
            gy = gs_ref[:, lane:lane + 1] * y
            acc = gy if acc is None else acc + gy
        for s in range(TOK_SUB):
            o_ref[pl.ds(s, tme, stride=TOK_SUB), :] = acc[:, s * LANES:(s + 1) * LANES]

    @pl.when(i >= nv_ref[0])
    def _():
        o_ref[...] = jnp.zeros_like(o_ref)


def _moe_experts(tok, gs, tile_ea, tile_eb, n_valid, wgu, wdn, tme):
    n_tiles = gs.shape[0] // tme
    kern = functools.partial(_moe_kernel, tme=tme)
    gu = lambda pick: pl.BlockSpec((1, D_MODEL, 2 * EXPERT_FF), lambda i, ea, eb, nv: (pick(ea, eb)[i], 0, 0))
    dn = lambda pick: pl.BlockSpec((1, EXPERT_FF, D_MODEL), lambda i, ea, eb, nv: (pick(ea, eb)[i], 0, 0))
    first = lambda ea, eb: ea
    second = lambda ea, eb: eb
    return pl.pallas_call(
        kern,
        grid_spec=pltpu.PrefetchScalarGridSpec(
            num_scalar_prefetch=3,
            grid=(n_tiles,),
            in_specs=[pl.BlockSpec((tme * TOK_SUB, LANES), lambda i, ea, eb, nv: (i, 0)),
                      pl.BlockSpec((tme, ROUTE_W), lambda i, ea, eb, nv: (i, 0)),
                      gu(first), gu(second), dn(first), dn(second)],
            out_specs=pl.BlockSpec((tme * TOK_SUB, LANES), lambda i, ea, eb, nv: (i, 0))),
        out_shape=jax.ShapeDtypeStruct(tok.shape, F32),
        compiler_params=_cparams(("arbitrary",)),
        name="moe_experts",
    )(tile_ea, tile_eb, n_valid, tok, gs, wgu, wgu, wdn, wdn)


def _ln_res_kernel(idx_ref, fs_ref, x_ref, g_ref, lng_ref, lnb_ref, o_ref, buf0, buf1, buf2, sem, *, tm):
    i = pl.program_id(0)
    n = pl.num_programs(0)
    bufs = (buf0, buf1, buf2)
    n_buf = len(bufs)

    def row_copy(d, sl, row_off):
        return pltpu.make_async_copy(fs_ref.at[pl.ds(pl.multiple_of(d * TOK_SUB, TOK_SUB), TOK_SUB)],
                                     bufs[sl].at[pl.ds(row_off, TOK_SUB)], sem.at[sl])

    def wait_rows(sl):
        pltpu.make_async_copy(fs_ref.at[pl.ds(0, tm * TOK_SUB)], bufs[sl], sem.at[sl]).wait()

    def step_base(k):
        return jnp.minimum(k, n - 1) * tm

    @pl.when(i == 0)
    def _():
        for sl in range(n_buf - 1):
            base = step_base(sl)

            def body(rr, carry):
                for prio in range(DMA_PRIORITIES):
                    r = rr * DMA_PRIORITIES + prio
                    row_copy(idx_ref[base + r], sl, pl.multiple_of(r * TOK_SUB, TOK_SUB)).start(priority=prio)
                return carry

            lax.fori_loop(0, tm // DMA_PRIORITIES, body, 0, unroll=4)

    ahead = step_base(i + n_buf - 1)

    def step(sl):
        wait_rows(sl)
        f = jnp.concatenate([bufs[sl][pl.ds(s, tm, stride=TOK_SUB), :] for s in range(TOK_SUB)], axis=1)
        o_ref[...] = _layer_norm(DN_ALPHA * x_ref[...] + g_ref[0] * f, lng_ref[...], lnb_ref[...])
        for r in range(tm):
            row_copy(idx_ref[ahead + r], (sl + n_buf - 1) % n_buf, r * TOK_SUB).start(
                priority=r % DMA_PRIORITIES)

    for sl in range(n_buf):
        @pl.when(i % n_buf == sl)
        def _():
            step(sl)

            @pl.when(i == n - 1)
            def _():
                for extra in range(1, n_buf):
                    wait_rows((sl + extra) % n_buf)


def _ln_res(tl, x, fs, dest, g2, lng, lnb):
    tm = tl.tm
    tps = tl.tiles_per_seq
    kern = functools.partial(_ln_res_kernel, tm=tm)
    const = lambda arr: pl.BlockSpec(arr.shape, lambda i, idx: (0,) * arr.ndim)
    return pl.pallas_call(
        kern,
        grid_spec=pltpu.PrefetchScalarGridSpec(
            num_scalar_prefetch=1,
            grid=(tl.n,),
            in_specs=[pl.BlockSpec(memory_space=pl.ANY),
                      pl.BlockSpec((tm, D_MODEL), lambda i, idx: (i, 0)),
                      pl.BlockSpec((1,) + g2.shape[1:], lambda i, idx: (i // tps, 0, 0)),
                      const(lng), const(lnb)],
            out_specs=pl.BlockSpec((tm, D_MODEL), lambda i, idx: (i, 0)),
            scratch_shapes=[pltpu.VMEM((tm * TOK_SUB, LANES), F32)] * 3 + [pltpu.SemaphoreType.DMA((3,))]),
        out_shape=jax.ShapeDtypeStruct((tl.t, D_MODEL), F32),
        compiler_params=_cparams(("arbitrary",)),
        name="ffn_residual_ln",
    )(dest, fs, x, g2, lng, lnb)


def _moe_ffn(tl, x1, sc2, sh2, gt, gi, g2, lng, lnb, wgu, wdn, layer):
    t = tl.t
    tme = EXPERT_TILE if t % TOKEN_TILE == 0 else 64
    n_tiles = t // tme + N_BUCKETS
    n_pad = N_BUCKETS * tme
    assert n_pad % tl.tm == 0
    i32 = jnp.int32
    onehot = (gi[:, None] == jnp.arange(N_BUCKETS, dtype=i32)[None, :]).astype(i32)
    counts = jnp.sum(onehot, axis=0)
    padded = ((counts + tme - 1) // tme) * tme
    ends = jnp.cumsum(padded)
    starts = ends - padded
    dest = jnp.sum(onehot * (jnp.cumsum(onehot, axis=0) - onehot + starts[None, :]), axis=1).astype(i32)
    cpad = jnp.cumsum(padded - counts)
    k = jnp.arange(n_pad, dtype=i32)
    seg_base = jnp.concatenate([starts + counts, ends[-1:]])
    seg_first = jnp.concatenate([jnp.zeros((1,), cpad.dtype), cpad])
    seg_hot = ((k[:, None] >= seg_first[None, :])
               & (k[:, None] < jnp.concatenate([cpad, jnp.full((1,), n_pad, cpad.dtype)])[None, :])).astype(i32)
    pad_rows = (k + jnp.sum(seg_hot * (seg_base - seg_first)[None, :], axis=1)).astype(i32)
    tile_start = jnp.arange(n_tiles, dtype=i32) * tme
    tile_hot = ((tile_start[:, None] >= starts[None, :]) & (tile_start[:, None] < ends[None, :])).astype(i32)
    bucket_ids = jnp.arange(N_BUCKETS, dtype=i32)
    first_expert = layer * N_EXPERTS + (bucket_ids // N_PAIRS) * EXP_PER_GROUP
    pair_a = jnp.asarray(PAIR_A * N_GROUPS, i32)
    pair_b = jnp.asarray(PAIR_B * N_GROUPS, i32)
    last_expert = layer * N_EXPERTS + N_EXPERTS - 1
    in_use = jnp.sum(tile_hot, axis=1)
    tile_ea = (jnp.sum(tile_hot * (first_expert + pair_a)[None, :], axis=1) + (1 - in_use) * last_expert).astype(i32)
    tile_eb = (jnp.sum(tile_hot * (first_expert + pair_b)[None, :], axis=1) + (1 - in_use) * last_expert).astype(i32)
    n_valid = (ends[-1] // tme).astype(i32).reshape(1)
    tok, gs = _moe_sort(tl, x1, sc2, sh2, gt, jnp.concatenate([dest, pad_rows]), n_tiles * tme)
    fs = _moe_experts(tok, gs, tile_ea, tile_eb, n_valid, wgu, wdn, tme)
    return _ln_res(tl, x1, fs, dest, g2, lng, lnb)


def _mla_in_kernel(x_ref, sc_ref, sh_ref, wdkv_ref, kvg_ref, wukv_ref, wdq_ref, qg_ref, wuq_ref, cos_ref, sin_ref,
                   lat_ref, kr_ref, q_ref, *kv_refs, qscale, expand):
    x = x_ref[...]
    cos = cos_ref[...]
    sin = sin_ref[...]
    kv = jnp.dot(x.astype(BF16), wdkv_ref[...], preferred_element_type=F32)
    c = kv[:, :KV_LORA]
    lat = c * lax.rsqrt(jnp.mean(c * c, axis=-1, keepdims=True) + RMS_EPS) * kvg_ref[...]
    lat_ref[...] = lat
    kr = kv[:, KV_LORA:KV_LORA + LANES] * cos + kv[:, KV_LORA + LANES:] * sin
    kr_ref[...] = kr[:, :MLA_ROPE]
    if expand:
        k_ref, v_ref = kv_refs
        kvx = jnp.dot(lat.astype(BF16), wukv_ref[...], preferred_element_type=F32)
        krb = kr.astype(BF16)
        ones_col = jnp.where(lax.broadcasted_iota(jnp.int32, krb.shape, 1) == 0, 1.0, 0.0).astype(BF16)
        v0 = MLA_HEADS * MLA_NOPE
        for hh in range(MLA_HEADS):
            k_ref[:, hh * HEAD_W:hh * HEAD_W + MLA_NOPE] = kvx[:, hh * MLA_NOPE:(hh + 1) * MLA_NOPE].astype(BF16)
            k_ref[:, hh * HEAD_W + MLA_NOPE:(hh + 1) * HEAD_W] = krb
            v_ref[:, hh * HEAD_W:hh * HEAD_W + MLA_V] = kvx[:, v0 + hh * MLA_V:v0 + (hh + 1) * MLA_V].astype(BF16)
            v_ref[:, hh * HEAD_W + MLA_V:(hh + 1) * HEAD_W] = ones_col
    else:
        (krp_ref,) = kv_refs
        krp_ref[...] = kr
    h = (x * (1.0 + sc_ref[0]) + sh_ref[0]).astype(BF16)
    cq = jnp.dot(h, wdq_ref[...], preferred_element_type=F32)
    qn = (cq * lax.rsqrt(jnp.mean(cq * cq, axis=-1, keepdims=True) + RMS_EPS) * qg_ref[...]).astype(BF16)
    nw = MLA_HEADS * LANES
    qnope = jnp.dot(qn, wuq_ref[:, :nw], preferred_element_type=F32)
    qpe = jnp.dot(qn, wuq_ref[:, nw:2 * nw], preferred_element_type=F32)
    qpe_sw = jnp.dot(qn, wuq_ref[:, 2 * nw:], preferred_element_type=F32)
    for hh in range(MLA_HEADS):
        cols = slice(hh * LANES, (hh + 1) * LANES)
        q_ref[:, hh * HEAD_W:hh * HEAD_W + LANES] = (qnope[:, cols] * qscale).astype(BF16)
        pe = qpe[:, cols] * cos + qpe_sw[:, cols] * sin
        q_ref[:, hh * HEAD_W + LANES:(hh + 1) * HEAD_W] = (pe * qscale).astype(BF16)


def _mla_in(tl, x, sc, sh, p, j, cos, sin, qscale, expand):
    kern = functools.partial(_mla_in_kernel, qscale=qscale, expand=expand)
    weights = [p["w_dkv_ext"], p["kv_norm_g"], p["w_ukv_r"], p["w_dq"][j], p["q_norm_g"][j], p["w_uq_ext"][j]]
    out_w = [(KV_LORA, F32), (MLA_ROPE, F32), (MLA_HEADS * HEAD_W, BF16)]
    out_w += [(MLA_HEADS * HEAD_W, BF16)] * 2 if expand else [(LANES, F32)]
    return pl.pallas_call(
        kern,
        grid=(tl.n,),
        in_specs=[tl.rows(D_MODEL), tl.mod_spec(sc), tl.mod_spec(sh)] + [_const_spec(w) for w in weights]
                 + [tl.table_spec(cos), tl.table_spec(sin)],
        out_specs=[tl.rows(w) for w, _ in out_w],
        out_shape=[jax.ShapeDtypeStruct((tl.t, w), dt) for w, dt in out_w],
        compiler_params=_cparams(("parallel",)),
        name="mla_in_proj",
    )(x, sc, sh, *weights, cos, sin)


def _attn_kernel(q_ref, qn_ref, k_ref, v_ref, o_ref, m_sc, acc_sc, sa_sc, sb_sc, sc_sc, *, t, n_sub, hp):
    i = pl.program_id(2)
    m_sc[...] = jnp.full(m_sc.shape, NEG_INF, F32)
    acc_sc[...] = jnp.zeros(acc_sc.shape, F32)
    rs = t // n_sub

    def scores(j, s_ref, qr=q_ref):
        for h in range(hp):
            hc = slice(h * HEAD_W, (h + 1) * HEAD_W)
            k = k_ref[0, pl.ds(pl.multiple_of(j * t, t), t), hc]
            s_ref[h] = lax.dot_general(qr[0, :, hc], k, (((1,), (1,)), ((), ())), preferred_element_type=F32)

    def fold(j, s_ref, diagonal):
        for h in range(hp):
            v = v_ref[0, pl.ds(pl.multiple_of(j * t, t), t), h * HEAD_W:(h + 1) * HEAD_W]
            for u in range(n_sub):
                rows = slice(u * rs, (u + 1) * rs)
                s = s_ref[h, rows, :]
                if diagonal:
                    qpos = u * rs + lax.broadcasted_iota(jnp.int32, (rs, 1), 0)
                    kpos = lax.broadcasted_iota(jnp.int32, (1, t), 1)
                    s = jnp.where((kpos >> LOG2_CHUNK) <= (qpos >> LOG2_CHUNK), s, NEG_INF)
                m_old = m_sc[h, rows, :]
                m_new = jnp.maximum(m_old, jnp.max(s, axis=1, keepdims=True))
                alpha = jnp.exp2(m_old - m_new)
                p = jnp.exp2(s - jnp.tile(m_new, (1, t // LANES)))
                pv = jnp.dot(p.astype(BF16), v, preferred_element_type=F32)
                acc_sc[h, rows, :] = jnp.tile(alpha, (1, HEAD_W // LANES)) * acc_sc[h, rows, :] + pv
                m_sc[h, rows, :] = m_new

    def next_tile_scores():
        scores(0, sc_sc, qn_ref)

    @pl.when(i == 0)
    def _():
        scores(0, sc_sc)
        fold(0, sc_sc, True)
        next_tile_scores()

    @pl.when(i == 1)
    def _():
        scores(1, sb_sc)
        fold(0, sc_sc, False)
        next_tile_scores()
        fold(1, sb_sc, True)

    @pl.when(i >= 2)
    def _():
        scores(1, sb_sc)
        fold(0, sc_sc, False)
        scores(2, sa_sc)
        fold(1, sb_sc, False)

        def pair_step(jj, carry):
            j = 2 * jj
            scores(j + 1, sb_sc)
            fold(j, sa_sc, False)
            scores(j + 2, sa_sc)
            fold(j + 1, sb_sc, False)
            return carry

        lax.fori_loop(1, i // 2, pair_step, 0)

        @pl.when(i % 2 == 0)
        def _():
            next_tile_scores()
            fold(i, sa_sc, True)

        @pl.when(i % 2 == 1)
        def _():
            scores(i, sb_sc)
            fold(i - 1, sa_sc, False)
            next_tile_scores()
            fold(i, sb_sc, True)

    for h in range(hp):
        acc = acc_sc[h]
        o_ref[0, :, h * MLA_V:(h + 1) * MLA_V] = (acc[:, :MLA_V] / acc[:, MLA_V:MLA_V + 1]).astype(BF16)


def _attention(qc, kc, vv, b, l, t, hp):
    assert l % t == 0 and t % CHUNK == 0
    n_sub = 2 if t % (2 * LANES) == 0 else 1
    nq = l // t
    kern = functools.partial(_attn_kernel, t=t, n_sub=n_sub, hp=hp)
    scores_buf = pltpu.VMEM((hp, t, t), F32)
    return pl.pallas_call(
        kern,
        grid=(b, MLA_HEADS // hp, nq),
        in_specs=[pl.BlockSpec((1, t, hp * HEAD_W), lambda bi, h, i: (bi, i, h)),
                  pl.BlockSpec((1, t, hp * HEAD_W), lambda bi, h, i: (bi, jnp.minimum(i + 1, nq - 1), h)),
                  pl.BlockSpec((1, l, hp * HEAD_W), lambda bi, h, i: (bi, 0, h)),
                  pl.BlockSpec((1, l, hp * HEAD_W), lambda bi, h, i: (bi, 0, h))],
        out_specs=pl.BlockSpec((1, t, hp * MLA_V), lambda bi, h, i: (bi, i, h)),
        out_shape=jax.ShapeDtypeStruct((b, l, MLA_HEADS * MLA_V), BF16),
        scratch_shapes=[pltpu.VMEM((hp, t, LANES), F32), pltpu.VMEM((hp, t, HEAD_W), F32),
                        scores_buf, scores_buf, scores_buf],
        compiler_params=_cparams(("parallel", "parallel", "arbitrary"), ATTN_VMEM_LIMIT),
        name="mla_attention",
    )(qc.reshape(b, l, MLA_HEADS * HEAD_W), qc.reshape(b, l, MLA_HEADS * HEAD_W),
      kc.reshape(b, l, MLA_HEADS * HEAD_W), vv.reshape(b, l, MLA_HEADS * HEAD_W))


def _decode_attn_kernel(q_ref, lnew_ref, knew_ref, lpast_ref, kpast_ref, wukt_ref, wuv_ref, o_ref, qa_sc,
                        *, l, pos0, past):
    nope_w = KV_LORA
    for h in range(MLA_HEADS):
        rows = slice(h * l, (h + 1) * l)
        qn = q_ref[0, :, h * HEAD_W:h * HEAD_W + MLA_NOPE]
        qa_sc[rows, :nope_w] = jnp.dot(qn, wukt_ref[h], preferred_element_type=F32).astype(BF16)
        qa_sc[rows, nope_w:] = q_ref[0, :, h * HEAD_W + MLA_NOPE:(h + 1) * HEAD_W]
    q_lat = qa_sc[:, :nope_w]
    q_pe = qa_sc[:, nope_w:nope_w + MLA_ROPE]
    nt = (((1,), (1,)), ((), ()))
    lp = lpast_ref[0].astype(BF16)
    kp = kpast_ref[0].astype(BF16)
    ln = lnew_ref[0].astype(BF16)
    kn = knew_ref[0][:, :MLA_ROPE].astype(BF16)
    s_past = (lax.dot_general(q_lat, lp, nt, preferred_element_type=F32)
              + lax.dot_general(q_pe, kp, nt, preferred_element_type=F32))
    s_new = (lax.dot_general(q_lat, ln, nt, preferred_element_type=F32)
             + lax.dot_general(q_pe, kn, nt, preferred_element_type=F32))
    n_rows = MLA_HEADS * l
    qpos = pos0 + lax.rem(lax.broadcasted_iota(jnp.int32, (n_rows, 1), 0), l)
    kpos_new = pos0 + lax.broadcasted_iota(jnp.int32, (1, l), 1)
    s_new = jnp.where((kpos_new >> LOG2_CHUNK) <= (qpos >> LOG2_CHUNK), s_new, NEG_INF)
    kpos_past = lax.broadcasted_iota(jnp.int32, (1, past), 1)
    s_past = jnp.where((kpos_past >> LOG2_CHUNK) <= (qpos >> LOG2_CHUNK), s_past, NEG_INF)
    m = jnp.maximum(jnp.max(s_past, axis=1, keepdims=True), jnp.max(s_new, axis=1, keepdims=True))
    p_past = jnp.exp2(s_past - m)
    p_new = jnp.exp2(s_new - m)
    denom = jnp.sum(p_past, axis=1, keepdims=True) + jnp.sum(p_new, axis=1, keepdims=True)
    ctx = (jnp.dot(p_past.astype(BF16), lp, preferred_element_type=F32)
           + jnp.dot(p_new.astype(BF16), ln, preferred_element_type=F32)) / denom
    ctx = ctx.astype(BF16)
    for h in range(MLA_HEADS):
        o_ref[0, :, h * MLA_V:(h + 1) * MLA_V] = jnp.dot(
            ctx[h * l:(h + 1) * l], wuv_ref[h], preferred_element_type=F32).astype(BF16)


def _decode_attention(qc, latent_new, krp_new, past_latent, past_k_rope, w_uk_t, w_uv, b, l, pos0):
    past = past_latent.shape[1]
    kern = functools.partial(_decode_attn_kernel, l=l, pos0=pos0, past=past)
    per_b = lambda shape: pl.BlockSpec((1,) + shape, lambda bi: (bi, 0, 0))
    return pl.pallas_call(
        kern,
        grid=(b,),
        in_specs=[per_b((l, MLA_HEADS * HEAD_W)), per_b((l, KV_LORA)), per_b((l, LANES)),
                  per_b((past, KV_LORA)), per_b((past, MLA_ROPE)), _const_spec(w_uk_t), _const_spec(w_uv)],
        out_specs=per_b((l, MLA_HEADS * MLA_V)),
        out_shape=jax.ShapeDtypeStruct((b, l, MLA_HEADS * MLA_V), BF16),
        scratch_shapes=[pltpu.VMEM((MLA_HEADS * l, KV_LORA + LANES), BF16)],
        compiler_params=_cparams(("parallel",)),
        name="mla_decode_attention",
    )(qc.reshape(b, l, MLA_HEADS * HEAD_W), latent_new.reshape(b, l, KV_LORA), krp_new.reshape(b, l, LANES),
      past_latent, past_k_rope, w_uk_t, w_uv)


def _rope_tables(pos0, length, half):
    inv = ROPE_THETA ** (-np.arange(half, dtype=np.float64) / half)
    ang = (pos0 + np.arange(length, dtype=np.float64))[:, None] * inv[None, :]
    return np.cos(ang), np.sin(ang)


def _mla_rope_tables(pos0, length):
    cos, sin = _rope_tables(pos0, length, MLA_ROPE // 2)
    z = np.zeros((length, LANES - MLA_ROPE))
    return (jnp.asarray(np.concatenate([cos, cos, z], axis=1), F32),
            jnp.asarray(np.concatenate([-sin, sin, z], axis=1), F32))


def _swap_halves(w):
    half = w.shape[-1] // 2
    return jnp.concatenate([w[..., half:], w[..., :half]], axis=-1)


def _pad_lanes(w):
    return jnp.pad(w, [(0, 0)] * (w.ndim - 1) + [(0, LANES - w.shape[-1])])


def _prep_weights(w_ret_in, w_ret_out, w_dq, q_norm_g, w_uq, w_mla_out, w_dkv, kv_norm_g, w_ukv,
                  w_route_group, b_route_group, w_route_expert, b_route_expert,
                  w_expert_gate_up, w_expert_down):
    p = {}
    p["w_ret_in"] = [w_ret_in[i].astype(BF16) for i in range(N_A)]
    p["w_ret_out"] = [w_ret_out[i].astype(BF16) for i in range(N_A)]
    p["w_dq"] = [w_dq[j].astype(BF16) for j in range(DEPTH - N_A)]
    p["q_norm_g"] = [q_norm_g[j][None, :] for j in range(DEPTH - N_A)]
    w_uq_ext = []
    for j in range(DEPTH - N_A):
        wq = w_uq[j].reshape(Q_LORA, MLA_HEADS, MLA_NOPE + MLA_ROPE)
        nope = wq[:, :, :MLA_NOPE].reshape(Q_LORA, MLA_HEADS * LANES)
        pe = wq[:, :, MLA_NOPE:]
        w_uq_ext.append(jnp.concatenate(
            [nope, _pad_lanes(pe).reshape(Q_LORA, MLA_HEADS * LANES),
             _pad_lanes(_swap_halves(pe)).reshape(Q_LORA, MLA_HEADS * LANES)], axis=1).astype(BF16))
    p["w_uq_ext"] = w_uq_ext
    p["w_mla_out"] = [w_mla_out[j].astype(BF16) for j in range(DEPTH - N_A)]
    kr = w_dkv[:, KV_LORA:]
    p["w_dkv_ext"] = jnp.concatenate([w_dkv[:, :KV_LORA], _pad_lanes(kr), _pad_lanes(_swap_halves(kr))],
                                     axis=1).astype(BF16)
    p["kv_norm_g"] = kv_norm_g[None, :]
    w_ukv4 = w_ukv.reshape(KV_LORA, MLA_HEADS, 2, MLA_NOPE)
    p["w_ukv_r"] = w_ukv4.transpose(0, 2, 1, 3).reshape(KV_LORA, 2 * MLA_HEADS * MLA_NOPE).astype(BF16)
    p["w_uk_t"] = w_ukv4[:, :, 0, :].transpose(1, 2, 0).astype(BF16)
    p["w_uv"] = w_ukv4[:, :, 1, :].transpose(1, 0, 2).astype(BF16)
    wr = jnp.concatenate([w_route_group, w_route_expert], axis=-1)
    wr = jnp.pad(wr, ((0, 0), (0, 0), (0, ROUTE_W - wr.shape[-1])))
    wr_hi = wr.astype(BF16)
    wr_lo = (wr - wr_hi.astype(F32)).astype(BF16)
    br = jnp.concatenate([b_route_group, b_route_expert], axis=-1)
    br = jnp.pad(br, ((0, 0), (0, ROUTE_W - br.shape[-1])))
    wr_cat = jnp.concatenate([wr_hi, wr_lo], axis=-1)
    p["wr"] = [wr_cat[l] for l in range(DEPTH)]
    p["br"] = [br[l][None, :] for l in range(DEPTH)]
    p["w_gu"] = w_expert_gate_up.astype(BF16).reshape(DEPTH * N_EXPERTS, D_MODEL, 2 * EXPERT_FF)
    p["w_dn"] = w_expert_down.astype(BF16).reshape(DEPTH * N_EXPERTS, EXPERT_FF, D_MODEL)
    return p


def _trunk(x3, mod, pos0, ret_s0, past_latent, past_k_rope, ret_chunk, ret_block, ln_g, ln_b, p):
    b, l, _ = x3.shape
    tl = _Tiles(b, l)
    x = x3.reshape(tl.t, D_MODEL)
    new_ret = []
    latent_new = k_rope_new = None
    for layer in range(DEPTH):
        sh1, sc1, g1, sh2, sc2, g2 = [tl.mod(m) for m in jnp.split(mod[layer], 6, axis=-1)]
        lng = [ln_g[layer, s][None, :] for s in range(2)]
        lnb = [ln_b[layer, s][None, :] for s in range(2)]
        if layer < N_A:
            cos, sin = [jnp.asarray(tab, F32) for tab in _rope_tables(pos0, l, RET_DK // 2)]
            q, k, v, sg = _ret_in(tl, x, sc1, sh1, p["w_ret_in"][layer], tl.table(cos), tl.table(sin))
            a, s_new = _retention(b, l, q, k, v, sg, ret_s0[layer], ret_chunk, ret_block)
            a = a.reshape(tl.t, RET_HEADS * RET_DV)
            new_ret.append(s_new)
            w_out = p["w_ret_out"][layer]
            name = "ret_out_ln_route"
        else:
            j = layer - N_A
            cos, sin = _mla_rope_tables(pos0, l)
            cos, sin = tl.table(cos), tl.table(sin)
            assert layer == N_A == DEPTH - 1
            qscale = float((MLA_NOPE + MLA_ROPE) ** -0.5 * math.log2(math.e))
            if past_latent is None:
                assert pos0 == 0
                latent_new, k_rope_new, qc, kc, vv = _mla_in(tl, x, sc1, sh1, p, j, cos, sin, qscale, True)
                a = _attention(qc, kc, vv, b, l, ATTN_TILE, 2)
            else:
                latent_new, k_rope_new, qc, krp = _mla_in(tl, x, sc1, sh1, p, j, cos, sin, qscale, False)
                a = _decode_attention(qc, latent_new, krp, past_latent, past_k_rope, p["w_uk_t"], p["w_uv"],
                                      b, l, pos0)
            a = a.reshape(tl.t, MLA_HEADS * MLA_V)
            w_out = p["w_mla_out"][j]
            name = "mla_out_ln_route"
        x1, gt, gi = _mix_out(tl, a, w_out, x, g1, lng[0], lnb[0], sc2, sh2,
                              p["wr"][layer], p["br"][layer], name)
        x = _moe_ffn(tl, x1, sc2, sh2, gt, gi.reshape(tl.t), g2, lng[1], lnb[1],
                     p["w_gu"], p["w_dn"], layer)
    return (x.reshape(b, l, D_MODEL), jnp.stack(new_ret), latent_new.reshape(b, l, KV_LORA),
            k_rope_new.reshape(b, l, MLA_ROPE))


def kernel(x_prompt, x_sample, state_retention, cache_kv_latent, cache_k_rope, c_prompt, c_sample,
           w_ada, b_ada, ln_g, ln_b, w_ret_in, w_ret_out, w_dq, q_norm_g, w_uq, w_mla_out,
           w_dkv, kv_norm_g, w_ukv, w_route_group, b_route_group, w_route_expert, b_route_expert,
           w_expert_gate_up, w_expert_down):
    bp, lp, _ = x_prompt.shape
    bs, ls, _ = x_sample.shape
    p = _prep_weights(w_ret_in, w_ret_out, w_dq, q_norm_g, w_uq, w_mla_out, w_dkv, kv_norm_g, w_ukv,
                      w_route_group, b_route_group, w_route_expert, b_route_expert,
                      w_expert_gate_up, w_expert_down)
    n_seq = bp + bs
    n_rows = -(-n_seq // 8) * 8
    c_all = jnp.concatenate([c_prompt, c_sample, jnp.zeros((n_rows - n_seq, D_MODEL), F32)], axis=0)
    mod = _ada(c_all, w_ada, b_ada)
    s0_p = jnp.zeros((N_A, bp, RET_HEADS, RET_DK, RET_DV), F32)
    ret_chunk_p = RET_CHUNK if lp % RET_CHUNK == 0 else CHUNK
    y_p, st_p, lat_p, kr_p = _trunk(x_prompt, mod[:, :bp], 0, s0_p, None, None,
                                    ret_chunk_p, max(ret_chunk_p, min(lp, TOKEN_TILE)), ln_g, ln_b, p)
    past = cache_kv_latent.shape[1]
    y_s, st_s, lat_s, kr_s = _trunk(x_sample, mod[:, bp:n_seq], past, state_retention,
                                    cache_kv_latent, cache_k_rope, ls, ls, ln_g, ln_b, p)
    return (y_p, y_s, st_p, st_s, lat_p, kr_p, lat_s, kr_s)
```

```python
import functools
import math

import numpy as np
import jax
import jax.numpy as jnp
from jax import lax
from jax.experimental import pallas as pl
from jax.experimental.pallas import tpu as pltpu

F32 = jnp.float32
BF16 = jnp.bfloat16

D_MODEL = 1024
DEPTH = 2
CHUNK = 64
LOG2_CHUNK = 6
N_A = DEPTH // 2
RET_HEADS = 4
RET_DK = 256
RET_DV = 512
MLA_HEADS = 8
MLA_NOPE = 128
MLA_ROPE = 64
MLA_V = 128
Q_LORA = 384
KV_LORA = 256
ROPE_THETA = 10000.0
N_GROUPS = 4
EXP_PER_GROUP = 4
N_EXPERTS = N_GROUPS * EXP_PER_GROUP
N_PAIRS = EXP_PER_GROUP * (EXP_PER_GROUP - 1) // 2
N_BUCKETS = N_GROUPS * N_PAIRS
PAIR_A = (0, 0, 0, 1, 1, 2)
PAIR_B = (1, 2, 3, 2, 3, 3)
EXPERT_FF = 512
LN_EPS = 1e-5
RMS_EPS = 1e-6
DN_ALPHA = (2 * DEPTH) ** 0.25
NEG_INF = -1e30

LANES = 128
ROUTE_W = LANES
TOKEN_TILE = 512
EXPERT_TILE = 256
RET_CHUNK = 256
ATTN_TILE = 512
HEAD_W = 2 * LANES
VMEM_LIMIT = 56 * 1024 * 1024
ATTN_VMEM_LIMIT = 56 * 1024 * 1024


def _cparams(sem, vmem=VMEM_LIMIT):
    return pltpu.CompilerParams(dimension_semantics=sem, vmem_limit_bytes=vmem)


def _silu(x):
    return x * jax.nn.sigmoid(x)


def _layer_norm(z, g, b):
    mu = jnp.mean(z, axis=-1, keepdims=True)
    zc = z - mu
    var = jnp.mean(zc * zc, axis=-1, keepdims=True)
    return zc * lax.rsqrt(var + LN_EPS) * g + b


def _ada_kernel(c_ref, w_ref, b_ref, o_ref):
    s = _silu(c_ref[...]).astype(BF16)
    o_ref[0] = jnp.dot(s, w_ref[0].astype(BF16), preferred_element_type=F32) + b_ref[0]


def _ada(c_all, w_ada, b_ada):
    r = c_all.shape[0]
    n = w_ada.shape[-1]
    tn = 1536
    return pl.pallas_call(
        _ada_kernel,
        grid=(DEPTH, n // tn),
        in_specs=[pl.BlockSpec((r, D_MODEL), lambda l, j: (0, 0)),
                  pl.BlockSpec((1, D_MODEL, tn), lambda l, j: (l, 0, j)),
                  pl.BlockSpec((1, 1, tn), lambda l, j: (l, 0, j))],
        out_specs=pl.BlockSpec((1, r, tn), lambda l, j: (l, 0, j)),
        out_shape=jax.ShapeDtypeStruct((DEPTH, r, n), F32),
        compiler_params=_cparams(("parallel", "parallel")),
        name="ada_mod",
    )(c_all, w_ada, b_ada.reshape(DEPTH, 1, n))


class _Tiles:
    def __init__(self, b, l):
        self.b, self.l, self.t = b, l, b * l
        self.per_token = (l % TOKEN_TILE) != 0
        self.tm = self.t if self.per_token else TOKEN_TILE
        self.n = self.t // self.tm
        self.tiles_per_seq = 1 if self.per_token else l // self.tm

    def mod(self, m):
        if self.per_token:
            return jnp.repeat(m, self.l, axis=0)[None]
        return m[:, None, :]

    def mod_spec(self, arr):
        tps = self.tiles_per_seq
        return pl.BlockSpec((1,) + arr.shape[1:], lambda i: (i // tps, 0, 0))

    def table(self, tab):
        return jnp.tile(tab, (self.b, 1)) if self.per_token else tab

    def table_spec(self, tab):
        nt = tab.shape[0] // self.tm
        return pl.BlockSpec((self.tm, tab.shape[1]), lambda i: (i % nt, 0))

    def rows(self, w):
        return pl.BlockSpec((self.tm, w), lambda i: (i, 0))


def _const_spec(arr):
    nd = arr.ndim
    return pl.BlockSpec(arr.shape, lambda *_: (0,) * nd)


def _ret_in_kernel(x_ref, sc_ref, sh_ref, w_ref, cos_ref, sin_ref, q_ref, k_ref, v_ref, sg_ref):
    h = (x_ref[...] * (1.0 + sc_ref[0]) + sh_ref[0]).astype(BF16)
    cos = cos_ref[...]
    sin = sin_ref[...]
    qk_w = RET_HEADS * RET_DK
    half = RET_DK // 2

    def rope_store(r, out_ref, scale):
        for hh in range(RET_HEADS):
            a = r[:, hh * RET_DK:hh * RET_DK + half]
            b = r[:, hh * RET_DK + half:(hh + 1) * RET_DK]
            out_ref[:, hh * RET_DK:hh * RET_DK + half] = ((a * cos - b * sin) * scale).astype(BF16)
            out_ref[:, hh * RET_DK + half:(hh + 1) * RET_DK] = ((a * sin + b * cos) * scale).astype(BF16)

    r = jnp.dot(h, w_ref[:, 0:qk_w], preferred_element_type=F32)
    rope_store(r, q_ref, 1.0)
    r = jnp.dot(h, w_ref[:, qk_w:2 * qk_w], preferred_element_type=F32)
    rope_store(r, k_ref, RET_DK ** -0.5)
    for c in range(2):
        lo = 2 * qk_w + c * qk_w
        v_ref[:, c * qk_w:(c + 1) * qk_w] = jnp.dot(
            h, w_ref[:, lo:lo + qk_w], preferred_element_type=F32).astype(BF16)
    for c in range(2):
        lo = 4 * qk_w + c * qk_w
        g = jnp.dot(h, w_ref[:, lo:lo + qk_w], preferred_element_type=F32)
        sg_ref[:, c * qk_w:(c + 1) * qk_w] = _silu(g).astype(BF16)


def _ret_in(tl, x, sc, sh, w_in, cos, sin):
    qk_w, v_w = RET_HEADS * RET_DK, RET_HEADS * RET_DV
    return pl.pallas_call(
        _ret_in_kernel,
        grid=(tl.n,),
        in_specs=[tl.rows(D_MODEL), tl.mod_spec(sc), tl.mod_spec(sh), _const_spec(w_in),
                  tl.table_spec(cos), tl.table_spec(sin)],
        out_specs=[tl.rows(qk_w), tl.rows(qk_w), tl.rows(v_w), tl.rows(v_w)],
        out_shape=[jax.ShapeDtypeStruct((tl.t, qk_w), BF16), jax.ShapeDtypeStruct((tl.t, qk_w), BF16),
                   jax.ShapeDtypeStruct((tl.t, v_w), BF16), jax.ShapeDtypeStruct((tl.t, v_w), BF16)],
        compiler_params=_cparams(("parallel",)),
        name="ret_in_proj",
    )(x, sc, sh, w_in, cos, sin)


def _retention_tables(chunk):
    lg = np.log1p(-np.exp2(-5.0 - np.arange(RET_HEADS, dtype=np.float64)))
    idx = np.arange(chunk, dtype=np.float64)
    diff = idx[:, None] - idx[None, :]
    dmask = np.where(diff >= 0, np.exp(lg[:, None, None] * np.maximum(diff, 0.0)), 0.0)
    xi = np.exp(lg[:, None] * (idx[None, :] + 1.0))[:, :, None]
    zeta = np.exp(lg[:, None] * (chunk - 1.0 - idx[None, :]))[:, :, None]
    cdec = np.exp(lg * chunk)
    return (jnp.asarray(dmask, F32), jnp.asarray(xi, F32), jnp.asarray(zeta, F32),
            [float(np.float32(c)) for c in cdec])


def _retention_kernel(q_ref, k_ref, v_ref, sg_ref, s0_ref, dm_ref, xi_ref, zeta_ref,
                      o_ref, sout_ref, s_sc, *, chunk, n_chunks, cdec):
    j = pl.program_id(1)

    @pl.when(j == 0)
    def _():
        s_sc[...] = s0_ref[0]

    for c in range(n_chunks):
        rows = slice(c * chunk, (c + 1) * chunk)
        for hh in range(RET_HEADS):
            kcols = slice(hh * RET_DK, (hh + 1) * RET_DK)
            vcols = slice(hh * RET_DV, (hh + 1) * RET_DV)
            q = q_ref[0, rows, kcols]
            k = k_ref[0, rows, kcols]
            v = v_ref[0, rows, vcols]
            s_old = s_sc[hh]
            sc = lax.dot_general(q, k, (((1,), (1,)), ((), ())), preferred_element_type=F32) * dm_ref[hh]
            o = jnp.dot(sc.astype(BF16), v, preferred_element_type=F32)
            o = o + jnp.dot(q, s_old.astype(BF16), preferred_element_type=F32) * xi_ref[hh]
            kz_t = (k.astype(F32) * zeta_ref[hh]).T.astype(BF16)
            s_sc[hh] = s_old * cdec[hh] + jnp.dot(kz_t, v, preferred_element_type=F32)
            mu = jnp.mean(o, axis=-1, keepdims=True)
            oc = o - mu
            var = jnp.mean(oc * oc, axis=-1, keepdims=True)
            on = oc * lax.rsqrt(var + LN_EPS)
            o_ref[0, rows, vcols] = (sg_ref[0, rows, vcols].astype(F32) * on).astype(BF16)

    @pl.when(j == pl.num_programs(1) - 1)
    def _():
        sout_ref[0] = s_sc[...]


def _retention(b, l, q, k, v, sg, s0, chunk, block):
    qk_w, v_w = RET_HEADS * RET_DK, RET_HEADS * RET_DV
    dmask, xi, zeta, cdec = _retention_tables(chunk)
    kern = functools.partial(_retention_kernel, chunk=chunk, n_chunks=block // chunk, cdec=cdec)
    seq = lambda w: pl.BlockSpec((1, block, w), lambda bi, j: (bi, j, 0))
    st = pl.BlockSpec((1, RET_HEADS, RET_DK, RET_DV), lambda bi, j: (bi, 0, 0, 0))
    return pl.pallas_call(
        kern,
        grid=(b, l // block),
        in_specs=[seq(qk_w), seq(qk_w), seq(v_w), seq(v_w), st,
                  _const_spec(dmask), _const_spec(xi), _const_spec(zeta)],
        out_specs=[seq(v_w), st],
        out_shape=[jax.ShapeDtypeStruct((b, l, v_w), BF16),
                   jax.ShapeDtypeStruct((b, RET_HEADS, RET_DK, RET_DV), F32)],
        scratch_shapes=[pltpu.VMEM((RET_HEADS, RET_DK, RET_DV), F32)],
        compiler_params=_cparams(("parallel", "arbitrary")),
        name="retention_scan",
    )(q.reshape(b, l, qk_w), k.reshape(b, l, qk_w), v.reshape(b, l, v_w), sg.reshape(b, l, v_w),
      s0, dmask, xi, zeta)


def _route_rows(lt):
    g = [lt[i:i + 1] for i in range(N_GROUPS)]
    m = jnp.maximum(jnp.maximum(g[0], g[1]), jnp.maximum(g[2], g[3]))
    gi = jnp.where(g[0] == m, 0, jnp.where(g[1] == m, 1, jnp.where(g[2] == m, 2, 3))).astype(jnp.int32)
    denom = jnp.exp(g[0] - m) + jnp.exp(g[1] - m) + jnp.exp(g[2] - m) + jnp.exp(g[3] - m)
    p_group = 1.0 / denom
    le = []
    for e in range(EXP_PER_GROUP):
        r = [lt[N_GROUPS + gg * EXP_PER_GROUP + e:N_GROUPS + gg * EXP_PER_GROUP + e + 1]
             for gg in range(N_GROUPS)]
        le.append(jnp.where(gi == 0, r[0], jnp.where(gi == 1, r[1], jnp.where(gi == 2, r[2], r[3]))))
    me = jnp.maximum(jnp.maximum(le[0], le[1]), jnp.maximum(le[2], le[3]))
    ex = [jnp.exp(x - me) for x in le]

    def first_argmax(vals):
        mx = jnp.maximum(jnp.maximum(vals[0], vals[1]), jnp.maximum(vals[2], vals[3]))
        ix = jnp.where(vals[0] == mx, 0, jnp.where(vals[1] == mx, 1, jnp.where(vals[2] == mx, 2, 3)))
        return mx, ix.astype(jnp.int32)

    e1, i1 = first_argmax(ex)
    ex2 = [jnp.where(i1 == e, -1.0, ex[e]) for e in range(EXP_PER_GROUP)]
    e2, i2 = first_argmax(ex2)
    tot = e1 + e2
    w1 = e1 / tot * p_group
    w2 = e2 / tot * p_group
    first_low = i1 < i2
    a = jnp.where(first_low, i1, i2)
    b = jnp.where(first_low, i2, i1)
    pair = jnp.where(a == 0, b - 1, jnp.where(a == 1, b + 1, N_PAIRS - 1))
    bucket = gi * N_PAIRS + pair
    return bucket, jnp.where(first_low, w1, w2), jnp.where(first_low, w2, w1)


def _mix_out_kernel(a_ref, w_ref, x_ref, g1_ref, lng_ref, lnb_ref, sc2_ref, sh2_ref,
                    wr_ref, br_ref, x1_ref, gt_ref, gi_ref):
    y = jnp.dot(a_ref[...], w_ref[...], preferred_element_type=F32)
    x1 = _layer_norm(DN_ALPHA * x_ref[...] + g1_ref[0] * y, lng_ref[...], lnb_ref[...])
    x1_ref[...] = x1
    h2 = x1 * (1.0 + sc2_ref[0]) + sh2_ref[0]
    hi = h2.astype(BF16)
    lo = (h2 - hi.astype(F32)).astype(BF16)
    hw = jnp.dot(hi, wr_ref[...], preferred_element_type=F32)
    logits = (hw[:, :ROUTE_W] + (hw[:, ROUTE_W:] + jnp.dot(lo, wr_ref[:, :ROUTE_W], preferred_element_type=F32))
              + br_ref[...])
    lt = logits.T
    bucket, gate_a, gate_b = _route_rows(lt)
    gi_ref[0] = bucket
    tm = lt.shape[1]
    row = lax.broadcasted_iota(jnp.int32, (8, tm), 0)
    g8 = jnp.where(row == 0, gate_a, jnp.where(row == 1, gate_b, 0.0))
    gt = jnp.concatenate([g8, jnp.zeros((ROUTE_W - 8, tm), F32)], axis=0)
    gt_ref[...] = gt.T


def _mix_out(tl, a, w, x, g1, lng, lnb, sc2, sh2, wr, br, name):
    kd = a.shape[1]
    return pl.pallas_call(
        _mix_out_kernel,
        grid=(tl.n,),
        in_specs=[tl.rows(kd), _const_spec(w), tl.rows(D_MODEL), tl.mod_spec(g1),
                  _const_spec(lng), _const_spec(lnb), tl.mod_spec(sc2), tl.mod_spec(sh2),
                  _const_spec(wr), _const_spec(br)],
        out_specs=[tl.rows(D_MODEL), tl.rows(ROUTE_W),
                   pl.BlockSpec((1, 1, tl.tm), lambda i: (i, 0, 0))],
        out_shape=[jax.ShapeDtypeStruct((tl.t, D_MODEL), F32),
                   jax.ShapeDtypeStruct((tl.t, ROUTE_W), F32),
                   jax.ShapeDtypeStruct((tl.n, 1, tl.tm), jnp.int32)],
        compiler_params=_cparams(("parallel",)),
        name=name,
    )(a, w, x, g1, lng, lnb, sc2, sh2, wr, br)


TOK_SUB = D_MODEL // LANES
DMA_PRIORITIES = 2


def _moe_sort_kernel(idx_ref, x1_ref, sc2_ref, sh2_ref, gt_ref, tok_ref, gs_ref, buf0, buf1, gs_sc, sem, gsem,
                     *, tm):
    i = pl.program_id(0)
    n = pl.num_programs(0)
    bufs = (buf0, buf1)
    base = i * tm

    def wait_slot(sl):
        pltpu.make_async_copy(bufs[sl], tok_ref.at[pl.ds(0, tm * TOK_SUB)], sem.at[sl]).wait()

    def step(sl):
        @pl.when(i >= 2)
        def _():
            wait_slot(sl)

        h2 = x1_ref[...] * (1.0 + sc2_ref[0]) + sh2_ref[0]
        for s in range(TOK_SUB):
            bufs[sl][pl.ds(s, tm, stride=TOK_SUB), :] = h2[:, s * LANES:(s + 1) * LANES]
        for r in range(tm):
            d = idx_ref[base + r]
            pltpu.make_async_copy(bufs[sl].at[pl.ds(r * TOK_SUB, TOK_SUB)],
                                  tok_ref.at[pl.ds(pl.multiple_of(d * TOK_SUB, TOK_SUB), TOK_SUB)],
                                  sem.at[sl]).start(priority=r % DMA_PRIORITIES)
            gs_sc[pl.ds(d, 1), :] = gt_ref[r:r + 1, :]

        @pl.when(i == n - 1)
        def _():
            gates_out = pltpu.make_async_copy(gs_sc, gs_ref, gsem)
            gates_out.start()

            @pl.when(n >= 2)
            def _():
                wait_slot(1 - sl)
            wait_slot(sl)
            gates_out.wait()

    for sl in range(len(bufs)):
        @pl.when(i % len(bufs) == sl)
        def _():
            step(sl)


def _moe_sort(tl, x1, sc2, sh2, gt, dst_idx, n_rows):
    tm = tl.tm
    n_steps = dst_idx.shape[0] // tm
    last = tl.n - 1
    tps = tl.tiles_per_seq
    kern = functools.partial(_moe_sort_kernel, tm=tm)
    rows = lambda w: pl.BlockSpec((tm, w), lambda i, idx: (jnp.minimum(i, last), 0))
    mod = lambda arr: pl.BlockSpec((1,) + arr.shape[1:], lambda i, idx: (jnp.minimum(i, last) // tps, 0, 0))
    return pl.pallas_call(
        kern,
        grid_spec=pltpu.PrefetchScalarGridSpec(
            num_scalar_prefetch=1,
            grid=(n_steps,),
            in_specs=[rows(D_MODEL), mod(sc2), mod(sh2), rows(ROUTE_W)],
            out_specs=[pl.BlockSpec(memory_space=pl.ANY), pl.BlockSpec(memory_space=pl.ANY)],
            scratch_shapes=[pltpu.VMEM((tm * TOK_SUB, LANES), F32), pltpu.VMEM((tm * TOK_SUB, LANES), F32),
                            pltpu.VMEM((n_rows, ROUTE_W), F32),
                            pltpu.SemaphoreType.DMA((2,)), pltpu.SemaphoreType.DMA(())]),
        out_shape=[jax.ShapeDtypeStruct((n_rows * TOK_SUB, LANES), F32),
                   jax.ShapeDtypeStruct((n_rows, ROUTE_W), F32)],
        compiler_params=_cparams(("arbitrary",)),
        name="moe_sort_rows",
    )(dst_idx, x1, sc2, sh2, gt)


def _moe_kernel(ea_ref, eb_ref, nv_ref, tok_ref, gs_ref, wgu_a_ref, wgu_b_ref, wdn_a_ref, wdn_b_ref, o_ref,
                *, tme):
    i = pl.program_id(0)

    @pl.when(i < nv_ref[0])
    def _():
        hb = jnp.concatenate([tok_ref[pl.ds(s, tme, stride=TOK_SUB), :] for s in range(TOK_SUB)],
                             axis=1).astype(BF16)
        acc = None
        for lane, (wgu_ref, wdn_ref) in enumerate(((wgu_a_ref, wdn_a_ref), (wgu_b_ref, wdn_b_ref))):
            au = jnp.dot(hb, wgu_ref[0], preferred_element_type=F32)
            act = (_silu(au[:, :EXPERT_FF]) * au[:, EXPERT_FF:]).astype(BF16)
            y = jnp.dot(act, wdn_ref[0], preferred_element_type=F32)
            gy = gs_ref[:, lane:lane + 1] * y
            acc = gy if acc is None else acc + gy
        for s in range(TOK_SUB):
            o_ref[pl.ds(s, tme, stride=TOK_SUB), :] = acc[:, s * LANES:(s + 1) * LANES]

    @pl.when(i >= nv_ref[0])
    def _():
        o_ref[...] = jnp.zeros_like(o_ref)


def _moe_experts(tok, gs, tile_ea, tile_eb, n_valid, wgu, wdn, tme):
    n_tiles = gs.shape[0] // tme
    kern = functools.partial(_moe_kernel, tme=tme)
    gu = lambda pick: pl.BlockSpec((1, D_MODEL, 2 * EXPERT_FF), lambda i, ea, eb, nv: (pick(ea, eb)[i], 0, 0))
    dn = lambda pick: pl.BlockSpec((1, EXPERT_FF, D_MODEL), lambda i, ea, eb, nv: (pick(ea, eb)[i], 0, 0))
    first = lambda ea, eb: ea
    second = lambda ea, eb: eb
    return pl.pallas_call(
        kern,
        grid_spec=pltpu.PrefetchScalarGridSpec(
            num_scalar_prefetch=3,
            grid=(n_tiles,),
            in_specs=[pl.BlockSpec((tme * TOK_SUB, LANES), lambda i, ea, eb, nv: (i, 0)),
                      pl.BlockSpec((tme, ROUTE_W), lambda i, ea, eb, nv: (i, 0)),
                      gu(first), gu(second), dn(first), dn(second)],
            out_specs=pl.BlockSpec((tme * TOK_SUB, LANES), lambda i, ea, eb, nv: (i, 0))),
        out_shape=jax.ShapeDtypeStruct(tok.shape, F32),
        compiler_params=_cparams(("arbitrary",)),
        name="moe_experts",
    )(tile_ea, tile_eb, n_valid, tok, gs, wgu, wgu, wdn, wdn)


def _ln_res_kernel(idx_ref, fs_ref, x_ref, g_ref, lng_ref, lnb_ref, o_ref, buf0, buf1, buf2, sem, *, tm):
    i = pl.program_id(0)
    n = pl.num_programs(0)
    bufs = (buf0, buf1, buf2)
    n_buf = len(bufs)

    def row_copy(d, sl, row_off):
        return pltpu.make_async_copy(fs_ref.at[pl.ds(pl.multiple_of(d * TOK_SUB, TOK_SUB), TOK_SUB)],
                                     bufs[sl].at[pl.ds(row_off, TOK_SUB)], sem.at[sl])

    def wait_rows(sl):
        pltpu.make_async_copy(fs_ref.at[pl.ds(0, tm * TOK_SUB)], bufs[sl], sem.at[sl]).wait()

    def step_base(k):
        return jnp.minimum(k, n - 1) * tm

    @pl.when(i == 0)
    def _():
        for sl in range(n_buf - 1):
            base = step_base(sl)

            def body(rr, carry):
                for prio in range(DMA_PRIORITIES):
                    r = rr * DMA_PRIORITIES + prio
                    row_copy(idx_ref[base + r], sl, pl.multiple_of(r * TOK_SUB, TOK_SUB)).start(priority=prio)
                return carry

            lax.fori_loop(0, tm // DMA_PRIORITIES, body, 0, unroll=4)

    ahead = step_base(i + n_buf - 1)

    def step(sl):
        wait_rows(sl)
        f = jnp.concatenate([bufs[sl][pl.ds(s, tm, stride=TOK_SUB), :] for s in range(TOK_SUB)], axis=1)
        o_ref[...] = _layer_norm(DN_ALPHA * x_ref[...] + g_ref[0] * f, lng_ref[...], lnb_ref[...])
        for r in range(tm):
            row_copy(idx_ref[ahead + r], (sl + n_buf - 1) % n_buf, r * TOK_SUB).start(
                priority=r % DMA_PRIORITIES)

    for sl in range(n_buf):
        @pl.when(i % n_buf == sl)
        def _():
            step(sl)

            @pl.when(i == n - 1)
            def _():
                for extra in range(1, n_buf):
                    wait_rows((sl + extra) % n_buf)


def _ln_res(tl, x, fs, dest, g2, lng, lnb):
    tm = tl.tm
    tps = tl.tiles_per_seq
    kern = functools.partial(_ln_res_kernel, tm=tm)
    const = lambda arr: pl.BlockSpec(arr.shape, lambda i, idx: (0,) * arr.ndim)
    return pl.pallas_call(
        kern,
        grid_spec=pltpu.PrefetchScalarGridSpec(
            num_scalar_prefetch=1,
            grid=(tl.n,),
            in_specs=[pl.BlockSpec(memory_space=pl.ANY),
                      pl.BlockSpec((tm, D_MODEL), lambda i, idx: (i, 0)),
                      pl.BlockSpec((1,) + g2.shape[1:], lambda i, idx: (i // tps, 0, 0)),
                      const(lng), const(lnb)],
            out_specs=pl.BlockSpec((tm, D_MODEL), lambda i, idx: (i, 0)),
            scratch_shapes=[pltpu.VMEM((tm * TOK_SUB, LANES), F32)] * 3 + [pltpu.SemaphoreType.DMA((3,))]),
        out_shape=jax.ShapeDtypeStruct((tl.t, D_MODEL), F32),
        compiler_params=_cparams(("arbitrary",)),
        name="ffn_residual_ln",
    )(dest, fs, x, g2, lng, lnb)


def _moe_ffn(tl, x1, sc2, sh2, gt, gi, g2, lng, lnb, wgu, wdn, layer):
    t = tl.t
    tme = EXPERT_TILE if t % TOKEN_TILE == 0 else 64
    n_tiles = t // tme + N_BUCKETS
    n_pad = N_BUCKETS * tme
    assert n_pad % tl.tm == 0
    i32 = jnp.int32
    onehot = (gi[:, None] == jnp.arange(N_BUCKETS, dtype=i32)[None, :]).astype(i32)
    counts = jnp.sum(onehot, axis=0)
    padded = ((counts + tme - 1) // tme) * tme
    ends = jnp.cumsum(padded)
    starts = ends - padded
    dest = jnp.sum(onehot * (jnp.cumsum(onehot, axis=0) - onehot + starts[None, :]), axis=1).astype(i32)
    cpad = jnp.cumsum(padded - counts)
    k = jnp.arange(n_pad, dtype=i32)
    seg_base = jnp.concatenate([starts + counts, ends[-1:]])
    seg_first = jnp.concatenate([jnp.zeros((1,), cpad.dtype), cpad])
    seg_hot = ((k[:, None] >= seg_first[None, :])
               & (k[:, None] < jnp.concatenate([cpad, jnp.full((1,), n_pad, cpad.dtype)])[None, :])).astype(i32)
    pad_rows = (k + jnp.sum(seg_hot * (seg_base - seg_first)[None, :], axis=1)).astype(i32)
    tile_start = jnp.arange(n_tiles, dtype=i32) * tme
    tile_hot = ((tile_start[:, None] >= starts[None, :]) & (tile_start[:, None] < ends[None, :])).astype(i32)
    bucket_ids = jnp.arange(N_BUCKETS, dtype=i32)
    first_expert = layer * N_EXPERTS + (bucket_ids // N_PAIRS) * EXP_PER_GROUP
    pair_a = jnp.asarray(PAIR_A * N_GROUPS, i32)
    pair_b = jnp.asarray(PAIR_B * N_GROUPS, i32)
    last_expert = layer * N_EXPERTS + N_EXPERTS - 1
    in_use = jnp.sum(tile_hot, axis=1)
    tile_ea = (jnp.sum(tile_hot * (first_expert + pair_a)[None, :], axis=1) + (1 - in_use) * last_expert).astype(i32)
    tile_eb = (jnp.sum(tile_hot * (first_expert + pair_b)[None, :], axis=1) + (1 - in_use) * last_expert).astype(i32)
    n_valid = (ends[-1] // tme).astype(i32).reshape(1)
    tok, gs = _moe_sort(tl, x1, sc2, sh2, gt, jnp.concatenate([dest, pad_rows]), n_tiles * tme)
    fs = _moe_experts(tok, gs, tile_ea, tile_eb, n_valid, wgu, wdn, tme)
    return _ln_res(tl, x1, fs, dest, g2, lng, lnb)


def _mla_in_kernel(x_ref, sc_ref, sh_ref, wdkv_ref, kvg_ref, wukv_ref, wdq_ref, qg_ref, wuq_ref, cos_ref, sin_ref,
                   lat_ref, kr_ref, q_ref, *kv_refs, qscale, expand):
    x = x_ref[...]
    cos = cos_ref[...]
    sin = sin_ref[...]
    kv = jnp.dot(x.astype(BF16), wdkv_ref[...], preferred_element_type=F32)
    c = kv[:, :KV_LORA]
    lat = c * lax.rsqrt(jnp.mean(c * c, axis=-1, keepdims=True) + RMS_EPS) * kvg_ref[...]
    lat_ref[...] = lat
    kr = kv[:, KV_LORA:KV_LORA + LANES] * cos + kv[:, KV_LORA + LANES:] * sin
    kr_ref[...] = kr[:, :MLA_ROPE]
    if expand:
        k_ref, v_ref = kv_refs
        kvx = jnp.dot(lat.astype(BF16), wukv_ref[...], preferred_element_type=F32)
        krb = kr.astype(BF16)
        ones_col = jnp.where(lax.broadcasted_iota(jnp.int32, krb.shape, 1) == 0, 1.0, 0.0).astype(BF16)
        v0 = MLA_HEADS * MLA_NOPE
        for hh in range(MLA_HEADS):
            k_ref[:, hh * HEAD_W:hh * HEAD_W + MLA_NOPE] = kvx[:, hh * MLA_NOPE:(hh + 1) * MLA_NOPE].astype(BF16)
            k_ref[:, hh * HEAD_W + MLA_NOPE:(hh + 1) * HEAD_W] = krb
            v_ref[:, hh * HEAD_W:hh * HEAD_W + MLA_V] = kvx[:, v0 + hh * MLA_V:v0 + (hh + 1) * MLA_V].astype(BF16)
            v_ref[:, hh * HEAD_W + MLA_V:(hh + 1) * HEAD_W] = ones_col
    else:
        (krp_ref,) = kv_refs
        krp_ref[...] = kr
    h = (x * (1.0 + sc_ref[0]) + sh_ref[0]).astype(BF16)
    cq = jnp.dot(h, wdq_ref[...], preferred_element_type=F32)
    qn = (cq * lax.rsqrt(jnp.mean(cq * cq, axis=-1, keepdims=True) + RMS_EPS) * qg_ref[...]).astype(BF16)
    nw = MLA_HEADS * LANES
    qnope = jnp.dot(qn, wuq_ref[:, :nw], preferred_element_type=F32)
    qpe = jnp.dot(qn, wuq_ref[:, nw:2 * nw], preferred_element_type=F32)
    qpe_sw = jnp.dot(qn, wuq_ref[:, 2 * nw:], preferred_element_type=F32)
    for hh in range(MLA_HEADS):
        cols = slice(hh * LANES, (hh + 1) * LANES)
        q_ref[:, hh * HEAD_W:hh * HEAD_W + LANES] = (qnope[:, cols] * qscale).astype(BF16)
        pe = qpe[:, cols] * cos + qpe_sw[:, cols] * sin
        q_ref[:, hh * HEAD_W + LANES:(hh + 1) * HEAD_W] = (pe * qscale).astype(BF16)


def _mla_in(tl, x, sc, sh, p, j, cos, sin, qscale, expand):
    kern = functools.partial(_mla_in_kernel, qscale=qscale, expand=expand)
    weights = [p["w_dkv_ext"], p["kv_norm_g"], p["w_ukv_r"], p["w_dq"][j], p["q_norm_g"][j], p["w_uq_ext"][j]]
    out_w = [(KV_LORA, F32), (MLA_ROPE, F32), (MLA_HEADS * HEAD_W, BF16)]
    out_w += [(MLA_HEADS * HEAD_W, BF16)] * 2 if expand else [(LANES, F32)]
    return pl.pallas_call(
        kern,
        grid=(tl.n,),
        in_specs=[tl.rows(D_MODEL), tl.mod_spec(sc), tl.mod_spec(sh)] + [_const_spec(w) for w in weights]
                 + [tl.table_spec(cos), tl.table_spec(sin)],
        out_specs=[tl.rows(w) for w, _ in out_w],
        out_shape=[jax.ShapeDtypeStruct((tl.t, w), dt) for w, dt in out_w],
        compiler_params=_cparams(("parallel",)),
        name="mla_in_proj",
    )(x, sc, sh, *weights, cos, sin)


def _attn_kernel(q_ref, qn_ref, k_ref, v_ref, o_ref, m_sc, acc_sc, sa_sc, sb_sc, sc_sc, *, t, n_sub, hp):
    i = pl.program_id(2)
    m_sc[...] = jnp.full(m_sc.shape, NEG_INF, F32)
    acc_sc[...] = jnp.zeros(acc_sc.shape, F32)
    rs = t // n_sub

    def scores(j, s_ref, qr=q_ref):
        for h in range(hp):
            hc = slice(h * HEAD_W, (h + 1) * HEAD_W)
            k = k_ref[0, pl.ds(pl.multiple_of(j * t, t), t), hc]
            s_ref[h] = lax.dot_general(qr[0, :, hc], k, (((1,), (1,)), ((), ())), preferred_element_type=F32)

    def fold(j, s_ref, diagonal):
        for h in range(hp):
            v = v_ref[0, pl.ds(pl.multiple_of(j * t, t), t), h * HEAD_W:(h + 1) * HEAD_W]
            for u in range(n_sub):
                rows = slice(u * rs, (u + 1) * rs)
                s = s_ref[h, rows, :]
                if diagonal:
                    qpos = u * rs + lax.broadcasted_iota(jnp.int32, (rs, 1), 0)
                    kpos = lax.broadcasted_iota(jnp.int32, (1, t), 1)
                    s = jnp.where((kpos >> LOG2_CHUNK) <= (qpos >> LOG2_CHUNK), s, NEG_INF)
                m_old = m_sc[h, rows, :]
                m_new = jnp.maximum(m_old, jnp.max(s, axis=1, keepdims=True))
                alpha = jnp.exp2(m_old - m_new)
                p = jnp.exp2(s - jnp.tile(m_new, (1, t // LANES)))
                pv = jnp.dot(p.astype(BF16), v, preferred_element_type=F32)
                acc_sc[h, rows, :] = jnp.tile(alpha, (1, HEAD_W // LANES)) * acc_sc[h, rows, :] + pv
                m_sc[h, rows, :] = m_new

    def next_tile_scores():
        scores(0, sc_sc, qn_ref)

    @pl.when(i == 0)
    def _():
        scores(0, sc_sc)
        fold(0, sc_sc, True)
        next_tile_scores()

    @pl.when(i == 1)
    def _():
        scores(1, sb_sc)
        fold(0, sc_sc, False)
        next_tile_scores()
        fold(1, sb_sc, True)

    @pl.when(i >= 2)
    def _():
        scores(1, sb_sc)
        fold(0, sc_sc, False)
        scores(2, sa_sc)
        fold(1, sb_sc, False)

        def fold_pair(j):
            scores(j + 1, sb_sc)
            fold(j, sa_sc, False)
            scores(j + 2, sa_sc)
            fold(j + 1, sb_sc, False)

        def quad_step(qq, carry):
            fold_pair(2 + 4 * qq)
            fold_pair(4 + 4 * qq)
            return carry

        n_pair = i // 2 - 1
        lax.fori_loop(0, n_pair // 2, quad_step, 0)

        @pl.when(n_pair % 2 == 1)
        def _():
            fold_pair(2 * n_pair)

        @pl.when(i % 2 == 0)
        def _():
            next_tile_scores()
            fold(i, sa_sc, True)

        @pl.when(i % 2 == 1)
        def _():
            scores(i, sb_sc)
            fold(i - 1, sa_sc, False)
            next_tile_scores()
            fold(i, sb_sc, True)

    for h in range(hp):
        acc = acc_sc[h]
        o_ref[0, :, h * MLA_V:(h + 1) * MLA_V] = (acc[:, :MLA_V] / acc[:, MLA_V:MLA_V + 1]).astype(BF16)


def _attention(qc, kc, vv, b, l, t, hp):
    assert l % t == 0 and t % CHUNK == 0
    n_sub = 2 if t % (2 * LANES) == 0 else 1
    nq = l // t
    kern = functools.partial(_attn_kernel, t=t, n_sub=n_sub, hp=hp)
    scores_buf = pltpu.VMEM((hp, t, t), F32)
    return pl.pallas_call(
        kern,
        grid=(b, MLA_HEADS // hp, nq),
        in_specs=[pl.BlockSpec((1, t, hp * HEAD_W), lambda bi, h, i: (bi, i, h)),
                  pl.BlockSpec((1, t, hp * HEAD_W), lambda bi, h, i: (bi, jnp.minimum(i + 1, nq - 1), h)),
                  pl.BlockSpec((1, l, hp * HEAD_W), lambda bi, h, i: (bi, 0, h)),
                  pl.BlockSpec((1, l, hp * HEAD_W), lambda bi, h, i: (bi, 0, h))],
        out_specs=pl.BlockSpec((1, t, hp * MLA_V), lambda bi, h, i: (bi, i, h)),
        out_shape=jax.ShapeDtypeStruct((b, l, MLA_HEADS * MLA_V), BF16),
        scratch_shapes=[pltpu.VMEM((hp, t, LANES), F32), pltpu.VMEM((hp, t, HEAD_W), F32),
                        scores_buf, scores_buf, scores_buf],
        compiler_params=_cparams(("parallel", "parallel", "arbitrary"), ATTN_VMEM_LIMIT),
        name="mla_attention",
    )(qc.reshape(b, l, MLA_HEADS * HEAD_W), qc.reshape(b, l, MLA_HEADS * HEAD_W),
      kc.reshape(b, l, MLA_HEADS * HEAD_W), vv.reshape(b, l, MLA_HEADS * HEAD_W))


def _decode_attn_kernel(q_ref, lnew_ref, knew_ref, lpast_ref, kpast_ref, wukt_ref, wuv_ref, o_ref, qa_sc,
                        *, l, pos0, past):
    nope_w = KV_LORA
    for h in range(MLA_HEADS):
        rows = slice(h * l, (h + 1) * l)
        qn = q_ref[0, :, h * HEAD_W:h * HEAD_W + MLA_NOPE]
        qa_sc[rows, :nope_w] = jnp.dot(qn, wukt_ref[h], preferred_element_type=F32).astype(BF16)
        qa_sc[rows, nope_w:] = q_ref[0, :, h * HEAD_W + MLA_NOPE:(h + 1) * HEAD_W]
    q_lat = qa_sc[:, :nope_w]
    q_pe = qa_sc[:, nope_w:nope_w + MLA_ROPE]
    nt = (((1,), (1,)), ((), ()))
    lp = lpast_ref[0].astype(BF16)
    kp = kpast_ref[0].astype(BF16)
    ln = lnew_ref[0].astype(BF16)
    kn = knew_ref[0][:, :MLA_ROPE].astype(BF16)
    s_past = (lax.dot_general(q_lat, lp, nt, preferred_element_type=F32)
              + lax.dot_general(q_pe, kp, nt, preferred_element_type=F32))
    s_new = (lax.dot_general(q_lat, ln, nt, preferred_element_type=F32)
             + lax.dot_general(q_pe, kn, nt, preferred_element_type=F32))
    n_rows = MLA_HEADS * l
    qpos = pos0 + lax.rem(lax.broadcasted_iota(jnp.int32, (n_rows, 1), 0), l)
    kpos_new = pos0 + lax.broadcasted_iota(jnp.int32, (1, l), 1)
    s_new = jnp.where((kpos_new >> LOG2_CHUNK) <= (qpos >> LOG2_CHUNK), s_new, NEG_INF)
    kpos_past = lax.broadcasted_iota(jnp.int32, (1, past), 1)
    s_past = jnp.where((kpos_past >> LOG2_CHUNK) <= (qpos >> LOG2_CHUNK), s_past, NEG_INF)
    m = jnp.maximum(jnp.max(s_past, axis=1, keepdims=True), jnp.max(s_new, axis=1, keepdims=True))
    p_past = jnp.exp2(s_past - m)
    p_new = jnp.exp2(s_new - m)
    denom = jnp.sum(p_past, axis=1, keepdims=True) + jnp.sum(p_new, axis=1, keepdims=True)
    ctx = (jnp.dot(p_past.astype(BF16), lp, preferred_element_type=F32)
           + jnp.dot(p_new.astype(BF16), ln, preferred_element_type=F32)) / denom
    ctx = ctx.astype(BF16)
    for h in range(MLA_HEADS):
        o_ref[0, :, h * MLA_V:(h + 1) * MLA_V] = jnp.dot(
            ctx[h * l:(h + 1) * l], wuv_ref[h], preferred_element_type=F32).astype(BF16)


def _decode_attention(qc, latent_new, krp_new, past_latent, past_k_rope, w_uk_t, w_uv, b, l, pos0):
    past = past_latent.shape[1]
    kern = functools.partial(_decode_attn_kernel, l=l, pos0=pos0, past=past)
    per_b = lambda shape: pl.BlockSpec((1,) + shape, lambda bi: (bi, 0, 0))
    return pl.pallas_call(
        kern,
        grid=(b,),
        in_specs=[per_b((l, MLA_HEADS * HEAD_W)), per_b((l, KV_LORA)), per_b((l, LANES)),
                  per_b((past, KV_LORA)), per_b((past, MLA_ROPE)), _const_spec(w_uk_t), _const_spec(w_uv)],
        out_specs=per_b((l, MLA_HEADS * MLA_V)),
        out_shape=jax.ShapeDtypeStruct((b, l, MLA_HEADS * MLA_V), BF16),
        scratch_shapes=[pltpu.VMEM((MLA_HEADS * l, KV_LORA + LANES), BF16)],
        compiler_params=_cparams(("parallel",)),
        name="mla_decode_attention",
    )(qc.reshape(b, l, MLA_HEADS * HEAD_W), latent_new.reshape(b, l, KV_LORA), krp_new.reshape(b, l, LANES),
      past_latent, past_k_rope, w_uk_t, w_uv)


def _rope_tables(pos0, length, half):
    inv = ROPE_THETA ** (-np.arange(half, dtype=np.float64) / half)
    ang = (pos0 + np.arange(length, dtype=np.float64))[:, None] * inv[None, :]
    return np.cos(ang), np.sin(ang)


def _mla_rope_tables(pos0, length):
    cos, sin = _rope_tables(pos0, length, MLA_ROPE // 2)
    z = np.zeros((length, LANES - MLA_ROPE))
    return (jnp.asarray(np.concatenate([cos, cos, z], axis=1), F32),
            jnp.asarray(np.concatenate([-sin, sin, z], axis=1), F32))


def _swap_halves(w):
    half = w.shape[-1] // 2
    return jnp.concatenate([w[..., half:], w[..., :half]], axis=-1)


def _pad_lanes(w):
    return jnp.pad(w, [(0, 0)] * (w.ndim - 1) + [(0, LANES - w.shape[-1])])


def _prep_weights(w_ret_in, w_ret_out, w_dq, q_norm_g, w_uq, w_mla_out, w_dkv, kv_norm_g, w_ukv,
                  w_route_group, b_route_group, w_route_expert, b_route_expert,
                  w_expert_gate_up, w_expert_down):
    p = {}
    p["w_ret_in"] = [w_ret_in[i].astype(BF16) for i in range(N_A)]
    p["w_ret_out"] = [w_ret_out[i].astype(BF16) for i in range(N_A)]
    p["w_dq"] = [w_dq[j].astype(BF16) for j in range(DEPTH - N_A)]
    p["q_norm_g"] = [q_norm_g[j][None, :] for j in range(DEPTH - N_A)]
    w_uq_ext = []
    for j in range(DEPTH - N_A):
        wq = w_uq[j].reshape(Q_LORA, MLA_HEADS, MLA_NOPE + MLA_ROPE)
        nope = wq[:, :, :MLA_NOPE].reshape(Q_LORA, MLA_HEADS * LANES)
        pe = wq[:, :, MLA_NOPE:]
        w_uq_ext.append(jnp.concatenate(
            [nope, _pad_lanes(pe).reshape(Q_LORA, MLA_HEADS * LANES),
             _pad_lanes(_swap_halves(pe)).reshape(Q_LORA, MLA_HEADS * LANES)], axis=1).astype(BF16))
    p["w_uq_ext"] = w_uq_ext
    p["w_mla_out"] = [w_mla_out[j].astype(BF16) for j in range(DEPTH - N_A)]
    kr = w_dkv[:, KV_LORA:]
    p["w_dkv_ext"] = jnp.concatenate([w_dkv[:, :KV_LORA], _pad_lanes(kr), _pad_lanes(_swap_halves(kr))],
                                     axis=1).astype(BF16)
    p["kv_norm_g"] = kv_norm_g[None, :]
    w_ukv4 = w_ukv.reshape(KV_LORA, MLA_HEADS, 2, MLA_NOPE)
    p["w_ukv_r"] = w_ukv4.transpose(0, 2, 1, 3).reshape(KV_LORA, 2 * MLA_HEADS * MLA_NOPE).astype(BF16)
    p["w_uk_t"] = w_ukv4[:, :, 0, :].transpose(1, 2, 0).astype(BF16)
    p["w_uv"] = w_ukv4[:, :, 1, :].transpose(1, 0, 2).astype(BF16)
    wr = jnp.concatenate([w_route_group, w_route_expert], axis=-1)
    wr = jnp.pad(wr, ((0, 0), (0, 0), (0, ROUTE_W - wr.shape[-1])))
    wr_hi = wr.astype(BF16)
    wr_lo = (wr - wr_hi.astype(F32)).astype(BF16)
    br = jnp.concatenate([b_route_group, b_route_expert], axis=-1)
    br = jnp.pad(br, ((0, 0), (0, ROUTE_W - br.shape[-1])))
    wr_cat = jnp.concatenate([wr_hi, wr_lo], axis=-1)
    p["wr"] = [wr_cat[l] for l in range(DEPTH)]
    p["br"] = [br[l][None, :] for l in range(DEPTH)]
    p["w_gu"] = w_expert_gate_up.astype(BF16).reshape(DEPTH * N_EXPERTS, D_MODEL, 2 * EXPERT_FF)
    p["w_dn"] = w_expert_down.astype(BF16).reshape(DEPTH * N_EXPERTS, EXPERT_FF, D_MODEL)
    return p


def _trunk(x3, mod, pos0, ret_s0, past_latent, past_k_rope, ret_chunk, ret_block, ln_g, ln_b, p):
    b, l, _ = x3.shape
    tl = _Tiles(b, l)
    x = x3.reshape(tl.t, D_MODEL)
    new_ret = []
    latent_new = k_rope_new = None
    for layer in range(DEPTH):
        sh1, sc1, g1, sh2, sc2, g2 = [tl.mod(m) for m in jnp.split(mod[layer], 6, axis=-1)]
        lng = [ln_g[layer, s][None, :] for s in range(2)]
        lnb = [ln_b[layer, s][None, :] for s in range(2)]
        if layer < N_A:
            cos, sin = [jnp.asarray(tab, F32) for tab in _rope_tables(pos0, l, RET_DK // 2)]
            q, k, v, sg = _ret_in(tl, x, sc1, sh1, p["w_ret_in"][layer], tl.table(cos), tl.table(sin))
            a, s_new = _retention(b, l, q, k, v, sg, ret_s0[layer], ret_chunk, ret_block)
            a = a.reshape(tl.t, RET_HEADS * RET_DV)
            new_ret.append(s_new)
            w_out = p["w_ret_out"][layer]
            name = "ret_out_ln_route"
        else:
            j = layer - N_A
            cos, sin = _mla_rope_tables(pos0, l)
            cos, sin = tl.table(cos), tl.table(sin)
            assert layer == N_A == DEPTH - 1
            qscale = float((MLA_NOPE + MLA_ROPE) ** -0.5 * math.log2(math.e))
            if past_latent is None:
                assert pos0 == 0
                latent_new, k_rope_new, qc, kc, vv = _mla_in(tl, x, sc1, sh1, p, j, cos, sin, qscale, True)
                a = _attention(qc, kc, vv, b, l, ATTN_TILE, 2)
            else:
                latent_new, k_rope_new, qc, krp = _mla_in(tl, x, sc1, sh1, p, j, cos, sin, qscale, False)
                a = _decode_attention(qc, latent_new, krp, past_latent, past_k_rope, p["w_uk_t"], p["w_uv"],
                                      b, l, pos0)
            a = a.reshape(tl.t, MLA_HEADS * MLA_V)
            w_out = p["w_mla_out"][j]
            name = "mla_out_ln_route"
        x1, gt, gi = _mix_out(tl, a, w_out, x, g1, lng[0], lnb[0], sc2, sh2,
                              p["wr"][layer], p["br"][layer], name)
        x = _moe_ffn(tl, x1, sc2, sh2, gt, gi.reshape(tl.t), g2, lng[1], lnb[1],
                     p["w_gu"], p["w_dn"], layer)
    return (x.reshape(b, l, D_MODEL), jnp.stack(new_ret), latent_new.reshape(b, l, KV_LORA),
            k_rope_new.reshape(b, l, MLA_ROPE))


def kernel(x_prompt, x_sample, state_retention, cache_kv_latent, cache_k_rope, c_prompt, c_sample,
           w_ada, b_ada, ln_g, ln_b, w_ret_in, w_ret_out, w_dq, q_norm_g, w_uq, w_mla_out,
           w_dkv, kv_norm_g, w_ukv, w_route_group, b_route_group, w_route_expert, b_route_expert,
           w_expert_gate_up, w_expert_down):
    bp, lp, _ = x_prompt.shape
    bs, ls, _ = x_sample.shape
    p = _prep_weights(w_ret_in, w_ret_out, w_dq, q_norm_g, w_uq, w_mla_out, w_dkv, kv_norm_g, w_ukv,
                      w_route_group, b_route_group, w_route_expert, b_route_expert,
                      w_expert_gate_up, w_expert_down)
    n_seq = bp + bs
    n_rows = -(-n_seq // 8) * 8
    c_all = jnp.concatenate([c_prompt, c_sample, jnp.zeros((n_rows - n_seq, D_MODEL), F32)], axis=0)
    mod = _ada(c_all, w_ada, b_ada)
    s0_p = jnp.zeros((N_A, bp, RET_HEADS, RET_DK, RET_DV), F32)
    ret_chunk_p = RET_CHUNK if lp % RET_CHUNK == 0 else CHUNK
    y_p, st_p, lat_p, kr_p = _trunk(x_prompt, mod[:, :bp], 0, s0_p, None, None,
                                    ret_chunk_p, max(ret_chunk_p, min(lp, TOKEN_TILE)), ln_g, ln_b, p)
    past = cache_kv_latent.shape[1]
    y_s, st_s, lat_s, kr_s = _trunk(x_sample, mod[:, bp:n_seq], past, state_retention,
                                    cache_kv_latent, cache_k_rope, ls, ls, ln_g, ln_b, p)
    return (y_p, y_s, st_p, st_s, lat_p, kr_p, lat_s, kr_s)
```

```python
import functools
import math

import numpy as np
import jax
import jax.numpy as jnp
from jax import lax
from jax.experimental import pallas as pl
from jax.experimental.pallas import tpu as pltpu

F32 = jnp.float32
BF16 = jnp.bfloat16

D_MODEL = 1024
DEPTH = 2
CHUNK = 64
LOG2_CHUNK = 6
N_A = DEPTH // 2
RET_HEADS = 4
RET_DK = 256
RET_DV = 512
MLA_HEADS = 8
MLA_NOPE = 128
MLA_ROPE = 64
MLA_V = 128
Q_LORA = 384
KV_LORA = 256
ROPE_THETA = 10000.0
N_GROUPS = 4
EXP_PER_GROUP = 4
N_EXPERTS = N_GROUPS * EXP_PER_GROUP
N_PAIRS = EXP_PER_GROUP * (EXP_PER_GROUP - 1) // 2
N_BUCKETS = N_GROUPS * N_PAIRS
PAIR_A = (0, 0, 0, 1, 1, 2)
PAIR_B = (1, 2, 3, 2, 3, 3)
EXPERT_FF = 512
LN_EPS = 1e-5
RMS_EPS = 1e-6
DN_ALPHA = (2 * DEPTH) ** 0.25
NEG_INF = -1e30

LANES = 128
ROUTE_W = LANES
TOKEN_TILE = 512
EXPERT_TILE = 256
RET_CHUNK = 256
ATTN_TILE = 512
HEAD_W = 2 * LANES
VMEM_LIMIT = 56 * 1024 * 1024
ATTN_VMEM_LIMIT = 56 * 1024 * 1024


def _cparams(sem, vmem=VMEM_LIMIT):
    return pltpu.CompilerParams(dimension_semantics=sem, vmem_limit_bytes=vmem)


def _silu(x):
    return x * jax.nn.sigmoid(x)


def _layer_norm(z, g, b):
    mu = jnp.mean(z, axis=-1, keepdims=True)
    zc = z - mu
    var = jnp.mean(zc * zc, axis=-1, keepdims=True)
    return zc * lax.rsqrt(var + LN_EPS) * g + b


def _ada_kernel(c_ref, w_ref, b_ref, o_ref):
    s = _silu(c_ref[...]).astype(BF16)
    o_ref[0] = jnp.dot(s, w_ref[0].astype(BF16), preferred_element_type=F32) + b_ref[0]


def _ada(c_all, w_ada, b_ada):
    r = c_all.shape[0]
    n = w_ada.shape[-1]
    tn = 1536
    return pl.pallas_call(
        _ada_kernel,
        grid=(DEPTH, n // tn),
        in_specs=[pl.BlockSpec((r, D_MODEL), lambda l, j: (0, 0)),
                  pl.BlockSpec((1, D_MODEL, tn), lambda l, j: (l, 0, j)),
                  pl.BlockSpec((1, 1, tn), lambda l, j: (l, 0, j))],
        out_specs=pl.BlockSpec((1, r, tn), lambda l, j: (l, 0, j)),
        out_shape=jax.ShapeDtypeStruct((DEPTH, r, n), F32),
        compiler_params=_cparams(("parallel", "parallel")),
        name="ada_mod",
    )(c_all, w_ada, b_ada.reshape(DEPTH, 1, n))


class _Tiles:
    def __init__(self, b, l):
        self.b, self.l, self.t = b, l, b * l
        self.per_token = (l % TOKEN_TILE) != 0
        self.tm = self.t if self.per_token else TOKEN_TILE
        self.n = self.t // self.tm
        self.tiles_per_seq = 1 if self.per_token else l // self.tm

    def mod(self, m):
        if self.per_token:
            return jnp.repeat(m, self.l, axis=0)[None]
        return m[:, None, :]

    def mod_spec(self, arr):
        tps = self.tiles_per_seq
        return pl.BlockSpec((1,) + arr.shape[1:], lambda i: (i // tps, 0, 0))

    def table(self, tab):
        return jnp.tile(tab, (self.b, 1)) if self.per_token else tab

    def table_spec(self, tab):
        nt = tab.shape[0] // self.tm
        return pl.BlockSpec((self.tm, tab.shape[1]), lambda i: (i % nt, 0))

    def rows(self, w):
        return pl.BlockSpec((self.tm, w), lambda i: (i, 0))


def _const_spec(arr):
    nd = arr.ndim
    return pl.BlockSpec(arr.shape, lambda *_: (0,) * nd)


def _ret_in_kernel(x_ref, sc_ref, sh_ref, w_ref, cos_ref, sin_ref, q_ref, k_ref, v_ref, sg_ref):
    h = (x_ref[...] * (1.0 + sc_ref[0]) + sh_ref[0]).astype(BF16)
    cos = cos_ref[...]
    sin = sin_ref[...]
    qk_w = RET_HEADS * RET_DK
    half = RET_DK // 2

    def rope_store(r, out_ref, scale):
        for hh in range(RET_HEADS):
            a = r[:, hh * RET_DK:hh * RET_DK + half]
            b = r[:, hh * RET_DK + half:(hh + 1) * RET_DK]
            out_ref[:, hh * RET_DK:hh * RET_DK + half] = ((a * cos - b * sin) * scale).astype(BF16)
            out_ref[:, hh * RET_DK + half:(hh + 1) * RET_DK] = ((a * sin + b * cos) * scale).astype(BF16)

    r = jnp.dot(h, w_ref[:, 0:qk_w], preferred_element_type=F32)
    rope_store(r, q_ref, 1.0)
    r = jnp.dot(h, w_ref[:, qk_w:2 * qk_w], preferred_element_type=F32)
    rope_store(r, k_ref, RET_DK ** -0.5)
    for c in range(2):
        lo = 2 * qk_w + c * qk_w
        v_ref[:, c * qk_w:(c + 1) * qk_w] = jnp.dot(
            h, w_ref[:, lo:lo + qk_w], preferred_element_type=F32).astype(BF16)
    for c in range(2):
        lo = 4 * qk_w + c * qk_w
        g = jnp.dot(h, w_ref[:, lo:lo + qk_w], preferred_element_type=F32)
        sg_ref[:, c * qk_w:(c + 1) * qk_w] = _silu(g).astype(BF16)


def _ret_in(tl, x, sc, sh, w_in, cos, sin):
    qk_w, v_w = RET_HEADS * RET_DK, RET_HEADS * RET_DV
    return pl.pallas_call(
        _ret_in_kernel,
        grid=(tl.n,),
        in_specs=[tl.rows(D_MODEL), tl.mod_spec(sc), tl.mod_spec(sh), _const_spec(w_in),
                  tl.table_spec(cos), tl.table_spec(sin)],
        out_specs=[tl.rows(qk_w), tl.rows(qk_w), tl.rows(v_w), tl.rows(v_w)],
        out_shape=[jax.ShapeDtypeStruct((tl.t, qk_w), BF16), jax.ShapeDtypeStruct((tl.t, qk_w), BF16),
                   jax.ShapeDtypeStruct((tl.t, v_w), BF16), jax.ShapeDtypeStruct((tl.t, v_w), BF16)],
        compiler_params=_cparams(("parallel",)),
        name="ret_in_proj",
    )(x, sc, sh, w_in, cos, sin)


def _retention_tables(chunk):
    lg = np.log1p(-np.exp2(-5.0 - np.arange(RET_HEADS, dtype=np.float64)))
    idx = np.arange(chunk, dtype=np.float64)
    diff = idx[:, None] - idx[None, :]
    dmask = np.where(diff >= 0, np.exp(lg[:, None, None] * np.maximum(diff, 0.0)), 0.0)
    xi = np.exp(lg[:, None] * (idx[None, :] + 1.0))[:, :, None]
    zeta = np.exp(lg[:, None] * (chunk - 1.0 - idx[None, :]))[:, :, None]
    cdec = np.exp(lg * chunk)
    return (jnp.asarray(dmask, F32), jnp.asarray(xi, F32), jnp.asarray(zeta, F32),
            [float(np.float32(c)) for c in cdec])


def _retention_kernel(q_ref, k_ref, v_ref, sg_ref, s0_ref, dm_ref, xi_ref, zeta_ref,
                      o_ref, sout_ref, s_sc, *, chunk, n_chunks, cdec):
    j = pl.program_id(1)

    @pl.when(j == 0)
    def _():
        s_sc[...] = s0_ref[0]

    for c in range(n_chunks):
        rows = slice(c * chunk, (c + 1) * chunk)
        for hh in range(RET_HEADS):
            kcols = slice(hh * RET_DK, (hh + 1) * RET_DK)
            vcols = slice(hh * RET_DV, (hh + 1) * RET_DV)
            q = q_ref[0, rows, kcols]
            k = k_ref[0, rows, kcols]
            v = v_ref[0, rows, vcols]
            s_old = s_sc[hh]
            sc = lax.dot_general(q, k, (((1,), (1,)), ((), ())), preferred_element_type=F32) * dm_ref[hh]
            o = jnp.dot(sc.astype(BF16), v, preferred_element_type=F32)
            o = o + jnp.dot(q, s_old.astype(BF16), preferred_element_type=F32) * xi_ref[hh]
            kz_t = (k.astype(F32) * zeta_ref[hh]).T.astype(BF16)
            s_sc[hh] = s_old * cdec[hh] + jnp.dot(kz_t, v, preferred_element_type=F32)
            mu = jnp.mean(o, axis=-1, keepdims=True)
            oc = o - mu
            var = jnp.mean(oc * oc, axis=-1, keepdims=True)
            on = oc * lax.rsqrt(var + LN_EPS)
            o_ref[0, rows, vcols] = (sg_ref[0, rows, vcols].astype(F32) * on).astype(BF16)

    @pl.when(j == pl.num_programs(1) - 1)
    def _():
        sout_ref[0] = s_sc[...]


def _retention(b, l, q, k, v, sg, s0, chunk, block):
    qk_w, v_w = RET_HEADS * RET_DK, RET_HEADS * RET_DV
    dmask, xi, zeta, cdec = _retention_tables(chunk)
    kern = functools.partial(_retention_kernel, chunk=chunk, n_chunks=block // chunk, cdec=cdec)
    seq = lambda w: pl.BlockSpec((1, block, w), lambda bi, j: (bi, j, 0))
    st = pl.BlockSpec((1, RET_HEADS, RET_DK, RET_DV), lambda bi, j: (bi, 0, 0, 0))
    return pl.pallas_call(
        kern,
        grid=(b, l // block),
        in_specs=[seq(qk_w), seq(qk_w), seq(v_w), seq(v_w), st,
                  _const_spec(dmask), _const_spec(xi), _const_spec(zeta)],
        out_specs=[seq(v_w), st],
        out_shape=[jax.ShapeDtypeStruct((b, l, v_w), BF16),
                   jax.ShapeDtypeStruct((b, RET_HEADS, RET_DK, RET_DV), F32)],
        scratch_shapes=[pltpu.VMEM((RET_HEADS, RET_DK, RET_DV), F32)],
        compiler_params=_cparams(("parallel", "arbitrary")),
        name="retention_scan",
    )(q.reshape(b, l, qk_w), k.reshape(b, l, qk_w), v.reshape(b, l, v_w), sg.reshape(b, l, v_w),
      s0, dmask, xi, zeta)


def _route_rows(lt):
    g = [lt[i:i + 1] for i in range(N_GROUPS)]
    m = jnp.maximum(jnp.maximum(g[0], g[1]), jnp.maximum(g[2], g[3]))
    gi = jnp.where(g[0] == m, 0, jnp.where(g[1] == m, 1, jnp.where(g[2] == m, 2, 3))).astype(jnp.int32)
    denom = jnp.exp(g[0] - m) + jnp.exp(g[1] - m) + jnp.exp(g[2] - m) + jnp.exp(g[3] - m)
    p_group = 1.0 / denom
    le = []
    for e in range(EXP_PER_GROUP):
        r = [lt[N_GROUPS + gg * EXP_PER_GROUP + e:N_GROUPS + gg * EXP_PER_GROUP + e + 1]
             for gg in range(N_GROUPS)]
        le.append(jnp.where(gi == 0, r[0], jnp.where(gi == 1, r[1], jnp.where(gi == 2, r[2], r[3]))))
    me = jnp.maximum(jnp.maximum(le[0], le[1]), jnp.maximum(le[2], le[3]))
    ex = [jnp.exp(x - me) for x in le]

    def first_argmax(vals):
        mx = jnp.maximum(jnp.maximum(vals[0], vals[1]), jnp.maximum(vals[2], vals[3]))
        ix = jnp.where(vals[0] == mx, 0, jnp.where(vals[1] == mx, 1, jnp.where(vals[2] == mx, 2, 3)))
        return mx, ix.astype(jnp.int32)

    e1, i1 = first_argmax(ex)
    ex2 = [jnp.where(i1 == e, -1.0, ex[e]) for e in range(EXP_PER_GROUP)]
    e2, i2 = first_argmax(ex2)
    tot = e1 + e2
    w1 = e1 / tot * p_group
    w2 = e2 / tot * p_group
    first_low = i1 < i2
    a = jnp.where(first_low, i1, i2)
    b = jnp.where(first_low, i2, i1)
    pair = jnp.where(a == 0, b - 1, jnp.where(a == 1, b + 1, N_PAIRS - 1))
    bucket = gi * N_PAIRS + pair
    return bucket, gi * EXP_PER_GROUP + a, gi * EXP_PER_GROUP + b, \
        jnp.where(first_low, w1, w2), jnp.where(first_low, w2, w1)


def _mix_out_kernel(a_ref, w_ref, x_ref, g1_ref, lng_ref, lnb_ref, sc2_ref, sh2_ref,
                    wr_ref, br_ref, x1_ref, gt_ref, gi_ref, *, dense_gates):
    y = jnp.dot(a_ref[...], w_ref[...], preferred_element_type=F32)
    x1 = _layer_norm(DN_ALPHA * x_ref[...] + g1_ref[0] * y, lng_ref[...], lnb_ref[...])
    x1_ref[...] = x1
    h2 = x1 * (1.0 + sc2_ref[0]) + sh2_ref[0]
    hi = h2.astype(BF16)
    lo = (h2 - hi.astype(F32)).astype(BF16)
    hw = jnp.dot(hi, wr_ref[...], preferred_element_type=F32)
    logits = (hw[:, :ROUTE_W] + (hw[:, ROUTE_W:] + jnp.dot(lo, wr_ref[:, :ROUTE_W], preferred_element_type=F32))
              + br_ref[...])
    lt = logits.T
    bucket, exp_a, exp_b, gate_a, gate_b = _route_rows(lt)
    gi_ref[0] = bucket
    tm = lt.shape[1]
    if dense_gates:
        row = lax.broadcasted_iota(jnp.int32, (ROUTE_W, tm), 0)
        gt = jnp.where(row == exp_a, gate_a, jnp.where(row == exp_b, gate_b, 0.0))
    else:
        row = lax.broadcasted_iota(jnp.int32, (8, tm), 0)
        g8 = jnp.where(row == 0, gate_a, jnp.where(row == 1, gate_b, 0.0))
        gt = jnp.concatenate([g8, jnp.zeros((ROUTE_W - 8, tm), F32)], axis=0)
    gt_ref[...] = gt.T


def _mix_out(tl, a, w, x, g1, lng, lnb, sc2, sh2, wr, br, name):
    kd = a.shape[1]
    return pl.pallas_call(
        functools.partial(_mix_out_kernel, dense_gates=tl.per_token),
        grid=(tl.n,),
        in_specs=[tl.rows(kd), _const_spec(w), tl.rows(D_MODEL), tl.mod_spec(g1),
                  _const_spec(lng), _const_spec(lnb), tl.mod_spec(sc2), tl.mod_spec(sh2),
                  _const_spec(wr), _const_spec(br)],
        out_specs=[tl.rows(D_MODEL), tl.rows(ROUTE_W),
                   pl.BlockSpec((1, 1, tl.tm), lambda i: (i, 0, 0))],
        out_shape=[jax.ShapeDtypeStruct((tl.t, D_MODEL), F32),
                   jax.ShapeDtypeStruct((tl.t, ROUTE_W), F32),
                   jax.ShapeDtypeStruct((tl.n, 1, tl.tm), jnp.int32)],
        compiler_params=_cparams(("parallel",)),
        name=name,
    )(a, w, x, g1, lng, lnb, sc2, sh2, wr, br)


TOK_SUB = D_MODEL // LANES
DMA_PRIORITIES = 2


def _moe_sort_kernel(idx_ref, x1_ref, sc2_ref, sh2_ref, gt_ref, tok_ref, gs_ref, buf0, buf1, gs_sc, sem, gsem,
                     *, tm):
    i = pl.program_id(0)
    n = pl.num_programs(0)
    bufs = (buf0, buf1)
    base = i * tm

    def wait_slot(sl):
        pltpu.make_async_copy(bufs[sl], tok_ref.at[pl.ds(0, tm * TOK_SUB)], sem.at[sl]).wait()

    def step(sl):
        @pl.when(i >= 2)
        def _():
            wait_slot(sl)

        h2 = x1_ref[...] * (1.0 + sc2_ref[0]) + sh2_ref[0]
        for s in range(TOK_SUB):
            bufs[sl][pl.ds(s, tm, stride=TOK_SUB), :] = h2[:, s * LANES:(s + 1) * LANES]
        for r in range(tm):
            d = idx_ref[base + r]
            pltpu.make_async_copy(bufs[sl].at[pl.ds(r * TOK_SUB, TOK_SUB)],
                                  tok_ref.at[pl.ds(pl.multiple_of(d * TOK_SUB, TOK_SUB), TOK_SUB)],
                                  sem.at[sl]).start(priority=r % DMA_PRIORITIES)
            gs_sc[pl.ds(d, 1), :] = gt_ref[r:r + 1, :]

        @pl.when(i == n - 1)
        def _():
            gates_out = pltpu.make_async_copy(gs_sc, gs_ref, gsem)
            gates_out.start()

            @pl.when(n >= 2)
            def _():
                wait_slot(1 - sl)
            wait_slot(sl)
            gates_out.wait()

    for sl in range(len(bufs)):
        @pl.when(i % len(bufs) == sl)
        def _():
            step(sl)


def _moe_sort(tl, x1, sc2, sh2, gt, dst_idx, n_rows):
    tm = tl.tm
    n_steps = dst_idx.shape[0] // tm
    last = tl.n - 1
    tps = tl.tiles_per_seq
    kern = functools.partial(_moe_sort_kernel, tm=tm)
    rows = lambda w: pl.BlockSpec((tm, w), lambda i, idx: (jnp.minimum(i, last), 0))
    mod = lambda arr: pl.BlockSpec((1,) + arr.shape[1:], lambda i, idx: (jnp.minimum(i, last) // tps, 0, 0))
    return pl.pallas_call(
        kern,
        grid_spec=pltpu.PrefetchScalarGridSpec(
            num_scalar_prefetch=1,
            grid=(n_steps,),
            in_specs=[rows(D_MODEL), mod(sc2), mod(sh2), rows(ROUTE_W)],
            out_specs=[pl.BlockSpec(memory_space=pl.ANY), pl.BlockSpec(memory_space=pl.ANY)],
            scratch_shapes=[pltpu.VMEM((tm * TOK_SUB, LANES), F32), pltpu.VMEM((tm * TOK_SUB, LANES), F32),
                            pltpu.VMEM((n_rows, ROUTE_W), F32),
                            pltpu.SemaphoreType.DMA((2,)), pltpu.SemaphoreType.DMA(())]),
        out_shape=[jax.ShapeDtypeStruct((n_rows * TOK_SUB, LANES), F32),
                   jax.ShapeDtypeStruct((n_rows, ROUTE_W), F32)],
        compiler_params=_cparams(("arbitrary",)),
        name="moe_sort_rows",
    )(dst_idx, x1, sc2, sh2, gt)


def _expert_ffn(hb, wgu, wdn):
    au = jnp.dot(hb, wgu, preferred_element_type=F32)
    act = (_silu(au[:, :EXPERT_FF]) * au[:, EXPERT_FF:]).astype(BF16)
    return jnp.dot(act, wdn, preferred_element_type=F32)


def _moe_kernel(ea_ref, eb_ref, new_ref, nv_ref, tok_ref, gs_ref, wgu_a_ref, wgu_b_ref, wdn_a_ref, wdn_b_ref,
                o_ref, wgu_sc, wdn_sc, *, tme):
    i = pl.program_id(0)
    valid = i < nv_ref[0]

    @pl.when(valid & (new_ref[i] == 1))
    def _():
        wgu_sc[0] = wgu_a_ref[0].astype(BF16)
        wgu_sc[1] = wgu_b_ref[0].astype(BF16)
        wdn_sc[0] = wdn_a_ref[0].astype(BF16)
        wdn_sc[1] = wdn_b_ref[0].astype(BF16)

    @pl.when(valid)
    def _():
        hb = jnp.concatenate([tok_ref[pl.ds(s, tme, stride=TOK_SUB), :] for s in range(TOK_SUB)],
                             axis=1).astype(BF16)
        acc = (gs_ref[:, 0:1] * _expert_ffn(hb, wgu_sc[0], wdn_sc[0])
               + gs_ref[:, 1:2] * _expert_ffn(hb, wgu_sc[1], wdn_sc[1]))
        for s in range(TOK_SUB):
            o_ref[pl.ds(s, tme, stride=TOK_SUB), :] = acc[:, s * LANES:(s + 1) * LANES]

    @pl.when(jnp.logical_not(valid))
    def _():
        o_ref[...] = jnp.zeros_like(o_ref)


def _moe_experts(tok, gs, tile_ea, tile_eb, tile_new, n_valid, wgu, wdn, tme):
    n_tiles = gs.shape[0] // tme
    kern = functools.partial(_moe_kernel, tme=tme)
    gu = lambda pick: pl.BlockSpec((1, D_MODEL, 2 * EXPERT_FF),
                                   lambda i, ea, eb, new, nv: (pick(ea, eb)[i], 0, 0))
    dn = lambda pick: pl.BlockSpec((1, EXPERT_FF, D_MODEL),
                                   lambda i, ea, eb, new, nv: (pick(ea, eb)[i], 0, 0))
    first = lambda ea, eb: ea
    second = lambda ea, eb: eb
    return pl.pallas_call(
        kern,
        grid_spec=pltpu.PrefetchScalarGridSpec(
            num_scalar_prefetch=4,
            grid=(n_tiles,),
            in_specs=[pl.BlockSpec((tme * TOK_SUB, LANES), lambda i, ea, eb, new, nv: (i, 0)),
                      pl.BlockSpec((tme, ROUTE_W), lambda i, ea, eb, new, nv: (i, 0)),
                      gu(first), gu(second), dn(first), dn(second)],
            out_specs=pl.BlockSpec((tme * TOK_SUB, LANES), lambda i, ea, eb, new, nv: (i, 0)),
            scratch_shapes=[pltpu.VMEM((2, D_MODEL, 2 * EXPERT_FF), BF16),
                            pltpu.VMEM((2, EXPERT_FF, D_MODEL), BF16)]),
        out_shape=jax.ShapeDtypeStruct(tok.shape, F32),
        compiler_params=_cparams(("arbitrary",)),
        name="moe_experts",
    )(tile_ea, tile_eb, tile_new, n_valid, tok, gs, wgu, wgu, wdn, wdn)


def _moe_dense_kernel(x1_ref, sc2_ref, sh2_ref, gt_ref, g2_ref, lng_ref, lnb_ref, wgu_ref, wdn_ref, o_ref,
                      hb_sc, acc_sc):
    e = pl.program_id(0)

    @pl.when(e == 0)
    def _():
        hb_sc[...] = (x1_ref[...] * (1.0 + sc2_ref[0]) + sh2_ref[0]).astype(BF16)
        acc_sc[...] = jnp.zeros_like(acc_sc)

    lane = lax.broadcasted_iota(jnp.int32, gt_ref.shape, 1)
    gate = jnp.sum(jnp.where(lane == e, gt_ref[...], 0.0), axis=1, keepdims=True)
    acc_sc[...] += gate * _expert_ffn(hb_sc[...], wgu_ref[0].astype(BF16), wdn_ref[0].astype(BF16))

    @pl.when(e == pl.num_programs(0) - 1)
    def _():
        o_ref[...] = _layer_norm(DN_ALPHA * x1_ref[...] + g2_ref[0] * acc_sc[...], lng_ref[...], lnb_ref[...])


def _moe_dense(tl, x1, sc2, sh2, gt, g2, lng, lnb, wgu, wdn, layer):
    t = tl.t
    full = lambda arr: pl.BlockSpec(arr.shape, lambda e: (0,) * arr.ndim)
    return pl.pallas_call(
        _moe_dense_kernel,
        grid=(N_EXPERTS,),
        in_specs=[full(x1), full(sc2), full(sh2), full(gt), full(g2), full(lng), full(lnb),
                  pl.BlockSpec((1, D_MODEL, 2 * EXPERT_FF), lambda e: (layer * N_EXPERTS + e, 0, 0)),
                  pl.BlockSpec((1, EXPERT_FF, D_MODEL), lambda e: (layer * N_EXPERTS + e, 0, 0))],
        out_specs=pl.BlockSpec((t, D_MODEL), lambda e: (0, 0)),
        out_shape=jax.ShapeDtypeStruct((t, D_MODEL), F32),
        scratch_shapes=[pltpu.VMEM((t, D_MODEL), BF16), pltpu.VMEM((t, D_MODEL), F32)],
        compiler_params=_cparams(("arbitrary",)),
        name="moe_dense_ln",
    )(x1, sc2, sh2, gt, g2, lng, lnb, wgu, wdn)


def _ln_res_kernel(idx_ref, fs_ref, x_ref, g_ref, lng_ref, lnb_ref, o_ref, buf0, buf1, buf2, sem, *, tm):
    i = pl.program_id(0)
    n = pl.num_programs(0)
    bufs = (buf0, buf1, buf2)
    n_buf = len(bufs)

    def row_copy(d, sl, row_off):
        return pltpu.make_async_copy(fs_ref.at[pl.ds(pl.multiple_of(d * TOK_SUB, TOK_SUB), TOK_SUB)],
                                     bufs[sl].at[pl.ds(row_off, TOK_SUB)], sem.at[sl])

    def wait_rows(sl):
        pltpu.make_async_copy(fs_ref.at[pl.ds(0, tm * TOK_SUB)], bufs[sl], sem.at[sl]).wait()

    def step_base(k):
        return jnp.minimum(k, n - 1) * tm

    @pl.when(i == 0)
    def _():
        for sl in range(n_buf - 1):
            base = step_base(sl)

            def body(rr, carry):
                for prio in range(DMA_PRIORITIES):
                    r = rr * DMA_PRIORITIES + prio
                    row_copy(idx_ref[base + r], sl, pl.multiple_of(r * TOK_SUB, TOK_SUB)).start(priority=prio)
                return carry

            lax.fori_loop(0, tm // DMA_PRIORITIES, body, 0, unroll=4)

    ahead = step_base(i + n_buf - 1)

    def step(sl):
        wait_rows(sl)
        f = jnp.concatenate([bufs[sl][pl.ds(s, tm, stride=TOK_SUB), :] for s in range(TOK_SUB)], axis=1)
        o_ref[...] = _layer_norm(DN_ALPHA * x_ref[...] + g_ref[0] * f, lng_ref[...], lnb_ref[...])
        for r in range(tm):
            row_copy(idx_ref[ahead + r], (sl + n_buf - 1) % n_buf, r * TOK_SUB).start(
                priority=r % DMA_PRIORITIES)

    for sl in range(n_buf):
        @pl.when(i % n_buf == sl)
        def _():
            step(sl)

            @pl.when(i == n - 1)
            def _():
                for extra in range(1, n_buf):
                    wait_rows((sl + extra) % n_buf)


def _ln_res(tl, x, fs, dest, g2, lng, lnb):
    tm = tl.tm
    tps = tl.tiles_per_seq
    kern = functools.partial(_ln_res_kernel, tm=tm)
    const = lambda arr: pl.BlockSpec(arr.shape, lambda i, idx: (0,) * arr.ndim)
    return pl.pallas_call(
        kern,
        grid_spec=pltpu.PrefetchScalarGridSpec(
            num_scalar_prefetch=1,
            grid=(tl.n,),
            in_specs=[pl.BlockSpec(memory_space=pl.ANY),
                      pl.BlockSpec((tm, D_MODEL), lambda i, idx: (i, 0)),
                      pl.BlockSpec((1,) + g2.shape[1:], lambda i, idx: (i // tps, 0, 0)),
                      const(lng), const(lnb)],
            out_specs=pl.BlockSpec((tm, D_MODEL), lambda i, idx: (i, 0)),
            scratch_shapes=[pltpu.VMEM((tm * TOK_SUB, LANES), F32)] * 3 + [pltpu.SemaphoreType.DMA((3,))]),
        out_shape=jax.ShapeDtypeStruct((tl.t, D_MODEL), F32),
        compiler_params=_cparams(("arbitrary",)),
        name="ffn_residual_ln",
    )(dest, fs, x, g2, lng, lnb)


def _moe_ffn(tl, x1, sc2, sh2, gt, gi, g2, lng, lnb, wgu, wdn, layer):
    if tl.per_token:
        return _moe_dense(tl, x1, sc2, sh2, gt, g2, lng, lnb, wgu, wdn, layer)
    t = tl.t
    tme = EXPERT_TILE
    n_tiles = t // tme + N_BUCKETS
    n_pad = N_BUCKETS * tme
    assert n_pad % tl.tm == 0
    i32 = jnp.int32
    onehot = (gi[:, None] == jnp.arange(N_BUCKETS, dtype=i32)[None, :]).astype(i32)
    counts = jnp.sum(onehot, axis=0)
    padded = ((counts + tme - 1) // tme) * tme
    ends = jnp.cumsum(padded)
    starts = ends - padded
    dest = jnp.sum(onehot * (jnp.cumsum(onehot, axis=0) - onehot + starts[None, :]), axis=1).astype(i32)
    cpad = jnp.cumsum(padded - counts)
    k = jnp.arange(n_pad, dtype=i32)
    seg_base = jnp.concatenate([starts + counts, ends[-1:]])
    seg_first = jnp.concatenate([jnp.zeros((1,), cpad.dtype), cpad])
    seg_hot = ((k[:, None] >= seg_first[None, :])
               & (k[:, None] < jnp.concatenate([cpad, jnp.full((1,), n_pad, cpad.dtype)])[None, :])).astype(i32)
    pad_rows = (k + jnp.sum(seg_hot * (seg_base - seg_first)[None, :], axis=1)).astype(i32)
    tile_start = jnp.arange(n_tiles, dtype=i32) * tme
    tile_hot = ((tile_start[:, None] >= starts[None, :]) & (tile_start[:, None] < ends[None, :])).astype(i32)
    bucket_ids = jnp.arange(N_BUCKETS, dtype=i32)
    first_expert = layer * N_EXPERTS + (bucket_ids // N_PAIRS) * EXP_PER_GROUP
    pair_a = jnp.asarray(PAIR_A * N_GROUPS, i32)
    pair_b = jnp.asarray(PAIR_B * N_GROUPS, i32)
    last_expert = layer * N_EXPERTS + N_EXPERTS - 1
    in_use = jnp.sum(tile_hot, axis=1)
    tile_ea = (jnp.sum(tile_hot * (first_expert + pair_a)[None, :], axis=1) + (1 - in_use) * last_expert).astype(i32)
    tile_eb = (jnp.sum(tile_hot * (first_expert + pair_b)[None, :], axis=1) + (1 - in_use) * last_expert).astype(i32)
    tile_bucket = jnp.sum(tile_hot * bucket_ids[None, :], axis=1) - (1 - in_use)
    tile_new = (tile_bucket != jnp.concatenate([jnp.full((1,), -2, i32), tile_bucket[:-1]])).astype(i32)
    n_valid = (ends[-1] // tme).astype(i32).reshape(1)
    tok, gs = _moe_sort(tl, x1, sc2, sh2, gt, jnp.concatenate([dest, pad_rows]), n_tiles * tme)
    fs = _moe_experts(tok, gs, tile_ea, tile_eb, tile_new, n_valid, wgu, wdn, tme)
    return _ln_res(tl, x1, fs, dest, g2, lng, lnb)


def _mla_in_kernel(x_ref, sc_ref, sh_ref, wdkv_ref, kvg_ref, wukv_ref, wdq_ref, qg_ref, wuq_ref, cos_ref, sin_ref,
                   lat_ref, kr_ref, q_ref, *kv_refs, qscale, expand):
    x = x_ref[...]
    cos = cos_ref[...]
    sin = sin_ref[...]
    kv = jnp.dot(x.astype(BF16), wdkv_ref[...], preferred_element_type=F32)
    c = kv[:, :KV_LORA]
    lat = c * lax.rsqrt(jnp.mean(c * c, axis=-1, keepdims=True) + RMS_EPS) * kvg_ref[...]
    lat_ref[...] = lat
    kr = kv[:, KV_LORA:KV_LORA + LANES] * cos + kv[:, KV_LORA + LANES:] * sin
    kr_ref[...] = kr[:, :MLA_ROPE]
    if expand:
        k_ref, v_ref = kv_refs
        kvx = jnp.dot(lat.astype(BF16), wukv_ref[...], preferred_element_type=F32)
        krb = kr.astype(BF16)
        ones_col = jnp.where(lax.broadcasted_iota(jnp.int32, krb.shape, 1) == 0, 1.0, 0.0).astype(BF16)
        v0 = MLA_HEADS * MLA_NOPE
        for hh in range(MLA_HEADS):
            k_ref[:, hh * HEAD_W:hh * HEAD_W + MLA_NOPE] = kvx[:, hh * MLA_NOPE:(hh + 1) * MLA_NOPE].astype(BF16)
            k_ref[:, hh * HEAD_W + MLA_NOPE:(hh + 1) * HEAD_W] = krb
            v_ref[:, hh * HEAD_W:hh * HEAD_W + MLA_V] = kvx[:, v0 + hh * MLA_V:v0 + (hh + 1) * MLA_V].astype(BF16)
            v_ref[:, hh * HEAD_W + MLA_V:(hh + 1) * HEAD_W] = ones_col
    else:
        (krp_ref,) = kv_refs
        krp_ref[...] = kr
    h = (x * (1.0 + sc_ref[0]) + sh_ref[0]).astype(BF16)
    cq = jnp.dot(h, wdq_ref[...], preferred_element_type=F32)
    qn = (cq * lax.rsqrt(jnp.mean(cq * cq, axis=-1, keepdims=True) + RMS_EPS) * qg_ref[...]).astype(BF16)
    nw = MLA_HEADS * LANES
    qnope = jnp.dot(qn, wuq_ref[:, :nw], preferred_element_type=F32)
    qpe = jnp.dot(qn, wuq_ref[:, nw:2 * nw], preferred_element_type=F32)
    qpe_sw = jnp.dot(qn, wuq_ref[:, 2 * nw:], preferred_element_type=F32)
    for hh in range(MLA_HEADS):
        cols = slice(hh * LANES, (hh + 1) * LANES)
        q_ref[:, hh * HEAD_W:hh * HEAD_W + LANES] = (qnope[:, cols] * qscale).astype(BF16)
        pe = qpe[:, cols] * cos + qpe_sw[:, cols] * sin
        q_ref[:, hh * HEAD_W + LANES:(hh + 1) * HEAD_W] = (pe * qscale).astype(BF16)


def _mla_in(tl, x, sc, sh, p, j, cos, sin, qscale, expand):
    kern = functools.partial(_mla_in_kernel, qscale=qscale, expand=expand)
    weights = [p["w_dkv_ext"], p["kv_norm_g"], p["w_ukv_r"], p["w_dq"][j], p["q_norm_g"][j], p["w_uq_ext"][j]]
    out_w = [(KV_LORA, F32), (MLA_ROPE, F32), (MLA_HEADS * HEAD_W, BF16)]
    out_w += [(MLA_HEADS * HEAD_W, BF16)] * 2 if expand else [(LANES, F32)]
    return pl.pallas_call(
        kern,
        grid=(tl.n,),
        in_specs=[tl.rows(D_MODEL), tl.mod_spec(sc), tl.mod_spec(sh)] + [_const_spec(w) for w in weights]
                 + [tl.table_spec(cos), tl.table_spec(sin)],
        out_specs=[tl.rows(w) for w, _ in out_w],
        out_shape=[jax.ShapeDtypeStruct((tl.t, w), dt) for w, dt in out_w],
        compiler_params=_cparams(("parallel",)),
        name="mla_in_proj",
    )(x, sc, sh, *weights, cos, sin)


def _attn_kernel(q_ref, qn_ref, k_ref, v_ref, o_ref, m_sc, acc_sc, sa_sc, sb_sc, sc_sc, *, t, n_sub, hp):
    i = pl.program_id(2)
    m_sc[...] = jnp.full(m_sc.shape, NEG_INF, F32)
    acc_sc[...] = jnp.zeros(acc_sc.shape, F32)
    rs = t // n_sub

    def scores(j, s_ref, qr=q_ref):
        for h in range(hp):
            hc = slice(h * HEAD_W, (h + 1) * HEAD_W)
            k = k_ref[0, pl.ds(pl.multiple_of(j * t, t), t), hc]
            s_ref[h] = lax.dot_general(qr[0, :, hc], k, (((1,), (1,)), ((), ())), preferred_element_type=F32)

    def fold(j, s_ref, diagonal):
        for h in range(hp):
            v = v_ref[0, pl.ds(pl.multiple_of(j * t, t), t), h * HEAD_W:(h + 1) * HEAD_W]
            for u in range(n_sub):
                rows = slice(u * rs, (u + 1) * rs)
                s = s_ref[h, rows, :]
                if diagonal:
                    qpos = u * rs + lax.broadcasted_iota(jnp.int32, (rs, 1), 0)
                    kpos = lax.broadcasted_iota(jnp.int32, (1, t), 1)
                    s = jnp.where((kpos >> LOG2_CHUNK) <= (qpos >> LOG2_CHUNK), s, NEG_INF)
                m_old = m_sc[h, rows, :]
                m_new = jnp.maximum(m_old, jnp.max(s, axis=1, keepdims=True))
                alpha = jnp.exp2(m_old - m_new)
                p = jnp.exp2(s - jnp.tile(m_new, (1, t // LANES)))
                pv = jnp.dot(p.astype(BF16), v, preferred_element_type=F32)
                acc_sc[h, rows, :] = jnp.tile(alpha, (1, HEAD_W // LANES)) * acc_sc[h, rows, :] + pv
                m_sc[h, rows, :] = m_new

    def next_tile_scores():
        scores(0, sc_sc, qn_ref)

    @pl.when(i == 0)
    def _():
        scores(0, sc_sc)
        fold(0, sc_sc, True)
        next_tile_scores()

    @pl.when(i == 1)
    def _():
        scores(1, sb_sc)
        fold(0, sc_sc, False)
        next_tile_scores()
        fold(1, sb_sc, True)

    @pl.when(i >= 2)
    def _():
        scores(1, sb_sc)
        fold(0, sc_sc, False)
        scores(2, sa_sc)
        fold(1, sb_sc, False)

        def fold_pair(j):
            scores(j + 1, sb_sc)
            fold(j, sa_sc, False)
            scores(j + 2, sa_sc)
            fold(j + 1, sb_sc, False)

        def quad_step(qq, carry):
            fold_pair(2 + 4 * qq)
            fold_pair(4 + 4 * qq)
            return carry

        n_pair = i // 2 - 1
        lax.fori_loop(0, n_pair // 2, quad_step, 0)

        @pl.when(n_pair % 2 == 1)
        def _():
            fold_pair(2 * n_pair)

        @pl.when(i % 2 == 0)
        def _():
            next_tile_scores()
            fold(i, sa_sc, True)

        @pl.when(i % 2 == 1)
        def _():
            scores(i, sb_sc)
            fold(i - 1, sa_sc, False)
            next_tile_scores()
            fold(i, sb_sc, True)

    for h in range(hp):
        acc = acc_sc[h]
        o_ref[0, :, h * MLA_V:(h + 1) * MLA_V] = (acc[:, :MLA_V] / acc[:, MLA_V:MLA_V + 1]).astype(BF16)


def _attention(qc, kc, vv, b, l, t, hp):
    assert l % t == 0 and t % CHUNK == 0
    n_sub = 2 if t % (2 * LANES) == 0 else 1
    nq = l // t
    kern = functools.partial(_attn_kernel, t=t, n_sub=n_sub, hp=hp)
    scores_buf = pltpu.VMEM((hp, t, t), F32)
    return pl.pallas_call(
        kern,
        grid=(b, MLA_HEADS // hp, nq),
        in_specs=[pl.BlockSpec((1, t, hp * HEAD_W), lambda bi, h, i: (bi, i, h)),
                  pl.BlockSpec((1, t, hp * HEAD_W), lambda bi, h, i: (bi, jnp.minimum(i + 1, nq - 1), h)),
                  pl.BlockSpec((1, l, hp * HEAD_W), lambda bi, h, i: (bi, 0, h)),
                  pl.BlockSpec((1, l, hp * HEAD_W), lambda bi, h, i: (bi, 0, h))],
        out_specs=pl.BlockSpec((1, t, hp * MLA_V), lambda bi, h, i: (bi, i, h)),
        out_shape=jax.ShapeDtypeStruct((b, l, MLA_HEADS * MLA_V), BF16),
        scratch_shapes=[pltpu.VMEM((hp, t, LANES), F32), pltpu.VMEM((hp, t, HEAD_W), F32),
                        scores_buf, scores_buf, scores_buf],
        compiler_params=_cparams(("parallel", "parallel", "arbitrary"), ATTN_VMEM_LIMIT),
        name="mla_attention",
    )(qc.reshape(b, l, MLA_HEADS * HEAD_W), qc.reshape(b, l, MLA_HEADS * HEAD_W),
      kc.reshape(b, l, MLA_HEADS * HEAD_W), vv.reshape(b, l, MLA_HEADS * HEAD_W))


def _decode_attn_kernel(q_ref, lnew_ref, knew_ref, lpast_ref, kpast_ref, wukt_ref, wuv_ref, o_ref, qa_sc,
                        *, l, pos0, past):
    nope_w = KV_LORA
    for h in range(MLA_HEADS):
        rows = slice(h * l, (h + 1) * l)
        qn = q_ref[0, :, h * HEAD_W:h * HEAD_W + MLA_NOPE]
        qa_sc[rows, :nope_w] = jnp.dot(qn, wukt_ref[h], preferred_element_type=F32).astype(BF16)
        qa_sc[rows, nope_w:] = q_ref[0, :, h * HEAD_W + MLA_NOPE:(h + 1) * HEAD_W]
    q_lat = qa_sc[:, :nope_w]
    q_pe = qa_sc[:, nope_w:nope_w + MLA_ROPE]
    nt = (((1,), (1,)), ((), ()))
    lp = lpast_ref[0].astype(BF16)
    kp = kpast_ref[0].astype(BF16)
    ln = lnew_ref[0].astype(BF16)
    kn = knew_ref[0][:, :MLA_ROPE].astype(BF16)
    s_past = (lax.dot_general(q_lat, lp, nt, preferred_element_type=F32)
              + lax.dot_general(q_pe, kp, nt, preferred_element_type=F32))
    s_new = (lax.dot_general(q_lat, ln, nt, preferred_element_type=F32)
             + lax.dot_general(q_pe, kn, nt, preferred_element_type=F32))
    n_rows = MLA_HEADS * l
    qpos = pos0 + lax.rem(lax.broadcasted_iota(jnp.int32, (n_rows, 1), 0), l)
    kpos_new = pos0 + lax.broadcasted_iota(jnp.int32, (1, l), 1)
    s_new = jnp.where((kpos_new >> LOG2_CHUNK) <= (qpos >> LOG2_CHUNK), s_new, NEG_INF)
    kpos_past = lax.broadcasted_iota(jnp.int32, (1, past), 1)
    s_past = jnp.where((kpos_past >> LOG2_CHUNK) <= (qpos >> LOG2_CHUNK), s_past, NEG_INF)
    m = jnp.maximum(jnp.max(s_past, axis=1, keepdims=True), jnp.max(s_new, axis=1, keepdims=True))
    p_past = jnp.exp2(s_past - m)
    p_new = jnp.exp2(s_new - m)
    denom = jnp.sum(p_past, axis=1, keepdims=True) + jnp.sum(p_new, axis=1, keepdims=True)
    ctx = (jnp.dot(p_past.astype(BF16), lp, preferred_element_type=F32)
           + jnp.dot(p_new.astype(BF16), ln, preferred_element_type=F32)) / denom
    ctx = ctx.astype(BF16)
    for h in range(MLA_HEADS):
        o_ref[0, :, h * MLA_V:(h + 1) * MLA_V] = jnp.dot(
            ctx[h * l:(h + 1) * l], wuv_ref[h], preferred_element_type=F32).astype(BF16)


def _decode_attention(qc, latent_new, krp_new, past_latent, past_k_rope, w_uk_t, w_uv, b, l, pos0):
    past = past_latent.shape[1]
    kern = functools.partial(_decode_attn_kernel, l=l, pos0=pos0, past=past)
    per_b = lambda shape: pl.BlockSpec((1,) + shape, lambda bi: (bi, 0, 0))
    return pl.pallas_call(
        kern,
        grid=(b,),
        in_specs=[per_b((l, MLA_HEADS * HEAD_W)), per_b((l, KV_LORA)), per_b((l, LANES)),
                  per_b((past, KV_LORA)), per_b((past, MLA_ROPE)), _const_spec(w_uk_t), _const_spec(w_uv)],
        out_specs=per_b((l, MLA_HEADS * MLA_V)),
        out_shape=jax.ShapeDtypeStruct((b, l, MLA_HEADS * MLA_V), BF16),
        scratch_shapes=[pltpu.VMEM((MLA_HEADS * l, KV_LORA + LANES), BF16)],
        compiler_params=_cparams(("parallel",)),
        name="mla_decode_attention",
    )(qc.reshape(b, l, MLA_HEADS * HEAD_W), latent_new.reshape(b, l, KV_LORA), krp_new.reshape(b, l, LANES),
      past_latent, past_k_rope, w_uk_t, w_uv)


def _rope_tables(pos0, length, half):
    inv = ROPE_THETA ** (-np.arange(half, dtype=np.float64) / half)
    ang = (pos0 + np.arange(length, dtype=np.float64))[:, None] * inv[None, :]
    return np.cos(ang), np.sin(ang)


def _mla_rope_tables(pos0, length):
    cos, sin = _rope_tables(pos0, length, MLA_ROPE // 2)
    z = np.zeros((length, LANES - MLA_ROPE))
    return (jnp.asarray(np.concatenate([cos, cos, z], axis=1), F32),
            jnp.asarray(np.concatenate([-sin, sin, z], axis=1), F32))


def _swap_halves(w):
    half = w.shape[-1] // 2
    return jnp.concatenate([w[..., half:], w[..., :half]], axis=-1)


def _pad_lanes(w):
    return jnp.pad(w, [(0, 0)] * (w.ndim - 1) + [(0, LANES - w.shape[-1])])


def _prep_weights(w_ret_in, w_ret_out, w_dq, q_norm_g, w_uq, w_mla_out, w_dkv, kv_norm_g, w_ukv,
                  w_route_group, b_route_group, w_route_expert, b_route_expert,
                  w_expert_gate_up, w_expert_down):
    p = {}
    p["w_ret_in"] = [w_ret_in[i].astype(BF16) for i in range(N_A)]
    p["w_ret_out"] = [w_ret_out[i].astype(BF16) for i in range(N_A)]
    p["w_dq"] = [w_dq[j].astype(BF16) for j in range(DEPTH - N_A)]
    p["q_norm_g"] = [q_norm_g[j][None, :] for j in range(DEPTH - N_A)]
    w_uq_ext = []
    for j in range(DEPTH - N_A):
        wq = w_uq[j].reshape(Q_LORA, MLA_HEADS, MLA_NOPE + MLA_ROPE)
        nope = wq[:, :, :MLA_NOPE].reshape(Q_LORA, MLA_HEADS * LANES)
        pe = wq[:, :, MLA_NOPE:]
        w_uq_ext.append(jnp.concatenate(
            [nope, _pad_lanes(pe).reshape(Q_LORA, MLA_HEADS * LANES),
             _pad_lanes(_swap_halves(pe)).reshape(Q_LORA, MLA_HEADS * LANES)], axis=1).astype(BF16))
    p["w_uq_ext"] = w_uq_ext
    p["w_mla_out"] = [w_mla_out[j].astype(BF16) for j in range(DEPTH - N_A)]
    kr = w_dkv[:, KV_LORA:]
    p["w_dkv_ext"] = jnp.concatenate([w_dkv[:, :KV_LORA], _pad_lanes(kr), _pad_lanes(_swap_halves(kr))],
                                     axis=1).astype(BF16)
    p["kv_norm_g"] = kv_norm_g[None, :]
    w_ukv4 = w_ukv.reshape(KV_LORA, MLA_HEADS, 2, MLA_NOPE)
    p["w_ukv_r"] = w_ukv4.transpose(0, 2, 1, 3).reshape(KV_LORA, 2 * MLA_HEADS * MLA_NOPE).astype(BF16)
    p["w_uk_t"] = w_ukv4[:, :, 0, :].transpose(1, 2, 0).astype(BF16)
    p["w_uv"] = w_ukv4[:, :, 1, :].transpose(1, 0, 2).astype(BF16)
    wr = jnp.concatenate([w_route_group, w_route_expert], axis=-1)
    wr = jnp.pad(wr, ((0, 0), (0, 0), (0, ROUTE_W - wr.shape[-1])))
    wr_hi = wr.astype(BF16)
    wr_lo = (wr - wr_hi.astype(F32)).astype(BF16)
    br = jnp.concatenate([b_route_group, b_route_expert], axis=-1)
    br = jnp.pad(br, ((0, 0), (0, ROUTE_W - br.shape[-1])))
    wr_cat = jnp.concatenate([wr_hi, wr_lo], axis=-1)
    p["wr"] = [wr_cat[l] for l in range(DEPTH)]
    p["br"] = [br[l][None, :] for l in range(DEPTH)]
    p["w_gu"] = w_expert_gate_up.reshape(DEPTH * N_EXPERTS, D_MODEL, 2 * EXPERT_FF)
    p["w_dn"] = w_expert_down.reshape(DEPTH * N_EXPERTS, EXPERT_FF, D_MODEL)
    return p


def _trunk(x3, mod, pos0, ret_s0, past_latent, past_k_rope, ret_chunk, ret_block, ln_g, ln_b, p):
    b, l, _ = x3.shape
    tl = _Tiles(b, l)
    x = x3.reshape(tl.t, D_MODEL)
    new_ret = []
    latent_new = k_rope_new = None
    for layer in range(DEPTH):
        sh1, sc1, g1, sh2, sc2, g2 = [tl.mod(m) for m in jnp.split(mod[layer], 6, axis=-1)]
        lng = [ln_g[layer, s][None, :] for s in range(2)]
        lnb = [ln_b[layer, s][None, :] for s in range(2)]
        if layer < N_A:
            cos, sin = [jnp.asarray(tab, F32) for tab in _rope_tables(pos0, l, RET_DK // 2)]
            q, k, v, sg = _ret_in(tl, x, sc1, sh1, p["w_ret_in"][layer], tl.table(cos), tl.table(sin))
            a, s_new = _retention(b, l, q, k, v, sg, ret_s0[layer], ret_chunk, ret_block)
            a = a.reshape(tl.t, RET_HEADS * RET_DV)
            new_ret.append(s_new)
            w_out = p["w_ret_out"][layer]
            name = "ret_out_ln_route"
        else:
            j = layer - N_A
            cos, sin = _mla_rope_tables(pos0, l)
            cos, sin = tl.table(cos), tl.table(sin)
            assert layer == N_A == DEPTH - 1
            qscale = float((MLA_NOPE + MLA_ROPE) ** -0.5 * math.log2(math.e))
            if past_latent is None:
                assert pos0 == 0
                latent_new, k_rope_new, qc, kc, vv = _mla_in(tl, x, sc1, sh1, p, j, cos, sin, qscale, True)
                a = _attention(qc, kc, vv, b, l, ATTN_TILE, 2)
            else:
                latent_new, k_rope_new, qc, krp = _mla_in(tl, x, sc1, sh1, p, j, cos, sin, qscale, False)
                a = _decode_attention(qc, latent_new, krp, past_latent, past_k_rope, p["w_uk_t"], p["w_uv"],
                                      b, l, pos0)
            a = a.reshape(tl.t, MLA_HEADS * MLA_V)
            w_out = p["w_mla_out"][j]
            name = "mla_out_ln_route"
        x1, gt, gi = _mix_out(tl, a, w_out, x, g1, lng[0], lnb[0], sc2, sh2,
                              p["wr"][layer], p["br"][layer], name)
        x = _moe_ffn(tl, x1, sc2, sh2, gt, gi.reshape(tl.t), g2, lng[1], lnb[1],
                     p["w_gu"], p["w_dn"], layer)
    return (x.reshape(b, l, D_MODEL), jnp.stack(new_ret), latent_new.reshape(b, l, KV_LORA),
            k_rope_new.reshape(b, l, MLA_ROPE))


def kernel(x_prompt, x_sample, state_retention, cache_kv_latent, cache_k_rope, c_prompt, c_sample,
           w_ada, b_ada, ln_g, ln_b, w_ret_in, w_ret_out, w_dq, q_norm_g, w_uq, w_mla_out,
           w_dkv, kv_norm_g, w_ukv, w_route_group, b_route_group, w_route_expert, b_route_expert,
           w_expert_gate_up, w_expert_down):
    bp, lp, _ = x_prompt.shape
    bs, ls, _ = x_sample.shape
    p = _prep_weights(w_ret_in, w_ret_out, w_dq, q_norm_g, w_uq, w_mla_out, w_dkv, kv_norm_g, w_ukv,
                      w_route_group, b_route_group, w_route_expert, b_route_expert,
                      w_expert_gate_up, w_expert_down)
    n_seq = bp + bs
    n_rows = -(-n_seq // 8) * 8
    c_all = jnp.concatenate([c_prompt, c_sample, jnp.zeros((n_rows - n_seq, D_MODEL), F32)], axis=0)
    mod = _ada(c_all, w_ada, b_ada)
    s0_p = jnp.zeros((N_A, bp, RET_HEADS, RET_DK, RET_DV), F32)
    ret_chunk_p = RET_CHUNK if lp % RET_CHUNK == 0 else CHUNK
    y_p, st_p, lat_p, kr_p = _trunk(x_prompt, mod[:, :bp], 0, s0_p, None, None,
                                    ret_chunk_p, max(ret_chunk_p, min(lp, TOKEN_TILE)), ln_g, ln_b, p)
    past = cache_kv_latent.shape[1]
    y_s, st_s, lat_s, kr_s = _trunk(x_sample, mod[:, bp:n_seq], past, state_retention,
                                    cache_kv_latent, cache_k_rope, ls, ls, ln_g, ln_b, p)
    return (y_p, y_s, st_p, st_s, lat_p, kr_p, lat_s, kr_s)
```

```python
import functools
import math

import numpy as np
import jax
import jax.numpy as jnp
from jax import lax
from jax.experimental import pallas as pl
from jax.experimental.pallas import tpu as pltpu

F32 = jnp.float32
BF16 = jnp.bfloat16

D_MODEL = 1024
DEPTH = 2
CHUNK = 64
LOG2_CHUNK = 6
N_A = DEPTH // 2
RET_HEADS = 4
RET_DK = 256
RET_DV = 512
MLA_HEADS = 8
MLA_NOPE = 128
MLA_ROPE = 64
MLA_V = 128
Q_LORA = 384
KV_LORA = 256
ROPE_THETA = 10000.0
N_GROUPS = 4
EXP_PER_GROUP = 4
N_EXPERTS = N_GROUPS * EXP_PER_GROUP
N_PAIRS = EXP_PER_GROUP * (EXP_PER_GROUP - 1) // 2
N_BUCKETS = N_GROUPS * N_PAIRS
PAIR_A = (0, 0, 0, 1, 1, 2)
PAIR_B = (1, 2, 3, 2, 3, 3)
EXPERT_FF = 512
LN_EPS = 1e-5
RMS_EPS = 1e-6
DN_ALPHA = (2 * DEPTH) ** 0.25
NEG_INF = -1e30

LANES = 128
ROUTE_W = LANES
TOKEN_TILE = 512
EXPERT_TILE = 256
RET_CHUNK = 256
RET_BLOCK = 1024
ATTN_TILE = 512
HEAD_W = 2 * LANES
VMEM_LIMIT = 56 * 1024 * 1024
ATTN_VMEM_LIMIT = 56 * 1024 * 1024


def _cparams(sem, vmem=VMEM_LIMIT):
    return pltpu.CompilerParams(dimension_semantics=sem, vmem_limit_bytes=vmem)


def _silu(x):
    return x * jax.nn.sigmoid(x)


def _layer_norm(z, g, b):
    mu = jnp.mean(z, axis=-1, keepdims=True)
    zc = z - mu
    var = jnp.mean(zc * zc, axis=-1, keepdims=True)
    return zc * lax.rsqrt(var + LN_EPS) * g + b


def _ada_kernel(c_ref, w_ref, b_ref, o_ref):
    s = _silu(c_ref[...]).astype(BF16)
    o_ref[0] = jnp.dot(s, w_ref[0].astype(BF16), preferred_element_type=F32) + b_ref[0]


def _ada(c_all, w_ada, b_ada):
    r = c_all.shape[0]
    n = w_ada.shape[-1]
    tn = 1536
    return pl.pallas_call(
        _ada_kernel,
        grid=(DEPTH, n // tn),
        in_specs=[pl.BlockSpec((r, D_MODEL), lambda l, j: (0, 0)),
                  pl.BlockSpec((1, D_MODEL, tn), lambda l, j: (l, 0, j)),
                  pl.BlockSpec((1, 1, tn), lambda l, j: (l, 0, j))],
        out_specs=pl.BlockSpec((1, r, tn), lambda l, j: (l, 0, j)),
        out_shape=jax.ShapeDtypeStruct((DEPTH, r, n), F32),
        compiler_params=_cparams(("parallel", "parallel")),
        name="ada_mod",
    )(c_all, w_ada, b_ada.reshape(DEPTH, 1, n))


class _Tiles:
    def __init__(self, b, l):
        self.b, self.l, self.t = b, l, b * l
        self.per_token = (l % TOKEN_TILE) != 0
        self.tm = self.t if self.per_token else TOKEN_TILE
        self.n = self.t // self.tm
        self.tiles_per_seq = 1 if self.per_token else l // self.tm

    def mod(self, m):
        if self.per_token:
            return jnp.repeat(m, self.l, axis=0)[None]
        return m[:, None, :]

    def mod_spec(self, arr):
        tps = self.tiles_per_seq
        return pl.BlockSpec((1,) + arr.shape[1:], lambda i: (i // tps, 0, 0))

    def table(self, tab):
        return jnp.tile(tab, (self.b, 1)) if self.per_token else tab

    def table_spec(self, tab):
        nt = tab.shape[0] // self.tm
        return pl.BlockSpec((self.tm, tab.shape[1]), lambda i: (i % nt, 0))

    def rows(self, w):
        return pl.BlockSpec((self.tm, w), lambda i: (i, 0))


def _const_spec(arr):
    nd = arr.ndim
    return pl.BlockSpec(arr.shape, lambda *_: (0,) * nd)


def _ret_in_kernel(x_ref, sc_ref, sh_ref, w_ref, cos_ref, sin_ref, q_ref, k_ref, v_ref, sg_ref):
    h = (x_ref[...] * (1.0 + sc_ref[0]) + sh_ref[0]).astype(BF16)
    cos = cos_ref[...]
    sin = sin_ref[...]
    qk_w = RET_HEADS * RET_DK
    half = RET_DK // 2

    def rope_store(r, out_ref, scale):
        for hh in range(RET_HEADS):
            a = r[:, hh * RET_DK:hh * RET_DK + half]
            b = r[:, hh * RET_DK + half:(hh + 1) * RET_DK]
            out_ref[:, hh * RET_DK:hh * RET_DK + half] = ((a * cos - b * sin) * scale).astype(BF16)
            out_ref[:, hh * RET_DK + half:(hh + 1) * RET_DK] = ((a * sin + b * cos) * scale).astype(BF16)

    r = jnp.dot(h, w_ref[:, 0:qk_w], preferred_element_type=F32)
    rope_store(r, q_ref, 1.0)
    r = jnp.dot(h, w_ref[:, qk_w:2 * qk_w], preferred_element_type=F32)
    rope_store(r, k_ref, RET_DK ** -0.5)
    for c in range(2):
        lo = 2 * qk_w + c * qk_w
        v_ref[:, c * qk_w:(c + 1) * qk_w] = jnp.dot(
            h, w_ref[:, lo:lo + qk_w], preferred_element_type=F32).astype(BF16)
    for c in range(2):
        lo = 4 * qk_w + c * qk_w
        g = jnp.dot(h, w_ref[:, lo:lo + qk_w], preferred_element_type=F32)
        sg_ref[:, c * qk_w:(c + 1) * qk_w] = _silu(g).astype(BF16)


def _ret_in(tl, x, sc, sh, w_in, cos, sin):
    qk_w, v_w = RET_HEADS * RET_DK, RET_HEADS * RET_DV
    return pl.pallas_call(
        _ret_in_kernel,
        grid=(tl.n,),
        in_specs=[tl.rows(D_MODEL), tl.mod_spec(sc), tl.mod_spec(sh), _const_spec(w_in),
                  tl.table_spec(cos), tl.table_spec(sin)],
        out_specs=[tl.rows(qk_w), tl.rows(qk_w), tl.rows(v_w), tl.rows(v_w)],
        out_shape=[jax.ShapeDtypeStruct((tl.t, qk_w), BF16), jax.ShapeDtypeStruct((tl.t, qk_w), BF16),
                   jax.ShapeDtypeStruct((tl.t, v_w), BF16), jax.ShapeDtypeStruct((tl.t, v_w), BF16)],
        compiler_params=_cparams(("parallel",)),
        name="ret_in_proj",
    )(x, sc, sh, w_in, cos, sin)


def _retention_tables(chunk):
    lg = np.log1p(-np.exp2(-5.0 - np.arange(RET_HEADS, dtype=np.float64)))
    idx = np.arange(chunk, dtype=np.float64)
    diff = idx[:, None] - idx[None, :]
    dmask = np.where(diff >= 0, np.exp(lg[:, None, None] * np.maximum(diff, 0.0)), 0.0)
    xi = np.exp(lg[:, None] * (idx[None, :] + 1.0))[:, :, None]
    zeta = np.exp(lg[:, None] * (chunk - 1.0 - idx[None, :]))[:, :, None]
    cdec = np.exp(lg * chunk)
    return (jnp.asarray(dmask, F32), jnp.asarray(xi, F32), jnp.asarray(zeta, F32),
            [float(np.float32(c)) for c in cdec])


def _retention_kernel(q_ref, k_ref, v_ref, sg_ref, s0_ref, dm_ref, xi_ref, zeta_ref,
                      o_ref, sout_ref, s_sc, *, chunk, n_chunks, cdec):
    j = pl.program_id(1)

    @pl.when(j == 0)
    def _():
        s_sc[...] = s0_ref[0]

    heads = range(RET_HEADS)
    kcols = [slice(hh * RET_DK, (hh + 1) * RET_DK) for hh in heads]
    vcols = [slice(hh * RET_DV, (hh + 1) * RET_DV) for hh in heads]
    nt = (((1,), (1,)), ((), ()))

    def epilogue(rows, o):
        for hh in heads:
            mu = jnp.mean(o[hh], axis=-1, keepdims=True)
            oc = o[hh] - mu
            var = jnp.mean(oc * oc, axis=-1, keepdims=True)
            on = oc * lax.rsqrt(var + LN_EPS)
            o_ref[0, rows, vcols[hh]] = (sg_ref[0, rows, vcols[hh]].astype(F32) * on).astype(BF16)

    for c in range(n_chunks):
        rows = slice(c * chunk, (c + 1) * chunk)
        q = [q_ref[0, rows, kcols[hh]] for hh in heads]
        k = [k_ref[0, rows, kcols[hh]] for hh in heads]
        v = [v_ref[0, rows, vcols[hh]] for hh in heads]
        sc = [(lax.dot_general(q[hh], k[hh], nt, preferred_element_type=F32) * dm_ref[hh]).astype(BF16)
              for hh in heads]
        cross = [jnp.dot(q[hh], s_sc[hh].astype(BF16), preferred_element_type=F32) * xi_ref[hh] for hh in heads]
        o = [jnp.dot(sc[hh], v[hh], preferred_element_type=F32) + cross[hh] for hh in heads]
        for hh in heads:
            kz_t = (k[hh].astype(F32) * zeta_ref[hh]).T.astype(BF16)
            s_sc[hh] = s_sc[hh] * cdec[hh] + jnp.dot(kz_t, v[hh], preferred_element_type=F32)
        epilogue(rows, o)

    @pl.when(j == pl.num_programs(1) - 1)
    def _():
        sout_ref[0] = s_sc[...]


def _retention(b, l, q, k, v, sg, s0, chunk, block):
    qk_w, v_w = RET_HEADS * RET_DK, RET_HEADS * RET_DV
    dmask, xi, zeta, cdec = _retention_tables(chunk)
    kern = functools.partial(_retention_kernel, chunk=chunk, n_chunks=block // chunk, cdec=cdec)
    seq = lambda w: pl.BlockSpec((1, block, w), lambda bi, j: (bi, j, 0))
    st = pl.BlockSpec((1, RET_HEADS, RET_DK, RET_DV), lambda bi, j: (bi, 0, 0, 0))
    return pl.pallas_call(
        kern,
        grid=(b, l // block),
        in_specs=[seq(qk_w), seq(qk_w), seq(v_w), seq(v_w), st,
                  _const_spec(dmask), _const_spec(xi), _const_spec(zeta)],
        out_specs=[seq(v_w), st],
        out_shape=[jax.ShapeDtypeStruct((b, l, v_w), BF16),
                   jax.ShapeDtypeStruct((b, RET_HEADS, RET_DK, RET_DV), F32)],
        scratch_shapes=[pltpu.VMEM((RET_HEADS, RET_DK, RET_DV), F32)],
        compiler_params=_cparams(("parallel", "arbitrary")),
        name="retention_scan",
    )(q.reshape(b, l, qk_w), k.reshape(b, l, qk_w), v.reshape(b, l, v_w), sg.reshape(b, l, v_w),
      s0, dmask, xi, zeta)


def _route_rows(lt):
    g = [lt[i:i + 1] for i in range(N_GROUPS)]
    m = jnp.maximum(jnp.maximum(g[0], g[1]), jnp.maximum(g[2], g[3]))
    gi = jnp.where(g[0] == m, 0, jnp.where(g[1] == m, 1, jnp.where(g[2] == m, 2, 3))).astype(jnp.int32)
    denom = jnp.exp(g[0] - m) + jnp.exp(g[1] - m) + jnp.exp(g[2] - m) + jnp.exp(g[3] - m)
    p_group = 1.0 / denom
    le = []
    for e in range(EXP_PER_GROUP):
        r = [lt[N_GROUPS + gg * EXP_PER_GROUP + e:N_GROUPS + gg * EXP_PER_GROUP + e + 1]
             for gg in range(N_GROUPS)]
        le.append(jnp.where(gi == 0, r[0], jnp.where(gi == 1, r[1], jnp.where(gi == 2, r[2], r[3]))))
    me = jnp.maximum(jnp.maximum(le[0], le[1]), jnp.maximum(le[2], le[3]))
    ex = [jnp.exp(x - me) for x in le]

    def first_argmax(vals):
        mx = jnp.maximum(jnp.maximum(vals[0], vals[1]), jnp.maximum(vals[2], vals[3]))
        ix = jnp.where(vals[0] == mx, 0, jnp.where(vals[1] == mx, 1, jnp.where(vals[2] == mx, 2, 3)))
        return mx, ix.astype(jnp.int32)

    e1, i1 = first_argmax(ex)
    ex2 = [jnp.where(i1 == e, -1.0, ex[e]) for e in range(EXP_PER_GROUP)]
    e2, i2 = first_argmax(ex2)
    tot = e1 + e2
    w1 = e1 / tot * p_group
    w2 = e2 / tot * p_group
    first_low = i1 < i2
    a = jnp.where(first_low, i1, i2)
    b = jnp.where(first_low, i2, i1)
    pair = jnp.where(a == 0, b - 1, jnp.where(a == 1, b + 1, N_PAIRS - 1))
    bucket = gi * N_PAIRS + pair
    return bucket, gi * EXP_PER_GROUP + a, gi * EXP_PER_GROUP + b, \
        jnp.where(first_low, w1, w2), jnp.where(first_low, w2, w1)


def _mix_out_kernel(a_ref, w_ref, x_ref, g1_ref, lng_ref, lnb_ref, sc2_ref, sh2_ref,
                    wr_ref, br_ref, x1_ref, gt_ref, gi_ref, *, dense_gates):
    y = jnp.dot(a_ref[...], w_ref[...], preferred_element_type=F32)
    x1 = _layer_norm(DN_ALPHA * x_ref[...] + g1_ref[0] * y, lng_ref[...], lnb_ref[...])
    x1_ref[...] = x1
    h2 = x1 * (1.0 + sc2_ref[0]) + sh2_ref[0]
    hi = h2.astype(BF16)
    lo = (h2 - hi.astype(F32)).astype(BF16)
    hw = jnp.dot(hi, wr_ref[...], preferred_element_type=F32)
    logits = (hw[:, :ROUTE_W] + (hw[:, ROUTE_W:] + jnp.dot(lo, wr_ref[:, :ROUTE_W], preferred_element_type=F32))
              + br_ref[...])
    lt = logits.T
    bucket, exp_a, exp_b, gate_a, gate_b = _route_rows(lt)
    gi_ref[0] = bucket
    tm = lt.shape[1]
    if dense_gates:
        row = lax.broadcasted_iota(jnp.int32, (ROUTE_W, tm), 0)
        gt = jnp.where(row == exp_a, gate_a, jnp.where(row == exp_b, gate_b, 0.0))
    else:
        row = lax.broadcasted_iota(jnp.int32, (8, tm), 0)
        g8 = jnp.where(row == 0, gate_a, jnp.where(row == 1, gate_b, 0.0))
        gt = jnp.concatenate([g8, jnp.zeros((ROUTE_W - 8, tm), F32)], axis=0)
    gt_ref[...] = gt.T


def _mix_out(tl, a, w, x, g1, lng, lnb, sc2, sh2, wr, br, name):
    kd = a.shape[1]
    return pl.pallas_call(
        functools.partial(_mix_out_kernel, dense_gates=tl.per_token),
        grid=(tl.n,),
        in_specs=[tl.rows(kd), _const_spec(w), tl.rows(D_MODEL), tl.mod_spec(g1),
                  _const_spec(lng), _const_spec(lnb), tl.mod_spec(sc2), tl.mod_spec(sh2),
                  _const_spec(wr), _const_spec(br)],
        out_specs=[tl.rows(D_MODEL), tl.rows(ROUTE_W),
                   pl.BlockSpec((1, 1, tl.tm), lambda i: (i, 0, 0))],
        out_shape=[jax.ShapeDtypeStruct((tl.t, D_MODEL), F32),
                   jax.ShapeDtypeStruct((tl.t, ROUTE_W), F32),
                   jax.ShapeDtypeStruct((tl.n, 1, tl.tm), jnp.int32)],
        compiler_params=_cparams(("parallel",)),
        name=name,
    )(a, w, x, g1, lng, lnb, sc2, sh2, wr, br)


TOK_SUB = D_MODEL // LANES
DMA_PRIORITIES = 2


def _moe_sort_kernel(idx_ref, x1_ref, sc2_ref, sh2_ref, gt_ref, tok_ref, gs_ref, buf0, buf1, gs_sc, sem, gsem,
                     *, tm):
    i = pl.program_id(0)
    n = pl.num_programs(0)
    bufs = (buf0, buf1)
    base = i * tm

    def wait_slot(sl):
        pltpu.make_async_copy(bufs[sl], tok_ref.at[pl.ds(0, tm * TOK_SUB)], sem.at[sl]).wait()

    def step(sl):
        @pl.when(i >= 2)
        def _():
            wait_slot(sl)

        h2 = x1_ref[...] * (1.0 + sc2_ref[0]) + sh2_ref[0]
        for s in range(TOK_SUB):
            bufs[sl][pl.ds(s, tm, stride=TOK_SUB), :] = h2[:, s * LANES:(s + 1) * LANES]
        for r in range(tm):
            d = idx_ref[base + r]
            pltpu.make_async_copy(bufs[sl].at[pl.ds(r * TOK_SUB, TOK_SUB)],
                                  tok_ref.at[pl.ds(pl.multiple_of(d * TOK_SUB, TOK_SUB), TOK_SUB)],
                                  sem.at[sl]).start(priority=r % DMA_PRIORITIES)
            gs_sc[pl.ds(d, 1), :] = gt_ref[r:r + 1, :]

        @pl.when(i == n - 1)
        def _():
            gates_out = pltpu.make_async_copy(gs_sc, gs_ref, gsem)
            gates_out.start()

            @pl.when(n >= 2)
            def _():
                wait_slot(1 - sl)
            wait_slot(sl)
            gates_out.wait()

    for sl in range(len(bufs)):
        @pl.when(i % len(bufs) == sl)
        def _():
            step(sl)


def _moe_sort(tl, x1, sc2, sh2, gt, dst_idx, n_rows):
    tm = tl.tm
    n_steps = dst_idx.shape[0] // tm
    last = tl.n - 1
    tps = tl.tiles_per_seq
    kern = functools.partial(_moe_sort_kernel, tm=tm)
    rows = lambda w: pl.BlockSpec((tm, w), lambda i, idx: (jnp.minimum(i, last), 0))
    mod = lambda arr: pl.BlockSpec((1,) + arr.shape[1:], lambda i, idx: (jnp.minimum(i, last) // tps, 0, 0))
    return pl.pallas_call(
        kern,
        grid_spec=pltpu.PrefetchScalarGridSpec(
            num_scalar_prefetch=1,
            grid=(n_steps,),
            in_specs=[rows(D_MODEL), mod(sc2), mod(sh2), rows(ROUTE_W)],
            out_specs=[pl.BlockSpec(memory_space=pl.ANY), pl.BlockSpec(memory_space=pl.ANY)],
            scratch_shapes=[pltpu.VMEM((tm * TOK_SUB, LANES), F32), pltpu.VMEM((tm * TOK_SUB, LANES), F32),
                            pltpu.VMEM((n_rows, ROUTE_W), F32),
                            pltpu.SemaphoreType.DMA((2,)), pltpu.SemaphoreType.DMA(())]),
        out_shape=[jax.ShapeDtypeStruct((n_rows * TOK_SUB, LANES), F32),
                   jax.ShapeDtypeStruct((n_rows, ROUTE_W), F32)],
        compiler_params=_cparams(("arbitrary",)),
        name="moe_sort_rows",
    )(dst_idx, x1, sc2, sh2, gt)


def _expert_ffn(hb, wgu, wdn):
    au = jnp.dot(hb, wgu, preferred_element_type=F32)
    act = (_silu(au[:, :EXPERT_FF]) * au[:, EXPERT_FF:]).astype(BF16)
    return jnp.dot(act, wdn, preferred_element_type=F32)


def _moe_kernel(ea_ref, eb_ref, new_ref, nv_ref, tok_ref, gs_ref, wgu_a_ref, wgu_b_ref, wdn_a_ref, wdn_b_ref,
                o_ref, wgu_sc, wdn_sc, *, tme):
    i = pl.program_id(0)
    valid = i < nv_ref[0]

    @pl.when(valid & (new_ref[i] == 1))
    def _():
        wgu_sc[0] = wgu_a_ref[0].astype(BF16)
        wgu_sc[1] = wgu_b_ref[0].astype(BF16)
        wdn_sc[0] = wdn_a_ref[0].astype(BF16)
        wdn_sc[1] = wdn_b_ref[0].astype(BF16)

    @pl.when(valid)
    def _():
        hb = jnp.concatenate([tok_ref[pl.ds(s, tme, stride=TOK_SUB), :] for s in range(TOK_SUB)],
                             axis=1).astype(BF16)
        au = [jnp.dot(hb, wgu_sc[e], preferred_element_type=F32) for e in range(2)]
        act = [(_silu(a[:, :EXPERT_FF]) * a[:, EXPERT_FF:]).astype(BF16) for a in au]
        y = [jnp.dot(act[e], wdn_sc[e], preferred_element_type=F32) for e in range(2)]
        acc = gs_ref[:, 0:1] * y[0] + gs_ref[:, 1:2] * y[1]
        for s in range(TOK_SUB):
            o_ref[pl.ds(s, tme, stride=TOK_SUB), :] = acc[:, s * LANES:(s + 1) * LANES]

    @pl.when(jnp.logical_not(valid))
    def _():
        o_ref[...] = jnp.zeros_like(o_ref)


def _moe_experts(tok, gs, tile_ea, tile_eb, tile_new, n_valid, wgu, wdn, tme):
    n_tiles = gs.shape[0] // tme
    kern = functools.partial(_moe_kernel, tme=tme)
    gu = lambda pick: pl.BlockSpec((1, D_MODEL, 2 * EXPERT_FF),
                                   lambda i, ea, eb, new, nv: (pick(ea, eb)[i], 0, 0))
    dn = lambda pick: pl.BlockSpec((1, EXPERT_FF, D_MODEL),
                                   lambda i, ea, eb, new, nv: (pick(ea, eb)[i], 0, 0))
    first = lambda ea, eb: ea
    second = lambda ea, eb: eb
    return pl.pallas_call(
        kern,
        grid_spec=pltpu.PrefetchScalarGridSpec(
            num_scalar_prefetch=4,
            grid=(n_tiles,),
            in_specs=[pl.BlockSpec((tme * TOK_SUB, LANES), lambda i, ea, eb, new, nv: (i, 0)),
                      pl.BlockSpec((tme, ROUTE_W), lambda i, ea, eb, new, nv: (i, 0)),
                      gu(first), gu(second), dn(first), dn(second)],
            out_specs=pl.BlockSpec((tme * TOK_SUB, LANES), lambda i, ea, eb, new, nv: (i, 0)),
            scratch_shapes=[pltpu.VMEM((2, D_MODEL, 2 * EXPERT_FF), BF16),
                            pltpu.VMEM((2, EXPERT_FF, D_MODEL), BF16)]),
        out_shape=jax.ShapeDtypeStruct(tok.shape, F32),
        compiler_params=_cparams(("arbitrary",)),
        name="moe_experts",
    )(tile_ea, tile_eb, tile_new, n_valid, tok, gs, wgu, wgu, wdn, wdn)


def _moe_dense_kernel(x1_ref, sc2_ref, sh2_ref, gt_ref, g2_ref, lng_ref, lnb_ref, wgu_ref, wdn_ref, o_ref,
                      hb_sc, acc_sc):
    e = pl.program_id(0)

    @pl.when(e == 0)
    def _():
        hb_sc[...] = (x1_ref[...] * (1.0 + sc2_ref[0]) + sh2_ref[0]).astype(BF16)
        acc_sc[...] = jnp.zeros_like(acc_sc)

    lane = lax.broadcasted_iota(jnp.int32, gt_ref.shape, 1)
    gate = jnp.sum(jnp.where(lane == e, gt_ref[...], 0.0), axis=1, keepdims=True)
    acc_sc[...] += gate * _expert_ffn(hb_sc[...], wgu_ref[0].astype(BF16), wdn_ref[0].astype(BF16))

    @pl.when(e == pl.num_programs(0) - 1)
    def _():
        o_ref[...] = _layer_norm(DN_ALPHA * x1_ref[...] + g2_ref[0] * acc_sc[...], lng_ref[...], lnb_ref[...])


def _moe_dense(tl, x1, sc2, sh2, gt, g2, lng, lnb, wgu, wdn, layer):
    t = tl.t
    full = lambda arr: pl.BlockSpec(arr.shape, lambda e: (0,) * arr.ndim)
    return pl.pallas_call(
        _moe_dense_kernel,
        grid=(N_EXPERTS,),
        in_specs=[full(x1), full(sc2), full(sh2), full(gt), full(g2), full(lng), full(lnb),
                  pl.BlockSpec((1, D_MODEL, 2 * EXPERT_FF), lambda e: (layer * N_EXPERTS + e, 0, 0)),
                  pl.BlockSpec((1, EXPERT_FF, D_MODEL), lambda e: (layer * N_EXPERTS + e, 0, 0))],
        out_specs=pl.BlockSpec((t, D_MODEL), lambda e: (0, 0)),
        out_shape=jax.ShapeDtypeStruct((t, D_MODEL), F32),
        scratch_shapes=[pltpu.VMEM((t, D_MODEL), BF16), pltpu.VMEM((t, D_MODEL), F32)],
        compiler_params=_cparams(("arbitrary",)),
        name="moe_dense_ln",
    )(x1, sc2, sh2, gt, g2, lng, lnb, wgu, wdn)


def _ln_res_kernel(idx_ref, fs_ref, x_ref, g_ref, lng_ref, lnb_ref, o_ref, buf0, buf1, buf2, sem, *, tm):
    i = pl.program_id(0)
    n = pl.num_programs(0)
    bufs = (buf0, buf1, buf2)
    n_buf = len(bufs)

    def row_copy(d, sl, row_off):
        return pltpu.make_async_copy(fs_ref.at[pl.ds(pl.multiple_of(d * TOK_SUB, TOK_SUB), TOK_SUB)],
                                     bufs[sl].at[pl.ds(row_off, TOK_SUB)], sem.at[sl])

    def wait_rows(sl):
        pltpu.make_async_copy(fs_ref.at[pl.ds(0, tm * TOK_SUB)], bufs[sl], sem.at[sl]).wait()

    def step_base(k):
        return jnp.minimum(k, n - 1) * tm

    @pl.when(i == 0)
    def _():
        for sl in range(n_buf - 1):
            base = step_base(sl)

            def body(rr, carry):
                for prio in range(DMA_PRIORITIES):
                    r = rr * DMA_PRIORITIES + prio
                    row_copy(idx_ref[base + r], sl, pl.multiple_of(r * TOK_SUB, TOK_SUB)).start(priority=prio)
                return carry

            lax.fori_loop(0, tm // DMA_PRIORITIES, body, 0, unroll=4)

    ahead = step_base(i + n_buf - 1)

    def step(sl):
        wait_rows(sl)
        f = jnp.concatenate([bufs[sl][pl.ds(s, tm, stride=TOK_SUB), :] for s in range(TOK_SUB)], axis=1)
        o_ref[...] = _layer_norm(DN_ALPHA * x_ref[...] + g_ref[0] * f, lng_ref[...], lnb_ref[...])
        for r in range(tm):
            row_copy(idx_ref[ahead + r], (sl + n_buf - 1) % n_buf, r * TOK_SUB).start(
                priority=r % DMA_PRIORITIES)

    for sl in range(n_buf):
        @pl.when(i % n_buf == sl)
        def _():
            step(sl)

            @pl.when(i == n - 1)
            def _():
                for extra in range(1, n_buf):
                    wait_rows((sl + extra) % n_buf)


def _ln_res(tl, x, fs, dest, g2, lng, lnb):
    tm = tl.tm
    tps = tl.tiles_per_seq
    kern = functools.partial(_ln_res_kernel, tm=tm)
    const = lambda arr: pl.BlockSpec(arr.shape, lambda i, idx: (0,) * arr.ndim)
    return pl.pallas_call(
        kern,
        grid_spec=pltpu.PrefetchScalarGridSpec(
            num_scalar_prefetch=1,
            grid=(tl.n,),
            in_specs=[pl.BlockSpec(memory_space=pl.ANY),
                      pl.BlockSpec((tm, D_MODEL), lambda i, idx: (i, 0)),
                      pl.BlockSpec((1,) + g2.shape[1:], lambda i, idx: (i // tps, 0, 0)),
                      const(lng), const(lnb)],
            out_specs=pl.BlockSpec((tm, D_MODEL), lambda i, idx: (i, 0)),
            scratch_shapes=[pltpu.VMEM((tm * TOK_SUB, LANES), F32)] * 3 + [pltpu.SemaphoreType.DMA((3,))]),
        out_shape=jax.ShapeDtypeStruct((tl.t, D_MODEL), F32),
        compiler_params=_cparams(("arbitrary",)),
        name="ffn_residual_ln",
    )(dest, fs, x, g2, lng, lnb)


def _moe_ffn(tl, x1, sc2, sh2, gt, gi, g2, lng, lnb, wgu, wdn, layer):
    if tl.per_token:
        return _moe_dense(tl, x1, sc2, sh2, gt, g2, lng, lnb, wgu, wdn, layer)
    t = tl.t
    tme = EXPERT_TILE
    n_tiles = t // tme + N_BUCKETS
    n_pad = N_BUCKETS * tme
    assert n_pad % tl.tm == 0
    i32 = jnp.int32
    onehot = (gi[:, None] == jnp.arange(N_BUCKETS, dtype=i32)[None, :]).astype(i32)
    counts = jnp.sum(onehot, axis=0)
    padded = ((counts + tme - 1) // tme) * tme
    ends = jnp.cumsum(padded)
    starts = ends - padded
    dest = jnp.sum(onehot * (jnp.cumsum(onehot, axis=0) - onehot + starts[None, :]), axis=1).astype(i32)
    cpad = jnp.cumsum(padded - counts)
    k = jnp.arange(n_pad, dtype=i32)
    seg_base = jnp.concatenate([starts + counts, ends[-1:]])
    seg_first = jnp.concatenate([jnp.zeros((1,), cpad.dtype), cpad])
    seg_hot = ((k[:, None] >= seg_first[None, :])
               & (k[:, None] < jnp.concatenate([cpad, jnp.full((1,), n_pad, cpad.dtype)])[None, :])).astype(i32)
    pad_rows = (k + jnp.sum(seg_hot * (seg_base - seg_first)[None, :], axis=1)).astype(i32)
    tile_start = jnp.arange(n_tiles, dtype=i32) * tme
    tile_hot = ((tile_start[:, None] >= starts[None, :]) & (tile_start[:, None] < ends[None, :])).astype(i32)
    bucket_ids = jnp.arange(N_BUCKETS, dtype=i32)
    first_expert = layer * N_EXPERTS + (bucket_ids // N_PAIRS) * EXP_PER_GROUP
    pair_a = jnp.asarray(PAIR_A * N_GROUPS, i32)
    pair_b = jnp.asarray(PAIR_B * N_GROUPS, i32)
    last_expert = layer * N_EXPERTS + N_EXPERTS - 1
    in_use = jnp.sum(tile_hot, axis=1)
    tile_ea = (jnp.sum(tile_hot * (first_expert + pair_a)[None, :], axis=1) + (1 - in_use) * last_expert).astype(i32)
    tile_eb = (jnp.sum(tile_hot * (first_expert + pair_b)[None, :], axis=1) + (1 - in_use) * last_expert).astype(i32)
    tile_bucket = jnp.sum(tile_hot * bucket_ids[None, :], axis=1) - (1 - in_use)
    tile_new = (tile_bucket != jnp.concatenate([jnp.full((1,), -2, i32), tile_bucket[:-1]])).astype(i32)
    n_valid = (ends[-1] // tme).astype(i32).reshape(1)
    tok, gs = _moe_sort(tl, x1, sc2, sh2, gt, jnp.concatenate([dest, pad_rows]), n_tiles * tme)
    fs = _moe_experts(tok, gs, tile_ea, tile_eb, tile_new, n_valid, wgu, wdn, tme)
    return _ln_res(tl, x1, fs, dest, g2, lng, lnb)


def _mla_in_kernel(x_ref, sc_ref, sh_ref, wdkv_ref, kvg_ref, wukv_ref, wdq_ref, qg_ref, wuq_ref, cos_ref, sin_ref,
                   lat_ref, kr_ref, q_ref, *kv_refs, qscale, expand):
    x = x_ref[...]
    cos = cos_ref[...]
    sin = sin_ref[...]
    kv = jnp.dot(x.astype(BF16), wdkv_ref[...], preferred_element_type=F32)
    c = kv[:, :KV_LORA]
    lat = c * lax.rsqrt(jnp.mean(c * c, axis=-1, keepdims=True) + RMS_EPS) * kvg_ref[...]
    lat_ref[...] = lat
    kr = kv[:, KV_LORA:KV_LORA + LANES] * cos + kv[:, KV_LORA + LANES:] * sin
    kr_ref[...] = kr[:, :MLA_ROPE]
    if expand:
        k_ref, v_ref = kv_refs
        kvx = jnp.dot(lat.astype(BF16), wukv_ref[...], preferred_element_type=F32)
        krb = kr.astype(BF16)
        ones_col = jnp.where(lax.broadcasted_iota(jnp.int32, krb.shape, 1) == 0, 1.0, 0.0).astype(BF16)
        v0 = MLA_HEADS * MLA_NOPE
        for hh in range(MLA_HEADS):
            k_ref[:, hh * HEAD_W:hh * HEAD_W + MLA_NOPE] = kvx[:, hh * MLA_NOPE:(hh + 1) * MLA_NOPE].astype(BF16)
            k_ref[:, hh * HEAD_W + MLA_NOPE:(hh + 1) * HEAD_W] = krb
            v_ref[:, hh * HEAD_W:hh * HEAD_W + MLA_V] = kvx[:, v0 + hh * MLA_V:v0 + (hh + 1) * MLA_V].astype(BF16)
            v_ref[:, hh * HEAD_W + MLA_V:(hh + 1) * HEAD_W] = ones_col
    else:
        (krp_ref,) = kv_refs
        krp_ref[...] = kr
    h = (x * (1.0 + sc_ref[0]) + sh_ref[0]).astype(BF16)
    cq = jnp.dot(h, wdq_ref[...], preferred_element_type=F32)
    qn = (cq * lax.rsqrt(jnp.mean(cq * cq, axis=-1, keepdims=True) + RMS_EPS) * qg_ref[...]).astype(BF16)
    nw = MLA_HEADS * LANES
    qnope = jnp.dot(qn, wuq_ref[:, :nw], preferred_element_type=F32)
    qpe = jnp.dot(qn, wuq_ref[:, nw:2 * nw], preferred_element_type=F32)
    qpe_sw = jnp.dot(qn, wuq_ref[:, 2 * nw:], preferred_element_type=F32)
    for hh in range(MLA_HEADS):
        cols = slice(hh * LANES, (hh + 1) * LANES)
        q_ref[:, hh * HEAD_W:hh * HEAD_W + LANES] = (qnope[:, cols] * qscale).astype(BF16)
        pe = qpe[:, cols] * cos + qpe_sw[:, cols] * sin
        q_ref[:, hh * HEAD_W + LANES:(hh + 1) * HEAD_W] = (pe * qscale).astype(BF16)


def _mla_in(tl, x, sc, sh, p, j, cos, sin, qscale, expand):
    kern = functools.partial(_mla_in_kernel, qscale=qscale, expand=expand)
    weights = [p["w_dkv_ext"], p["kv_norm_g"], p["w_ukv_r"], p["w_dq"][j], p["q_norm_g"][j], p["w_uq_ext"][j]]
    out_w = [(KV_LORA, F32), (MLA_ROPE, F32), (MLA_HEADS * HEAD_W, BF16)]
    out_w += [(MLA_HEADS * HEAD_W, BF16)] * 2 if expand else [(LANES, F32)]
    return pl.pallas_call(
        kern,
        grid=(tl.n,),
        in_specs=[tl.rows(D_MODEL), tl.mod_spec(sc), tl.mod_spec(sh)] + [_const_spec(w) for w in weights]
                 + [tl.table_spec(cos), tl.table_spec(sin)],
        out_specs=[tl.rows(w) for w, _ in out_w],
        out_shape=[jax.ShapeDtypeStruct((tl.t, w), dt) for w, dt in out_w],
        compiler_params=_cparams(("parallel",)),
        name="mla_in_proj",
    )(x, sc, sh, *weights, cos, sin)


def _attn_kernel(q_ref, qn_ref, k_ref, v_ref, o_ref, m_sc, acc_sc, sa_sc, sb_sc, sc_sc, *, t, n_sub, hp):
    i = pl.program_id(2)
    m_sc[...] = jnp.full(m_sc.shape, NEG_INF, F32)
    acc_sc[...] = jnp.zeros(acc_sc.shape, F32)
    rs = t // n_sub

    def scores(j, s_ref, qr=q_ref):
        for h in range(hp):
            hc = slice(h * HEAD_W, (h + 1) * HEAD_W)
            k = k_ref[0, pl.ds(pl.multiple_of(j * t, t), t), hc]
            s_ref[h] = lax.dot_general(qr[0, :, hc], k, (((1,), (1,)), ((), ())), preferred_element_type=F32)

    def fold(j, s_ref, diagonal):
        for h in range(hp):
            v = v_ref[0, pl.ds(pl.multiple_of(j * t, t), t), h * HEAD_W:(h + 1) * HEAD_W]
            for u in range(n_sub):
                rows = slice(u * rs, (u + 1) * rs)
                s = s_ref[h, rows, :]
                if diagonal:
                    qpos = u * rs + lax.broadcasted_iota(jnp.int32, (rs, 1), 0)
                    kpos = lax.broadcasted_iota(jnp.int32, (1, t), 1)
                    s = jnp.where((kpos >> LOG2_CHUNK) <= (qpos >> LOG2_CHUNK), s, NEG_INF)
                m_old = m_sc[h, rows, :]
                m_new = jnp.maximum(m_old, jnp.max(s, axis=1, keepdims=True))
                alpha = jnp.exp2(m_old - m_new)
                p = jnp.exp2(s - jnp.tile(m_new, (1, t // LANES)))
                pv = jnp.dot(p.astype(BF16), v, preferred_element_type=F32)
                acc_sc[h, rows, :] = jnp.tile(alpha, (1, HEAD_W // LANES)) * acc_sc[h, rows, :] + pv
                m_sc[h, rows, :] = m_new

    def next_tile_scores():
        scores(0, sc_sc, qn_ref)

    @pl.when(i == 0)
    def _():
        scores(0, sc_sc)
        fold(0, sc_sc, True)
        next_tile_scores()

    @pl.when(i == 1)
    def _():
        scores(1, sb_sc)
        fold(0, sc_sc, False)
        next_tile_scores()
        fold(1, sb_sc, True)

    @pl.when(i >= 2)
    def _():
        scores(1, sb_sc)
        fold(0, sc_sc, False)
        scores(2, sa_sc)
        fold(1, sb_sc, False)

        def fold_pair(j):
            scores(j + 1, sb_sc)
            fold(j, sa_sc, False)
            scores(j + 2, sa_sc)
            fold(j + 1, sb_sc, False)

        def quad_step(qq, carry):
            fold_pair(2 + 4 * qq)
            fold_pair(4 + 4 * qq)
            return carry

        n_pair = i // 2 - 1
        lax.fori_loop(0, n_pair // 2, quad_step, 0)

        @pl.when(n_pair % 2 == 1)
        def _():
            fold_pair(2 * n_pair)

        @pl.when(i % 2 == 0)
        def _():
            next_tile_scores()
            fold(i, sa_sc, True)

        @pl.when(i % 2 == 1)
        def _():
            scores(i, sb_sc)
            fold(i - 1, sa_sc, False)
            next_tile_scores()
            fold(i, sb_sc, True)

    for h in range(hp):
        acc = acc_sc[h]
        o_ref[0, :, h * MLA_V:(h + 1) * MLA_V] = (acc[:, :MLA_V] / acc[:, MLA_V:MLA_V + 1]).astype(BF16)


def _attention(qc, kc, vv, b, l, t, hp):
    assert l % t == 0 and t % CHUNK == 0
    n_sub = 2 if t % (2 * LANES) == 0 else 1
    nq = l // t
    kern = functools.partial(_attn_kernel, t=t, n_sub=n_sub, hp=hp)
    scores_buf = pltpu.VMEM((hp, t, t), F32)
    return pl.pallas_call(
        kern,
        grid=(b, MLA_HEADS // hp, nq),
        in_specs=[pl.BlockSpec((1, t, hp * HEAD_W), lambda bi, h, i: (bi, i, h)),
                  pl.BlockSpec((1, t, hp * HEAD_W), lambda bi, h, i: (bi, jnp.minimum(i + 1, nq - 1), h)),
                  pl.BlockSpec((1, l, hp * HEAD_W), lambda bi, h, i: (bi, 0, h)),
                  pl.BlockSpec((1, l, hp * HEAD_W), lambda bi, h, i: (bi, 0, h))],
        out_specs=pl.BlockSpec((1, t, hp * MLA_V), lambda bi, h, i: (bi, i, h)),
        out_shape=jax.ShapeDtypeStruct((b, l, MLA_HEADS * MLA_V), BF16),
        scratch_shapes=[pltpu.VMEM((hp, t, LANES), F32), pltpu.VMEM((hp, t, HEAD_W), F32),
                        scores_buf, scores_buf, scores_buf],
        compiler_params=_cparams(("parallel", "parallel", "arbitrary"), ATTN_VMEM_LIMIT),
        name="mla_attention",
    )(qc.reshape(b, l, MLA_HEADS * HEAD_W), qc.reshape(b, l, MLA_HEADS * HEAD_W),
      kc.reshape(b, l, MLA_HEADS * HEAD_W), vv.reshape(b, l, MLA_HEADS * HEAD_W))


def _decode_attn_kernel(q_ref, lnew_ref, knew_ref, lpast_ref, kpast_ref, wukt_ref, wuv_ref, o_ref, qa_sc,
                        *, l, pos0, past):
    nope_w = KV_LORA
    for h in range(MLA_HEADS):
        rows = slice(h * l, (h + 1) * l)
        qn = q_ref[0, :, h * HEAD_W:h * HEAD_W + MLA_NOPE]
        qa_sc[rows, :nope_w] = jnp.dot(qn, wukt_ref[h], preferred_element_type=F32).astype(BF16)
        qa_sc[rows, nope_w:] = q_ref[0, :, h * HEAD_W + MLA_NOPE:(h + 1) * HEAD_W]
    q_lat = qa_sc[:, :nope_w]
    q_pe = qa_sc[:, nope_w:nope_w + MLA_ROPE]
    nt = (((1,), (1,)), ((), ()))
    lp = lpast_ref[0].astype(BF16)
    kp = kpast_ref[0].astype(BF16)
    ln = lnew_ref[0].astype(BF16)
    kn = knew_ref[0][:, :MLA_ROPE].astype(BF16)
    s_past = (lax.dot_general(q_lat, lp, nt, preferred_element_type=F32)
              + lax.dot_general(q_pe, kp, nt, preferred_element_type=F32))
    s_new = (lax.dot_general(q_lat, ln, nt, preferred_element_type=F32)
             + lax.dot_general(q_pe, kn, nt, preferred_element_type=F32))
    n_rows = MLA_HEADS * l
    qpos = pos0 + lax.rem(lax.broadcasted_iota(jnp.int32, (n_rows, 1), 0), l)
    kpos_new = pos0 + lax.broadcasted_iota(jnp.int32, (1, l), 1)
    s_new = jnp.where((kpos_new >> LOG2_CHUNK) <= (qpos >> LOG2_CHUNK), s_new, NEG_INF)
    kpos_past = lax.broadcasted_iota(jnp.int32, (1, past), 1)
    s_past = jnp.where((kpos_past >> LOG2_CHUNK) <= (qpos >> LOG2_CHUNK), s_past, NEG_INF)
    m = jnp.maximum(jnp.max(s_past, axis=1, keepdims=True), jnp.max(s_new, axis=1, keepdims=True))
    p_past = jnp.exp2(s_past - m)
    p_new = jnp.exp2(s_new - m)
    denom = jnp.sum(p_past, axis=1, keepdims=True) + jnp.sum(p_new, axis=1, keepdims=True)
    ctx = (jnp.dot(p_past.astype(BF16), lp, preferred_element_type=F32)
           + jnp.dot(p_new.astype(BF16), ln, preferred_element_type=F32)) / denom
    ctx = ctx.astype(BF16)
    for h in range(MLA_HEADS):
        o_ref[0, :, h * MLA_V:(h + 1) * MLA_V] = jnp.dot(
            ctx[h * l:(h + 1) * l], wuv_ref[h], preferred_element_type=F32).astype(BF16)


def _decode_attention(qc, latent_new, krp_new, past_latent, past_k_rope, w_uk_t, w_uv, b, l, pos0):
    past = past_latent.shape[1]
    kern = functools.partial(_decode_attn_kernel, l=l, pos0=pos0, past=past)
    per_b = lambda shape: pl.BlockSpec((1,) + shape, lambda bi: (bi, 0, 0))
    return pl.pallas_call(
        kern,
        grid=(b,),
        in_specs=[per_b((l, MLA_HEADS * HEAD_W)), per_b((l, KV_LORA)), per_b((l, LANES)),
                  per_b((past, KV_LORA)), per_b((past, MLA_ROPE)), _const_spec(w_uk_t), _const_spec(w_uv)],
        out_specs=per_b((l, MLA_HEADS * MLA_V)),
        out_shape=jax.ShapeDtypeStruct((b, l, MLA_HEADS * MLA_V), BF16),
        scratch_shapes=[pltpu.VMEM((MLA_HEADS * l, KV_LORA + LANES), BF16)],
        compiler_params=_cparams(("parallel",)),
        name="mla_decode_attention",
    )(qc.reshape(b, l, MLA_HEADS * HEAD_W), latent_new.reshape(b, l, KV_LORA), krp_new.reshape(b, l, LANES),
      past_latent, past_k_rope, w_uk_t, w_uv)


def _rope_tables(pos0, length, half):
    inv = ROPE_THETA ** (-np.arange(half, dtype=np.float64) / half)
    ang = (pos0 + np.arange(length, dtype=np.float64))[:, None] * inv[None, :]
    return np.cos(ang), np.sin(ang)


def _mla_rope_tables(pos0, length):
    cos, sin = _rope_tables(pos0, length, MLA_ROPE // 2)
    z = np.zeros((length, LANES - MLA_ROPE))
    return (jnp.asarray(np.concatenate([cos, cos, z], axis=1), F32),
            jnp.asarray(np.concatenate([-sin, sin, z], axis=1), F32))


def _swap_halves(w):
    half = w.shape[-1] // 2
    return jnp.concatenate([w[..., half:], w[..., :half]], axis=-1)


def _pad_lanes(w):
    return jnp.pad(w, [(0, 0)] * (w.ndim - 1) + [(0, LANES - w.shape[-1])])


def _prep_weights(w_ret_in, w_ret_out, w_dq, q_norm_g, w_uq, w_mla_out, w_dkv, kv_norm_g, w_ukv,
                  w_route_group, b_route_group, w_route_expert, b_route_expert,
                  w_expert_gate_up, w_expert_down):
    p = {}
    p["w_ret_in"] = [w_ret_in[i].astype(BF16) for i in range(N_A)]
    p["w_ret_out"] = [w_ret_out[i].astype(BF16) for i in range(N_A)]
    p["w_dq"] = [w_dq[j].astype(BF16) for j in range(DEPTH - N_A)]
    p["q_norm_g"] = [q_norm_g[j][None, :] for j in range(DEPTH - N_A)]
    w_uq_ext = []
    for j in range(DEPTH - N_A):
        wq = w_uq[j].reshape(Q_LORA, MLA_HEADS, MLA_NOPE + MLA_ROPE)
        nope = wq[:, :, :MLA_NOPE].reshape(Q_LORA, MLA_HEADS * LANES)
        pe = wq[:, :, MLA_NOPE:]
        w_uq_ext.append(jnp.concatenate(
            [nope, _pad_lanes(pe).reshape(Q_LORA, MLA_HEADS * LANES),
             _pad_lanes(_swap_halves(pe)).reshape(Q_LORA, MLA_HEADS * LANES)], axis=1).astype(BF16))
    p["w_uq_ext"] = w_uq_ext
    p["w_mla_out"] = [w_mla_out[j].astype(BF16) for j in range(DEPTH - N_A)]
    kr = w_dkv[:, KV_LORA:]
    p["w_dkv_ext"] = jnp.concatenate([w_dkv[:, :KV_LORA], _pad_lanes(kr), _pad_lanes(_swap_halves(kr))],
                                     axis=1).astype(BF16)
    p["kv_norm_g"] = kv_norm_g[None, :]
    w_ukv4 = w_ukv.reshape(KV_LORA, MLA_HEADS, 2, MLA_NOPE)
    p["w_ukv_r"] = w_ukv4.transpose(0, 2, 1, 3).reshape(KV_LORA, 2 * MLA_HEADS * MLA_NOPE).astype(BF16)
    p["w_uk_t"] = w_ukv4[:, :, 0, :].transpose(1, 2, 0).astype(BF16)
    p["w_uv"] = w_ukv4[:, :, 1, :].transpose(1, 0, 2).astype(BF16)
    wr = jnp.concatenate([w_route_group, w_route_expert], axis=-1)
    wr = jnp.pad(wr, ((0, 0), (0, 0), (0, ROUTE_W - wr.shape[-1])))
    wr_hi = wr.astype(BF16)
    wr_lo = (wr - wr_hi.astype(F32)).astype(BF16)
    br = jnp.concatenate([b_route_group, b_route_expert], axis=-1)
    br = jnp.pad(br, ((0, 0), (0, ROUTE_W - br.shape[-1])))
    wr_cat = jnp.concatenate([wr_hi, wr_lo], axis=-1)
    p["wr"] = [wr_cat[l] for l in range(DEPTH)]
    p["br"] = [br[l][None, :] for l in range(DEPTH)]
    p["w_gu"] = w_expert_gate_up.reshape(DEPTH * N_EXPERTS, D_MODEL, 2 * EXPERT_FF)
    p["w_dn"] = w_expert_down.reshape(DEPTH * N_EXPERTS, EXPERT_FF, D_MODEL)
    return p


def _trunk(x3, mod, pos0, ret_s0, past_latent, past_k_rope, ret_chunk, ret_block, ln_g, ln_b, p):
    b, l, _ = x3.shape
    tl = _Tiles(b, l)
    x = x3.reshape(tl.t, D_MODEL)
    new_ret = []
    latent_new = k_rope_new = None
    for layer in range(DEPTH):
        sh1, sc1, g1, sh2, sc2, g2 = [tl.mod(m) for m in jnp.split(mod[layer], 6, axis=-1)]
        lng = [ln_g[layer, s][None, :] for s in range(2)]
        lnb = [ln_b[layer, s][None, :] for s in range(2)]
        if layer < N_A:
            cos, sin = [jnp.asarray(tab, F32) for tab in _rope_tables(pos0, l, RET_DK // 2)]
            q, k, v, sg = _ret_in(tl, x, sc1, sh1, p["w_ret_in"][layer], tl.table(cos), tl.table(sin))
            a, s_new = _retention(b, l, q, k, v, sg, ret_s0[layer], ret_chunk, ret_block)
            a = a.reshape(tl.t, RET_HEADS * RET_DV)
            new_ret.append(s_new)
            w_out = p["w_ret_out"][layer]
            name = "ret_out_ln_route"
        else:
            j = layer - N_A
            cos, sin = _mla_rope_tables(pos0, l)
            cos, sin = tl.table(cos), tl.table(sin)
            assert layer == N_A == DEPTH - 1
            qscale = float((MLA_NOPE + MLA_ROPE) ** -0.5 * math.log2(math.e))
            if past_latent is None:
                assert pos0 == 0
                latent_new, k_rope_new, qc, kc, vv = _mla_in(tl, x, sc1, sh1, p, j, cos, sin, qscale, True)
                a = _attention(qc, kc, vv, b, l, ATTN_TILE, 2)
            else:
                latent_new, k_rope_new, qc, krp = _mla_in(tl, x, sc1, sh1, p, j, cos, sin, qscale, False)
                a = _decode_attention(qc, latent_new, krp, past_latent, past_k_rope, p["w_uk_t"], p["w_uv"],
                                      b, l, pos0)
            a = a.reshape(tl.t, MLA_HEADS * MLA_V)
            w_out = p["w_mla_out"][j]
            name = "mla_out_ln_route"
        x1, gt, gi = _mix_out(tl, a, w_out, x, g1, lng[0], lnb[0], sc2, sh2,
                              p["wr"][layer], p["br"][layer], name)
        x = _moe_ffn(tl, x1, sc2, sh2, gt, gi.reshape(tl.t), g2, lng[1], lnb[1],
                     p["w_gu"], p["w_dn"], layer)
    return (x.reshape(b, l, D_MODEL), jnp.stack(new_ret), latent_new.reshape(b, l, KV_LORA),
            k_rope_new.reshape(b, l, MLA_ROPE))


def kernel(x_prompt, x_sample, state_retention, cache_kv_latent, cache_k_rope, c_prompt, c_sample,
           w_ada, b_ada, ln_g, ln_b, w_ret_in, w_ret_out, w_dq, q_norm_g, w_uq, w_mla_out,
           w_dkv, kv_norm_g, w_ukv, w_route_group, b_route_group, w_route_expert, b_route_expert,
           w_expert_gate_up, w_expert_down):
    bp, lp, _ = x_prompt.shape
    bs, ls, _ = x_sample.shape
    p = _prep_weights(w_ret_in, w_ret_out, w_dq, q_norm_g, w_uq, w_mla_out, w_dkv, kv_norm_g, w_ukv,
                      w_route_group, b_route_group, w_route_expert, b_route_expert,
                      w_expert_gate_up, w_expert_down)
    n_seq = bp + bs
    n_rows = -(-n_seq // 8) * 8
    c_all = jnp.concatenate([c_prompt, c_sample, jnp.zeros((n_rows - n_seq, D_MODEL), F32)], axis=0)
    mod = _ada(c_all, w_ada, b_ada)
    s0_p = jnp.zeros((N_A, bp, RET_HEADS, RET_DK, RET_DV), F32)
    ret_chunk_p = RET_CHUNK if lp % RET_CHUNK == 0 else CHUNK
    y_p, st_p, lat_p, kr_p = _trunk(x_prompt, mod[:, :bp], 0, s0_p, None, None,
                                    ret_chunk_p, max(ret_chunk_p, min(lp, RET_BLOCK)), ln_g, ln_b, p)
    past = cache_kv_latent.shape[1]
    y_s, st_s, lat_s, kr_s = _trunk(x_sample, mod[:, bp:n_seq], past, state_retention,
                                    cache_kv_latent, cache_k_rope, ls, ls, ln_g, ln_b, p)
    return (y_p, y_s, st_p, st_s, lat_p, kr_p, lat_s, kr_s)
```

```python
import functools
import math

import numpy as np
import jax
import jax.numpy as jnp
from jax import lax
from jax.experimental import pallas as pl
from jax.experimental.pallas import tpu as pltpu

F32 = jnp.float32
BF16 = jnp.bfloat16

D_MODEL = 1024
DEPTH = 2
CHUNK = 64
LOG2_CHUNK = 6
N_A = DEPTH // 2
RET_HEADS = 4
RET_DK = 256
RET_DV = 512
MLA_HEADS = 8
MLA_NOPE = 128
MLA_ROPE = 64
MLA_V = 128
Q_LORA = 384
KV_LORA = 256
ROPE_THETA = 10000.0
N_GROUPS = 4
EXP_PER_GROUP = 4
N_EXPERTS = N_GROUPS * EXP_PER_GROUP
N_PAIRS = EXP_PER_GROUP * (EXP_PER_GROUP - 1) // 2
N_BUCKETS = N_GROUPS * N_PAIRS
PAIR_A = (0, 0, 0, 1, 1, 2)
PAIR_B = (1, 2, 3, 2, 3, 3)
EXPERT_FF = 512
LN_EPS = 1e-5
RMS_EPS = 1e-6
DN_ALPHA = (2 * DEPTH) ** 0.25
NEG_INF = -1e30

LANES = 128
ROUTE_W = LANES
TOKEN_TILE = 512
EXPERT_TILE = 256
RET_CHUNK = 256
RET_BLOCK = 1024
ATTN_TILE = 512
HEAD_W = 2 * LANES
VMEM_LIMIT = 56 * 1024 * 1024
ATTN_VMEM_LIMIT = 56 * 1024 * 1024


def _cparams(sem, vmem=VMEM_LIMIT):
    return pltpu.CompilerParams(dimension_semantics=sem, vmem_limit_bytes=vmem)


def _silu(x):
    return x * jax.nn.sigmoid(x)


def _layer_norm(z, g, b):
    mu = jnp.mean(z, axis=-1, keepdims=True)
    zc = z - mu
    var = jnp.mean(zc * zc, axis=-1, keepdims=True)
    return zc * lax.rsqrt(var + LN_EPS) * g + b


def _ada_kernel(c_ref, w_ref, b_ref, o_ref):
    s = _silu(c_ref[...]).astype(BF16)
    o_ref[0] = jnp.dot(s, w_ref[0].astype(BF16), preferred_element_type=F32) + b_ref[0]


def _ada(c_all, w_ada, b_ada):
    r = c_all.shape[0]
    n = w_ada.shape[-1]
    tn = 1536
    return pl.pallas_call(
        _ada_kernel,
        grid=(DEPTH, n // tn),
        in_specs=[pl.BlockSpec((r, D_MODEL), lambda l, j: (0, 0)),
                  pl.BlockSpec((1, D_MODEL, tn), lambda l, j: (l, 0, j)),
                  pl.BlockSpec((1, 1, tn), lambda l, j: (l, 0, j))],
        out_specs=pl.BlockSpec((1, r, tn), lambda l, j: (l, 0, j)),
        out_shape=jax.ShapeDtypeStruct((DEPTH, r, n), F32),
        compiler_params=_cparams(("parallel", "parallel")),
        name="ada_mod",
    )(c_all, w_ada, b_ada.reshape(DEPTH, 1, n))


class _Tiles:
    def __init__(self, b, l):
        self.b, self.l, self.t = b, l, b * l
        self.per_token = (l % TOKEN_TILE) != 0
        self.tm = self.t if self.per_token else TOKEN_TILE
        self.n = self.t // self.tm
        self.tiles_per_seq = 1 if self.per_token else l // self.tm

    def mod(self, m):
        if self.per_token:
            return jnp.repeat(m, self.l, axis=0)[None]
        return m[:, None, :]

    def mod_spec(self, arr):
        tps = self.tiles_per_seq
        return pl.BlockSpec((1,) + arr.shape[1:], lambda i: (i // tps, 0, 0))

    def table(self, tab):
        return jnp.tile(tab, (self.b, 1)) if self.per_token else tab

    def table_spec(self, tab):
        nt = tab.shape[0] // self.tm
        return pl.BlockSpec((self.tm, tab.shape[1]), lambda i: (i % nt, 0))

    def rows(self, w):
        return pl.BlockSpec((self.tm, w), lambda i: (i, 0))


def _const_spec(arr):
    nd = arr.ndim
    return pl.BlockSpec(arr.shape, lambda *_: (0,) * nd)


def _ret_in_kernel(x_ref, sc_ref, sh_ref, w_ref, cos_ref, sin_ref, q_ref, k_ref, v_ref, sg_ref):
    h = (x_ref[...] * (1.0 + sc_ref[0]) + sh_ref[0]).astype(BF16)
    cos = cos_ref[...]
    sin = sin_ref[...]
    qk_w = RET_HEADS * RET_DK
    half = RET_DK // 2

    def rope_store(r, out_ref, scale):
        for hh in range(RET_HEADS):
            a = r[:, hh * RET_DK:hh * RET_DK + half]
            b = r[:, hh * RET_DK + half:(hh + 1) * RET_DK]
            out_ref[:, hh * RET_DK:hh * RET_DK + half] = ((a * cos - b * sin) * scale).astype(BF16)
            out_ref[:, hh * RET_DK + half:(hh + 1) * RET_DK] = ((a * sin + b * cos) * scale).astype(BF16)

    r = jnp.dot(h, w_ref[:, 0:qk_w], preferred_element_type=F32)
    rope_store(r, q_ref, 1.0)
    r = jnp.dot(h, w_ref[:, qk_w:2 * qk_w], preferred_element_type=F32)
    rope_store(r, k_ref, RET_DK ** -0.5)
    for c in range(2):
        lo = 2 * qk_w + c * qk_w
        v_ref[:, c * qk_w:(c + 1) * qk_w] = jnp.dot(
            h, w_ref[:, lo:lo + qk_w], preferred_element_type=F32).astype(BF16)
    for c in range(2):
        lo = 4 * qk_w + c * qk_w
        g = jnp.dot(h, w_ref[:, lo:lo + qk_w], preferred_element_type=F32)
        sg_ref[:, c * qk_w:(c + 1) * qk_w] = _silu(g).astype(BF16)


def _ret_in(tl, x, sc, sh, w_in, cos, sin):
    qk_w, v_w = RET_HEADS * RET_DK, RET_HEADS * RET_DV
    return pl.pallas_call(
        _ret_in_kernel,
        grid=(tl.n,),
        in_specs=[tl.rows(D_MODEL), tl.mod_spec(sc), tl.mod_spec(sh), _const_spec(w_in),
                  tl.table_spec(cos), tl.table_spec(sin)],
        out_specs=[tl.rows(qk_w), tl.rows(qk_w), tl.rows(v_w), tl.rows(v_w)],
        out_shape=[jax.ShapeDtypeStruct((tl.t, qk_w), BF16), jax.ShapeDtypeStruct((tl.t, qk_w), BF16),
                   jax.ShapeDtypeStruct((tl.t, v_w), BF16), jax.ShapeDtypeStruct((tl.t, v_w), BF16)],
        compiler_params=_cparams(("parallel",)),
        name="ret_in_proj",
    )(x, sc, sh, w_in, cos, sin)


def _retention_tables(chunk):
    lg = np.log1p(-np.exp2(-5.0 - np.arange(RET_HEADS, dtype=np.float64)))
    idx = np.arange(chunk, dtype=np.float64)
    diff = idx[:, None] - idx[None, :]
    dmask = np.where(diff >= 0, np.exp(lg[:, None, None] * np.maximum(diff, 0.0)), 0.0)
    xi = np.exp(lg[:, None] * (idx[None, :] + 1.0))[:, :, None]
    zeta = np.exp(lg[:, None] * (chunk - 1.0 - idx[None, :]))[:, :, None]
    cdec = np.exp(lg * chunk)
    return (jnp.asarray(dmask, F32), jnp.asarray(xi, F32), jnp.asarray(zeta, F32),
            [float(np.float32(c)) for c in cdec])


def _retention_kernel(q_ref, k_ref, v_ref, sg_ref, s0_ref, dm_ref, xi_ref, zeta_ref,
                      o_ref, sout_ref, s_sc, *, chunk, n_chunks, cdec):
    j = pl.program_id(1)

    @pl.when(j == 0)
    def _():
        s_sc[...] = s0_ref[0]

    heads = range(RET_HEADS)
    kcols = [slice(hh * RET_DK, (hh + 1) * RET_DK) for hh in heads]
    vcols = [slice(hh * RET_DV, (hh + 1) * RET_DV) for hh in heads]
    nt = (((1,), (1,)), ((), ()))

    def epilogue(rows, o):
        for hh in heads:
            mu = jnp.mean(o[hh], axis=-1, keepdims=True)
            oc = o[hh] - mu
            var = jnp.mean(oc * oc, axis=-1, keepdims=True)
            on = oc * lax.rsqrt(var + LN_EPS)
            o_ref[0, rows, vcols[hh]] = (sg_ref[0, rows, vcols[hh]].astype(F32) * on).astype(BF16)

    for c in range(n_chunks):
        rows = slice(c * chunk, (c + 1) * chunk)
        q = [q_ref[0, rows, kcols[hh]] for hh in heads]
        k = [k_ref[0, rows, kcols[hh]] for hh in heads]
        v = [v_ref[0, rows, vcols[hh]] for hh in heads]
        sc = [(lax.dot_general(q[hh], k[hh], nt, preferred_element_type=F32) * dm_ref[hh]).astype(BF16)
              for hh in heads]
        cross = [jnp.dot(q[hh], s_sc[hh].astype(BF16), preferred_element_type=F32) * xi_ref[hh] for hh in heads]
        o = [jnp.dot(sc[hh], v[hh], preferred_element_type=F32) + cross[hh] for hh in heads]
        for hh in heads:
            kz_t = (k[hh].astype(F32) * zeta_ref[hh]).T.astype(BF16)
            s_sc[hh] = s_sc[hh] * cdec[hh] + jnp.dot(kz_t, v[hh], preferred_element_type=F32)
        epilogue(rows, o)

    @pl.when(j == pl.num_programs(1) - 1)
    def _():
        sout_ref[0] = s_sc[...]


def _retention(b, l, q, k, v, sg, s0, chunk, block):
    qk_w, v_w = RET_HEADS * RET_DK, RET_HEADS * RET_DV
    dmask, xi, zeta, cdec = _retention_tables(chunk)
    kern = functools.partial(_retention_kernel, chunk=chunk, n_chunks=block // chunk, cdec=cdec)
    seq = lambda w: pl.BlockSpec((1, block, w), lambda bi, j: (bi, j, 0))
    st = pl.BlockSpec((1, RET_HEADS, RET_DK, RET_DV), lambda bi, j: (bi, 0, 0, 0))
    return pl.pallas_call(
        kern,
        grid=(b, l // block),
        in_specs=[seq(qk_w), seq(qk_w), seq(v_w), seq(v_w), st,
                  _const_spec(dmask), _const_spec(xi), _const_spec(zeta)],
        out_specs=[seq(v_w), st],
        out_shape=[jax.ShapeDtypeStruct((b, l, v_w), BF16),
                   jax.ShapeDtypeStruct((b, RET_HEADS, RET_DK, RET_DV), F32)],
        scratch_shapes=[pltpu.VMEM((RET_HEADS, RET_DK, RET_DV), F32)],
        compiler_params=_cparams(("parallel", "arbitrary")),
        name="retention_scan",
    )(q.reshape(b, l, qk_w), k.reshape(b, l, qk_w), v.reshape(b, l, v_w), sg.reshape(b, l, v_w),
      s0, dmask, xi, zeta)


def _route_rows(lt):
    g = [lt[i:i + 1] for i in range(N_GROUPS)]
    m = jnp.maximum(jnp.maximum(g[0], g[1]), jnp.maximum(g[2], g[3]))
    gi = jnp.where(g[0] == m, 0, jnp.where(g[1] == m, 1, jnp.where(g[2] == m, 2, 3))).astype(jnp.int32)
    denom = jnp.exp(g[0] - m) + jnp.exp(g[1] - m) + jnp.exp(g[2] - m) + jnp.exp(g[3] - m)
    p_group = 1.0 / denom
    le = []
    for e in range(EXP_PER_GROUP):
        r = [lt[N_GROUPS + gg * EXP_PER_GROUP + e:N_GROUPS + gg * EXP_PER_GROUP + e + 1]
             for gg in range(N_GROUPS)]
        le.append(jnp.where(gi == 0, r[0], jnp.where(gi == 1, r[1], jnp.where(gi == 2, r[2], r[3]))))
    me = jnp.maximum(jnp.maximum(le[0], le[1]), jnp.maximum(le[2], le[3]))
    ex = [jnp.exp(x - me) for x in le]

    def first_argmax(vals):
        mx = jnp.maximum(jnp.maximum(vals[0], vals[1]), jnp.maximum(vals[2], vals[3]))
        ix = jnp.where(vals[0] == mx, 0, jnp.where(vals[1] == mx, 1, jnp.where(vals[2] == mx, 2, 3)))
        return mx, ix.astype(jnp.int32)

    e1, i1 = first_argmax(ex)
    ex2 = [jnp.where(i1 == e, -1.0, ex[e]) for e in range(EXP_PER_GROUP)]
    e2, i2 = first_argmax(ex2)
    tot = e1 + e2
    w1 = e1 / tot * p_group
    w2 = e2 / tot * p_group
    first_low = i1 < i2
    a = jnp.where(first_low, i1, i2)
    b = jnp.where(first_low, i2, i1)
    pair = jnp.where(a == 0, b - 1, jnp.where(a == 1, b + 1, N_PAIRS - 1))
    bucket = gi * N_PAIRS + pair
    return bucket, gi * EXP_PER_GROUP + a, gi * EXP_PER_GROUP + b, \
        jnp.where(first_low, w1, w2), jnp.where(first_low, w2, w1)


def _mix_out_kernel(a_ref, w_ref, x_ref, g1_ref, lng_ref, lnb_ref, sc2_ref, sh2_ref,
                    wr_ref, br_ref, x1_ref, gt_ref, gi_ref, *, dense_gates):
    y = jnp.dot(a_ref[...], w_ref[...], preferred_element_type=F32)
    x1 = _layer_norm(DN_ALPHA * x_ref[...] + g1_ref[0] * y, lng_ref[...], lnb_ref[...])
    x1_ref[...] = x1
    h2 = x1 * (1.0 + sc2_ref[0]) + sh2_ref[0]
    hi = h2.astype(BF16)
    lo = (h2 - hi.astype(F32)).astype(BF16)
    hw = jnp.dot(hi, wr_ref[...], preferred_element_type=F32)
    logits = (hw[:, :ROUTE_W] + (hw[:, ROUTE_W:] + jnp.dot(lo, wr_ref[:, :ROUTE_W], preferred_element_type=F32))
              + br_ref[...])
    lt = logits.T
    bucket, exp_a, exp_b, gate_a, gate_b = _route_rows(lt)
    gi_ref[0] = bucket
    tm = lt.shape[1]
    if dense_gates:
        row = lax.broadcasted_iota(jnp.int32, (ROUTE_W, tm), 0)
        gt = jnp.where(row == exp_a, gate_a, jnp.where(row == exp_b, gate_b, 0.0))
    else:
        row = lax.broadcasted_iota(jnp.int32, (8, tm), 0)
        g8 = jnp.where(row == 0, gate_a, jnp.where(row == 1, gate_b, 0.0))
        gt = jnp.concatenate([g8, jnp.zeros((ROUTE_W - 8, tm), F32)], axis=0)
    gt_ref[...] = gt.T


def _mix_out(tl, a, w, x, g1, lng, lnb, sc2, sh2, wr, br, name):
    kd = a.shape[1]
    return pl.pallas_call(
        functools.partial(_mix_out_kernel, dense_gates=tl.per_token),
        grid=(tl.n,),
        in_specs=[tl.rows(kd), _const_spec(w), tl.rows(D_MODEL), tl.mod_spec(g1),
                  _const_spec(lng), _const_spec(lnb), tl.mod_spec(sc2), tl.mod_spec(sh2),
                  _const_spec(wr), _const_spec(br)],
        out_specs=[tl.rows(D_MODEL), tl.rows(ROUTE_W),
                   pl.BlockSpec((1, 1, tl.tm), lambda i: (i, 0, 0))],
        out_shape=[jax.ShapeDtypeStruct((tl.t, D_MODEL), F32),
                   jax.ShapeDtypeStruct((tl.t, ROUTE_W), F32),
                   jax.ShapeDtypeStruct((tl.n, 1, tl.tm), jnp.int32)],
        compiler_params=_cparams(("parallel",)),
        name=name,
    )(a, w, x, g1, lng, lnb, sc2, sh2, wr, br)


TOK_SUB = D_MODEL // LANES
DMA_PRIORITIES = 2


def _moe_sort_kernel(idx_ref, x1_ref, sc2_ref, sh2_ref, gt_ref, tok_ref, gs_ref, buf0, buf1, gs_sc, sem, gsem,
                     *, tm):
    i = pl.program_id(0)
    n = pl.num_programs(0)
    bufs = (buf0, buf1)
    base = i * tm

    def wait_slot(sl):
        pltpu.make_async_copy(bufs[sl], tok_ref.at[pl.ds(0, tm * TOK_SUB)], sem.at[sl]).wait()

    def step(sl):
        @pl.when(i >= 2)
        def _():
            wait_slot(sl)

        h2 = x1_ref[...] * (1.0 + sc2_ref[0]) + sh2_ref[0]
        for s in range(TOK_SUB):
            bufs[sl][pl.ds(s, tm, stride=TOK_SUB), :] = h2[:, s * LANES:(s + 1) * LANES]
        for r in range(tm):
            d = idx_ref[base + r]
            pltpu.make_async_copy(bufs[sl].at[pl.ds(r * TOK_SUB, TOK_SUB)],
                                  tok_ref.at[pl.ds(pl.multiple_of(d * TOK_SUB, TOK_SUB), TOK_SUB)],
                                  sem.at[sl]).start(priority=r % DMA_PRIORITIES)
            gs_sc[pl.ds(d, 1), :] = gt_ref[r:r + 1, :]

        @pl.when(i == n - 1)
        def _():
            gates_out = pltpu.make_async_copy(gs_sc, gs_ref, gsem)
            gates_out.start()

            @pl.when(n >= 2)
            def _():
                wait_slot(1 - sl)
            wait_slot(sl)
            gates_out.wait()

    for sl in range(len(bufs)):
        @pl.when(i % len(bufs) == sl)
        def _():
            step(sl)


def _moe_sort(tl, x1, sc2, sh2, gt, dst_idx, n_rows):
    tm = tl.tm
    n_steps = dst_idx.shape[0] // tm
    last = tl.n - 1
    tps = tl.tiles_per_seq
    kern = functools.partial(_moe_sort_kernel, tm=tm)
    rows = lambda w: pl.BlockSpec((tm, w), lambda i, idx: (jnp.minimum(i, last), 0))
    mod = lambda arr: pl.BlockSpec((1,) + arr.shape[1:], lambda i, idx: (jnp.minimum(i, last) // tps, 0, 0))
    return pl.pallas_call(
        kern,
        grid_spec=pltpu.PrefetchScalarGridSpec(
            num_scalar_prefetch=1,
            grid=(n_steps,),
            in_specs=[rows(D_MODEL), mod(sc2), mod(sh2), rows(ROUTE_W)],
            out_specs=[pl.BlockSpec(memory_space=pl.ANY), pl.BlockSpec(memory_space=pl.ANY)],
            scratch_shapes=[pltpu.VMEM((tm * TOK_SUB, LANES), F32), pltpu.VMEM((tm * TOK_SUB, LANES), F32),
                            pltpu.VMEM((n_rows, ROUTE_W), F32),
                            pltpu.SemaphoreType.DMA((2,)), pltpu.SemaphoreType.DMA(())]),
        out_shape=[jax.ShapeDtypeStruct((n_rows * TOK_SUB, LANES), F32),
                   jax.ShapeDtypeStruct((n_rows, ROUTE_W), F32)],
        compiler_params=_cparams(("arbitrary",)),
        name="moe_sort_rows",
    )(dst_idx, x1, sc2, sh2, gt)


def _expert_ffn(hb, wgu, wdn):
    au = jnp.dot(hb, wgu, preferred_element_type=F32)
    act = (_silu(au[:, :EXPERT_FF]) * au[:, EXPERT_FF:]).astype(BF16)
    return jnp.dot(act, wdn, preferred_element_type=F32)


def _moe_kernel(ea_ref, eb_ref, new_ref, nv_ref, tok_ref, gs_ref, wgu_a_ref, wgu_b_ref, wdn_a_ref, wdn_b_ref,
                o_ref, wgu_sc, wdn_sc, *, tme):
    i = pl.program_id(0)
    valid = i < nv_ref[0]

    @pl.when(valid & (new_ref[i] == 1))
    def _():
        wgu_sc[0] = wgu_a_ref[0].astype(BF16)
        wgu_sc[1] = wgu_b_ref[0].astype(BF16)
        wdn_sc[0] = wdn_a_ref[0].astype(BF16)
        wdn_sc[1] = wdn_b_ref[0].astype(BF16)

    @pl.when(valid)
    def _():
        hb = jnp.concatenate([tok_ref[pl.ds(s, tme, stride=TOK_SUB), :] for s in range(TOK_SUB)],
                             axis=1).astype(BF16)
        au = [jnp.dot(hb, wgu_sc[e], preferred_element_type=F32) for e in range(2)]
        act = [(_silu(a[:, :EXPERT_FF]) * a[:, EXPERT_FF:]).astype(BF16) for a in au]
        y = [jnp.dot(act[e], wdn_sc[e], preferred_element_type=F32) for e in range(2)]
        acc = gs_ref[:, 0:1] * y[0] + gs_ref[:, 1:2] * y[1]
        for s in range(TOK_SUB):
            o_ref[pl.ds(s, tme, stride=TOK_SUB), :] = acc[:, s * LANES:(s + 1) * LANES]

    @pl.when(jnp.logical_not(valid))
    def _():
        o_ref[...] = jnp.zeros_like(o_ref)


def _moe_experts(tok, gs, tile_ea, tile_eb, tile_new, n_valid, wgu, wdn, tme):
    n_tiles = gs.shape[0] // tme
    kern = functools.partial(_moe_kernel, tme=tme)
    gu = lambda pick: pl.BlockSpec((1, D_MODEL, 2 * EXPERT_FF),
                                   lambda i, ea, eb, new, nv: (pick(ea, eb)[i], 0, 0))
    dn = lambda pick: pl.BlockSpec((1, EXPERT_FF, D_MODEL),
                                   lambda i, ea, eb, new, nv: (pick(ea, eb)[i], 0, 0))
    first = lambda ea, eb: ea
    second = lambda ea, eb: eb
    return pl.pallas_call(
        kern,
        grid_spec=pltpu.PrefetchScalarGridSpec(
            num_scalar_prefetch=4,
            grid=(n_tiles,),
            in_specs=[pl.BlockSpec((tme * TOK_SUB, LANES), lambda i, ea, eb, new, nv: (i, 0)),
                      pl.BlockSpec((tme, ROUTE_W), lambda i, ea, eb, new, nv: (i, 0)),
                      gu(first), gu(second), dn(first), dn(second)],
            out_specs=pl.BlockSpec((tme * TOK_SUB, LANES), lambda i, ea, eb, new, nv: (i, 0)),
            scratch_shapes=[pltpu.VMEM((2, D_MODEL, 2 * EXPERT_FF), BF16),
                            pltpu.VMEM((2, EXPERT_FF, D_MODEL), BF16)]),
        out_shape=jax.ShapeDtypeStruct(tok.shape, F32),
        compiler_params=_cparams(("arbitrary",)),
        name="moe_experts",
    )(tile_ea, tile_eb, tile_new, n_valid, tok, gs, wgu, wgu, wdn, wdn)


def _moe_dense_kernel(x1_ref, sc2_ref, sh2_ref, gt_ref, g2_ref, lng_ref, lnb_ref, wgu_ref, wdn_ref, o_ref,
                      hb_sc, acc_sc):
    e = pl.program_id(0)

    @pl.when(e == 0)
    def _():
        hb_sc[...] = (x1_ref[...] * (1.0 + sc2_ref[0]) + sh2_ref[0]).astype(BF16)
        acc_sc[...] = jnp.zeros_like(acc_sc)

    lane = lax.broadcasted_iota(jnp.int32, gt_ref.shape, 1)
    gate = jnp.sum(jnp.where(lane == e, gt_ref[...], 0.0), axis=1, keepdims=True)
    acc_sc[...] += gate * _expert_ffn(hb_sc[...], wgu_ref[0].astype(BF16), wdn_ref[0].astype(BF16))

    @pl.when(e == pl.num_programs(0) - 1)
    def _():
        o_ref[...] = _layer_norm(DN_ALPHA * x1_ref[...] + g2_ref[0] * acc_sc[...], lng_ref[...], lnb_ref[...])


def _moe_dense(tl, x1, sc2, sh2, gt, g2, lng, lnb, wgu, wdn, layer):
    t = tl.t
    full = lambda arr: pl.BlockSpec(arr.shape, lambda e: (0,) * arr.ndim)
    return pl.pallas_call(
        _moe_dense_kernel,
        grid=(N_EXPERTS,),
        in_specs=[full(x1), full(sc2), full(sh2), full(gt), full(g2), full(lng), full(lnb),
                  pl.BlockSpec((1, D_MODEL, 2 * EXPERT_FF), lambda e: (layer * N_EXPERTS + e, 0, 0)),
                  pl.BlockSpec((1, EXPERT_FF, D_MODEL), lambda e: (layer * N_EXPERTS + e, 0, 0))],
        out_specs=pl.BlockSpec((t, D_MODEL), lambda e: (0, 0)),
        out_shape=jax.ShapeDtypeStruct((t, D_MODEL), F32),
        scratch_shapes=[pltpu.VMEM((t, D_MODEL), BF16), pltpu.VMEM((t, D_MODEL), F32)],
        compiler_params=_cparams(("arbitrary",)),
        name="moe_dense_ln",
    )(x1, sc2, sh2, gt, g2, lng, lnb, wgu, wdn)


def _ln_res_kernel(idx_ref, fs_ref, x_ref, g_ref, lng_ref, lnb_ref, o_ref, buf0, buf1, buf2, sem, *, tm):
    i = pl.program_id(0)
    n = pl.num_programs(0)
    bufs = (buf0, buf1, buf2)
    n_buf = len(bufs)

    def row_copy(d, sl, row_off):
        return pltpu.make_async_copy(fs_ref.at[pl.ds(pl.multiple_of(d * TOK_SUB, TOK_SUB), TOK_SUB)],
                                     bufs[sl].at[pl.ds(row_off, TOK_SUB)], sem.at[sl])

    def wait_rows(sl):
        pltpu.make_async_copy(fs_ref.at[pl.ds(0, tm * TOK_SUB)], bufs[sl], sem.at[sl]).wait()

    def step_base(k):
        return jnp.minimum(k, n - 1) * tm

    @pl.when(i == 0)
    def _():
        for sl in range(n_buf - 1):
            base = step_base(sl)

            def body(rr, carry):
                for prio in range(DMA_PRIORITIES):
                    r = rr * DMA_PRIORITIES + prio
                    row_copy(idx_ref[base + r], sl, pl.multiple_of(r * TOK_SUB, TOK_SUB)).start(priority=prio)
                return carry

            lax.fori_loop(0, tm // DMA_PRIORITIES, body, 0, unroll=4)

    ahead = step_base(i + n_buf - 1)

    def step(sl):
        wait_rows(sl)
        f = jnp.concatenate([bufs[sl][pl.ds(s, tm, stride=TOK_SUB), :] for s in range(TOK_SUB)], axis=1)
        o_ref[...] = _layer_norm(DN_ALPHA * x_ref[...] + g_ref[0] * f, lng_ref[...], lnb_ref[...])
        for r in range(tm):
            row_copy(idx_ref[ahead + r], (sl + n_buf - 1) % n_buf, r * TOK_SUB).start(
                priority=r % DMA_PRIORITIES)

    for sl in range(n_buf):
        @pl.when(i % n_buf == sl)
        def _():
            step(sl)

            @pl.when(i == n - 1)
            def _():
                for extra in range(1, n_buf):
                    wait_rows((sl + extra) % n_buf)


def _ln_res(tl, x, fs, dest, g2, lng, lnb):
    tm = tl.tm
    tps = tl.tiles_per_seq
    kern = functools.partial(_ln_res_kernel, tm=tm)
    const = lambda arr: pl.BlockSpec(arr.shape, lambda i, idx: (0,) * arr.ndim)
    return pl.pallas_call(
        kern,
        grid_spec=pltpu.PrefetchScalarGridSpec(
            num_scalar_prefetch=1,
            grid=(tl.n,),
            in_specs=[pl.BlockSpec(memory_space=pl.ANY),
                      pl.BlockSpec((tm, D_MODEL), lambda i, idx: (i, 0)),
                      pl.BlockSpec((1,) + g2.shape[1:], lambda i, idx: (i // tps, 0, 0)),
                      const(lng), const(lnb)],
            out_specs=pl.BlockSpec((tm, D_MODEL), lambda i, idx: (i, 0)),
            scratch_shapes=[pltpu.VMEM((tm * TOK_SUB, LANES), F32)] * 3 + [pltpu.SemaphoreType.DMA((3,))]),
        out_shape=jax.ShapeDtypeStruct((tl.t, D_MODEL), F32),
        compiler_params=_cparams(("arbitrary",)),
        name="ffn_residual_ln",
    )(dest, fs, x, g2, lng, lnb)


def _moe_ffn(tl, x1, sc2, sh2, gt, gi, g2, lng, lnb, wgu, wdn, layer):
    if tl.per_token:
        return _moe_dense(tl, x1, sc2, sh2, gt, g2, lng, lnb, wgu, wdn, layer)
    t = tl.t
    tme = EXPERT_TILE
    n_tiles = t // tme + N_BUCKETS
    n_pad = N_BUCKETS * tme
    assert n_pad % tl.tm == 0
    i32 = jnp.int32
    onehot = (gi[:, None] == jnp.arange(N_BUCKETS, dtype=i32)[None, :]).astype(i32)
    counts = jnp.sum(onehot, axis=0)
    padded = ((counts + tme - 1) // tme) * tme
    ends = jnp.cumsum(padded)
    starts = ends - padded
    dest = jnp.sum(onehot * (jnp.cumsum(onehot, axis=0) - onehot + starts[None, :]), axis=1).astype(i32)
    cpad = jnp.cumsum(padded - counts)
    k = jnp.arange(n_pad, dtype=i32)
    seg_base = jnp.concatenate([starts + counts, ends[-1:]])
    seg_first = jnp.concatenate([jnp.zeros((1,), cpad.dtype), cpad])
    seg_hot = ((k[:, None] >= seg_first[None, :])
               & (k[:, None] < jnp.concatenate([cpad, jnp.full((1,), n_pad, cpad.dtype)])[None, :])).astype(i32)
    pad_rows = (k + jnp.sum(seg_hot * (seg_base - seg_first)[None, :], axis=1)).astype(i32)
    tile_start = jnp.arange(n_tiles, dtype=i32) * tme
    tile_hot = ((tile_start[:, None] >= starts[None, :]) & (tile_start[:, None] < ends[None, :])).astype(i32)
    bucket_ids = jnp.arange(N_BUCKETS, dtype=i32)
    first_expert = layer * N_EXPERTS + (bucket_ids // N_PAIRS) * EXP_PER_GROUP
    pair_a = jnp.asarray(PAIR_A * N_GROUPS, i32)
    pair_b = jnp.asarray(PAIR_B * N_GROUPS, i32)
    last_expert = layer * N_EXPERTS + N_EXPERTS - 1
    in_use = jnp.sum(tile_hot, axis=1)
    tile_ea = (jnp.sum(tile_hot * (first_expert + pair_a)[None, :], axis=1) + (1 - in_use) * last_expert).astype(i32)
    tile_eb = (jnp.sum(tile_hot * (first_expert + pair_b)[None, :], axis=1) + (1 - in_use) * last_expert).astype(i32)
    tile_bucket = jnp.sum(tile_hot * bucket_ids[None, :], axis=1) - (1 - in_use)
    tile_new = (tile_bucket != jnp.concatenate([jnp.full((1,), -2, i32), tile_bucket[:-1]])).astype(i32)
    n_valid = (ends[-1] // tme).astype(i32).reshape(1)
    tok, gs = _moe_sort(tl, x1, sc2, sh2, gt, jnp.concatenate([dest, pad_rows]), n_tiles * tme)
    fs = _moe_experts(tok, gs, tile_ea, tile_eb, tile_new, n_valid, wgu, wdn, tme)
    return _ln_res(tl, x1, fs, dest, g2, lng, lnb)


def _mla_in_kernel(x_ref, sc_ref, sh_ref, wdkv_ref, kvg_ref, wukv_ref, wdq_ref, qg_ref, wuq_ref, cos_ref, sin_ref,
                   lat_ref, kr_ref, q_ref, *kv_refs, qscale, expand):
    x = x_ref[...]
    cos = cos_ref[...]
    sin = sin_ref[...]
    kv = jnp.dot(x.astype(BF16), wdkv_ref[...], preferred_element_type=F32)
    h = (x * (1.0 + sc_ref[0]) + sh_ref[0]).astype(BF16)
    cq = jnp.dot(h, wdq_ref[...], preferred_element_type=F32)
    c = kv[:, :KV_LORA]
    lat = c * lax.rsqrt(jnp.mean(c * c, axis=-1, keepdims=True) + RMS_EPS) * kvg_ref[...]
    qn = (cq * lax.rsqrt(jnp.mean(cq * cq, axis=-1, keepdims=True) + RMS_EPS) * qg_ref[...]).astype(BF16)
    lat_ref[...] = lat
    kr = kv[:, KV_LORA:KV_LORA + LANES] * cos + kv[:, KV_LORA + LANES:] * sin
    kr_ref[...] = kr[:, :MLA_ROPE]
    if expand:
        k_ref, v_ref = kv_refs
        kvx = jnp.dot(lat.astype(BF16), wukv_ref[...], preferred_element_type=F32)
        krb = kr.astype(BF16)
        ones_col = jnp.where(lax.broadcasted_iota(jnp.int32, krb.shape, 1) == 0, 1.0, 0.0).astype(BF16)
        v0 = MLA_HEADS * MLA_NOPE
        for hh in range(MLA_HEADS):
            k_ref[:, hh * HEAD_W:hh * HEAD_W + MLA_NOPE] = kvx[:, hh * MLA_NOPE:(hh + 1) * MLA_NOPE].astype(BF16)
            k_ref[:, hh * HEAD_W + MLA_NOPE:(hh + 1) * HEAD_W] = krb
            v_ref[:, hh * HEAD_W:hh * HEAD_W + MLA_V] = kvx[:, v0 + hh * MLA_V:v0 + (hh + 1) * MLA_V].astype(BF16)
            v_ref[:, hh * HEAD_W + MLA_V:(hh + 1) * HEAD_W] = ones_col
    else:
        (krp_ref,) = kv_refs
        krp_ref[...] = kr
    nw = MLA_HEADS * LANES
    qnope = jnp.dot(qn, wuq_ref[:, :nw], preferred_element_type=F32)
    qpe = jnp.dot(qn, wuq_ref[:, nw:2 * nw], preferred_element_type=F32)
    qpe_sw = jnp.dot(qn, wuq_ref[:, 2 * nw:], preferred_element_type=F32)
    for hh in range(MLA_HEADS):
        cols = slice(hh * LANES, (hh + 1) * LANES)
        q_ref[:, hh * HEAD_W:hh * HEAD_W + LANES] = (qnope[:, cols] * qscale).astype(BF16)
        pe = qpe[:, cols] * cos + qpe_sw[:, cols] * sin
        q_ref[:, hh * HEAD_W + LANES:(hh + 1) * HEAD_W] = (pe * qscale).astype(BF16)


def _mla_in(tl, x, sc, sh, p, j, cos, sin, qscale, expand):
    kern = functools.partial(_mla_in_kernel, qscale=qscale, expand=expand)
    weights = [p["w_dkv_ext"], p["kv_norm_g"], p["w_ukv_r"], p["w_dq"][j], p["q_norm_g"][j], p["w_uq_ext"][j]]
    out_w = [(KV_LORA, F32), (MLA_ROPE, F32), (MLA_HEADS * HEAD_W, BF16)]
    out_w += [(MLA_HEADS * HEAD_W, BF16)] * 2 if expand else [(LANES, F32)]
    return pl.pallas_call(
        kern,
        grid=(tl.n,),
        in_specs=[tl.rows(D_MODEL), tl.mod_spec(sc), tl.mod_spec(sh)] + [_const_spec(w) for w in weights]
                 + [tl.table_spec(cos), tl.table_spec(sin)],
        out_specs=[tl.rows(w) for w, _ in out_w],
        out_shape=[jax.ShapeDtypeStruct((tl.t, w), dt) for w, dt in out_w],
        compiler_params=_cparams(("parallel",)),
        name="mla_in_proj",
    )(x, sc, sh, *weights, cos, sin)


def _attn_kernel(q_ref, qn_ref, k_ref, v_ref, o_ref, m_sc, acc_sc, sa_sc, sb_sc, sc_sc, *, t, n_sub, hp):
    i = pl.program_id(2)
    m_sc[...] = jnp.full(m_sc.shape, NEG_INF, F32)
    acc_sc[...] = jnp.zeros(acc_sc.shape, F32)
    rs = t // n_sub

    def scores(j, s_ref, qr=q_ref):
        for h in range(hp):
            hc = slice(h * HEAD_W, (h + 1) * HEAD_W)
            k = k_ref[0, pl.ds(pl.multiple_of(j * t, t), t), hc]
            s_ref[h] = lax.dot_general(qr[0, :, hc], k, (((1,), (1,)), ((), ())), preferred_element_type=F32)

    def fold(j, s_ref, diagonal):
        for h in range(hp):
            v = v_ref[0, pl.ds(pl.multiple_of(j * t, t), t), h * HEAD_W:(h + 1) * HEAD_W]
            for u in range(n_sub):
                rows = slice(u * rs, (u + 1) * rs)
                s = s_ref[h, rows, :]
                if diagonal:
                    qpos = u * rs + lax.broadcasted_iota(jnp.int32, (rs, 1), 0)
                    kpos = lax.broadcasted_iota(jnp.int32, (1, t), 1)
                    s = jnp.where((kpos >> LOG2_CHUNK) <= (qpos >> LOG2_CHUNK), s, NEG_INF)
                m_old = m_sc[h, rows, :]
                m_new = jnp.maximum(m_old, jnp.max(s, axis=1, keepdims=True))
                alpha = jnp.exp2(m_old - m_new)
                p = jnp.exp2(s - jnp.tile(m_new, (1, t // LANES)))
                pv = jnp.dot(p.astype(BF16), v, preferred_element_type=F32)
                acc_sc[h, rows, :] = jnp.tile(alpha, (1, HEAD_W // LANES)) * acc_sc[h, rows, :] + pv
                m_sc[h, rows, :] = m_new

    def next_tile_scores():
        scores(0, sc_sc, qn_ref)

    @pl.when(i == 0)
    def _():
        scores(0, sc_sc)
        fold(0, sc_sc, True)
        next_tile_scores()

    @pl.when(i == 1)
    def _():
        scores(1, sb_sc)
        fold(0, sc_sc, False)
        next_tile_scores()
        fold(1, sb_sc, True)

    @pl.when(i >= 2)
    def _():
        scores(1, sb_sc)
        fold(0, sc_sc, False)
        scores(2, sa_sc)
        fold(1, sb_sc, False)

        def fold_pair(j):
            scores(j + 1, sb_sc)
            fold(j, sa_sc, False)
            scores(j + 2, sa_sc)
            fold(j + 1, sb_sc, False)

        def quad_step(qq, carry):
            fold_pair(2 + 4 * qq)
            fold_pair(4 + 4 * qq)
            return carry

        n_pair = i // 2 - 1
        lax.fori_loop(0, n_pair // 2, quad_step, 0)

        @pl.when(n_pair % 2 == 1)
        def _():
            fold_pair(2 * n_pair)

        @pl.when(i % 2 == 0)
        def _():
            next_tile_scores()
            fold(i, sa_sc, True)

        @pl.when(i % 2 == 1)
        def _():
            scores(i, sb_sc)
            fold(i - 1, sa_sc, False)
            next_tile_scores()
            fold(i, sb_sc, True)

    for h in range(hp):
        acc = acc_sc[h]
        o_ref[0, :, h * MLA_V:(h + 1) * MLA_V] = (acc[:, :MLA_V] / acc[:, MLA_V:MLA_V + 1]).astype(BF16)


def _attention(qc, kc, vv, b, l, t, hp):
    assert l % t == 0 and t % CHUNK == 0
    n_sub = 2 if t % (2 * LANES) == 0 else 1
    nq = l // t
    kern = functools.partial(_attn_kernel, t=t, n_sub=n_sub, hp=hp)
    scores_buf = pltpu.VMEM((hp, t, t), F32)
    return pl.pallas_call(
        kern,
        grid=(b, MLA_HEADS // hp, nq),
        in_specs=[pl.BlockSpec((1, t, hp * HEAD_W), lambda bi, h, i: (bi, i, h)),
                  pl.BlockSpec((1, t, hp * HEAD_W), lambda bi, h, i: (bi, jnp.minimum(i + 1, nq - 1), h)),
                  pl.BlockSpec((1, l, hp * HEAD_W), lambda bi, h, i: (bi, 0, h)),
                  pl.BlockSpec((1, l, hp * HEAD_W), lambda bi, h, i: (bi, 0, h))],
        out_specs=pl.BlockSpec((1, t, hp * MLA_V), lambda bi, h, i: (bi, i, h)),
        out_shape=jax.ShapeDtypeStruct((b, l, MLA_HEADS * MLA_V), BF16),
        scratch_shapes=[pltpu.VMEM((hp, t, LANES), F32), pltpu.VMEM((hp, t, HEAD_W), F32),
                        scores_buf, scores_buf, scores_buf],
        compiler_params=_cparams(("parallel", "parallel", "arbitrary"), ATTN_VMEM_LIMIT),
        name="mla_attention",
    )(qc.reshape(b, l, MLA_HEADS * HEAD_W), qc.reshape(b, l, MLA_HEADS * HEAD_W),
      kc.reshape(b, l, MLA_HEADS * HEAD_W), vv.reshape(b, l, MLA_HEADS * HEAD_W))


def _decode_attn_kernel(q_ref, lnew_ref, knew_ref, lpast_ref, kpast_ref, wukt_ref, wuv_ref, o_ref, qa_sc,
                        *, l, pos0, past):
    nope_w = KV_LORA
    for h in range(MLA_HEADS):
        rows = slice(h * l, (h + 1) * l)
        qn = q_ref[0, :, h * HEAD_W:h * HEAD_W + MLA_NOPE]
        qa_sc[rows, :nope_w] = jnp.dot(qn, wukt_ref[h], preferred_element_type=F32).astype(BF16)
        qa_sc[rows, nope_w:] = q_ref[0, :, h * HEAD_W + MLA_NOPE:(h + 1) * HEAD_W]
    q_lat = qa_sc[:, :nope_w]
    q_pe = qa_sc[:, nope_w:nope_w + MLA_ROPE]
    nt = (((1,), (1,)), ((), ()))
    lp = lpast_ref[0].astype(BF16)
    kp = kpast_ref[0].astype(BF16)
    ln = lnew_ref[0].astype(BF16)
    kn = knew_ref[0][:, :MLA_ROPE].astype(BF16)
    s_past = (lax.dot_general(q_lat, lp, nt, preferred_element_type=F32)
              + lax.dot_general(q_pe, kp, nt, preferred_element_type=F32))
    s_new = (lax.dot_general(q_lat, ln, nt, preferred_element_type=F32)
             + lax.dot_general(q_pe, kn, nt, preferred_element_type=F32))
    n_rows = MLA_HEADS * l
    qpos = pos0 + lax.rem(lax.broadcasted_iota(jnp.int32, (n_rows, 1), 0), l)
    kpos_new = pos0 + lax.broadcasted_iota(jnp.int32, (1, l), 1)
    s_new = jnp.where((kpos_new >> LOG2_CHUNK) <= (qpos >> LOG2_CHUNK), s_new, NEG_INF)
    kpos_past = lax.broadcasted_iota(jnp.int32, (1, past), 1)
    s_past = jnp.where((kpos_past >> LOG2_CHUNK) <= (qpos >> LOG2_CHUNK), s_past, NEG_INF)
    m = jnp.maximum(jnp.max(s_past, axis=1, keepdims=True), jnp.max(s_new, axis=1, keepdims=True))
    p_past = jnp.exp2(s_past - m)
    p_new = jnp.exp2(s_new - m)
    denom = jnp.sum(p_past, axis=1, keepdims=True) + jnp.sum(p_new, axis=1, keepdims=True)
    ctx = (jnp.dot(p_past.astype(BF16), lp, preferred_element_type=F32)
           + jnp.dot(p_new.astype(BF16), ln, preferred_element_type=F32)) / denom
    ctx = ctx.astype(BF16)
    for h in range(MLA_HEADS):
        o_ref[0, :, h * MLA_V:(h + 1) * MLA_V] = jnp.dot(
            ctx[h * l:(h + 1) * l], wuv_ref[h], preferred_element_type=F32).astype(BF16)


def _decode_attention(qc, latent_new, krp_new, past_latent, past_k_rope, w_uk_t, w_uv, b, l, pos0):
    past = past_latent.shape[1]
    kern = functools.partial(_decode_attn_kernel, l=l, pos0=pos0, past=past)
    per_b = lambda shape: pl.BlockSpec((1,) + shape, lambda bi: (bi, 0, 0))
    return pl.pallas_call(
        kern,
        grid=(b,),
        in_specs=[per_b((l, MLA_HEADS * HEAD_W)), per_b((l, KV_LORA)), per_b((l, LANES)),
                  per_b((past, KV_LORA)), per_b((past, MLA_ROPE)), _const_spec(w_uk_t), _const_spec(w_uv)],
        out_specs=per_b((l, MLA_HEADS * MLA_V)),
        out_shape=jax.ShapeDtypeStruct((b, l, MLA_HEADS * MLA_V), BF16),
        scratch_shapes=[pltpu.VMEM((MLA_HEADS * l, KV_LORA + LANES), BF16)],
        compiler_params=_cparams(("parallel",)),
        name="mla_decode_attention",
    )(qc.reshape(b, l, MLA_HEADS * HEAD_W), latent_new.reshape(b, l, KV_LORA), krp_new.reshape(b, l, LANES),
      past_latent, past_k_rope, w_uk_t, w_uv)


def _rope_tables(pos0, length, half):
    inv = ROPE_THETA ** (-np.arange(half, dtype=np.float64) / half)
    ang = (pos0 + np.arange(length, dtype=np.float64))[:, None] * inv[None, :]
    return np.cos(ang), np.sin(ang)


def _mla_rope_tables(pos0, length):
    cos, sin = _rope_tables(pos0, length, MLA_ROPE // 2)
    z = np.zeros((length, LANES - MLA_ROPE))
    return (jnp.asarray(np.concatenate([cos, cos, z], axis=1), F32),
            jnp.asarray(np.concatenate([-sin, sin, z], axis=1), F32))


def _swap_halves(w):
    half = w.shape[-1] // 2
    return jnp.concatenate([w[..., half:], w[..., :half]], axis=-1)


def _pad_lanes(w):
    return jnp.pad(w, [(0, 0)] * (w.ndim - 1) + [(0, LANES - w.shape[-1])])


def _prep_weights(w_ret_in, w_ret_out, w_dq, q_norm_g, w_uq, w_mla_out, w_dkv, kv_norm_g, w_ukv,
                  w_route_group, b_route_group, w_route_expert, b_route_expert,
                  w_expert_gate_up, w_expert_down):
    p = {}
    p["w_ret_in"] = [w_ret_in[i].astype(BF16) for i in range(N_A)]
    p["w_ret_out"] = [w_ret_out[i].astype(BF16) for i in range(N_A)]
    p["w_dq"] = [w_dq[j].astype(BF16) for j in range(DEPTH - N_A)]
    p["q_norm_g"] = [q_norm_g[j][None, :] for j in range(DEPTH - N_A)]
    w_uq_ext = []
    for j in range(DEPTH - N_A):
        wq = w_uq[j].reshape(Q_LORA, MLA_HEADS, MLA_NOPE + MLA_ROPE)
        nope = wq[:, :, :MLA_NOPE].reshape(Q_LORA, MLA_HEADS * LANES)
        pe = wq[:, :, MLA_NOPE:]
        w_uq_ext.append(jnp.concatenate(
            [nope, _pad_lanes(pe).reshape(Q_LORA, MLA_HEADS * LANES),
             _pad_lanes(_swap_halves(pe)).reshape(Q_LORA, MLA_HEADS * LANES)], axis=1).astype(BF16))
    p["w_uq_ext"] = w_uq_ext
    p["w_mla_out"] = [w_mla_out[j].astype(BF16) for j in range(DEPTH - N_A)]
    kr = w_dkv[:, KV_LORA:]
    p["w_dkv_ext"] = jnp.concatenate([w_dkv[:, :KV_LORA], _pad_lanes(kr), _pad_lanes(_swap_halves(kr))],
                                     axis=1).astype(BF16)
    p["kv_norm_g"] = kv_norm_g[None, :]
    w_ukv4 = w_ukv.reshape(KV_LORA, MLA_HEADS, 2, MLA_NOPE)
    p["w_ukv_r"] = w_ukv4.transpose(0, 2, 1, 3).reshape(KV_LORA, 2 * MLA_HEADS * MLA_NOPE).astype(BF16)
    p["w_uk_t"] = w_ukv4[:, :, 0, :].transpose(1, 2, 0).astype(BF16)
    p["w_uv"] = w_ukv4[:, :, 1, :].transpose(1, 0, 2).astype(BF16)
    wr = jnp.concatenate([w_route_group, w_route_expert], axis=-1)
    wr = jnp.pad(wr, ((0, 0), (0, 0), (0, ROUTE_W - wr.shape[-1])))
    wr_hi = wr.astype(BF16)
    wr_lo = (wr - wr_hi.astype(F32)).astype(BF16)
    br = jnp.concatenate([b_route_group, b_route_expert], axis=-1)
    br = jnp.pad(br, ((0, 0), (0, ROUTE_W - br.shape[-1])))
    wr_cat = jnp.concatenate([wr_hi, wr_lo], axis=-1)
    p["wr"] = [wr_cat[l] for l in range(DEPTH)]
    p["br"] = [br[l][None, :] for l in range(DEPTH)]
    p["w_gu"] = w_expert_gate_up.reshape(DEPTH * N_EXPERTS, D_MODEL, 2 * EXPERT_FF)
    p["w_dn"] = w_expert_down.reshape(DEPTH * N_EXPERTS, EXPERT_FF, D_MODEL)
    return p


def _trunk(x3, mod, pos0, ret_s0, past_latent, past_k_rope, ret_chunk, ret_block, ln_g, ln_b, p):
    b, l, _ = x3.shape
    tl = _Tiles(b, l)
    x = x3.reshape(tl.t, D_MODEL)
    new_ret = []
    latent_new = k_rope_new = None
    for layer in range(DEPTH):
        sh1, sc1, g1, sh2, sc2, g2 = [tl.mod(m) for m in jnp.split(mod[layer], 6, axis=-1)]
        lng = [ln_g[layer, s][None, :] for s in range(2)]
        lnb = [ln_b[layer, s][None, :] for s in range(2)]
        if layer < N_A:
            cos, sin = [jnp.asarray(tab, F32) for tab in _rope_tables(pos0, l, RET_DK // 2)]
            q, k, v, sg = _ret_in(tl, x, sc1, sh1, p["w_ret_in"][layer], tl.table(cos), tl.table(sin))
            a, s_new = _retention(b, l, q, k, v, sg, ret_s0[layer], ret_chunk, ret_block)
            a = a.reshape(tl.t, RET_HEADS * RET_DV)
            new_ret.append(s_new)
            w_out = p["w_ret_out"][layer]
            name = "ret_out_ln_route"
        else:
            j = layer - N_A
            cos, sin = _mla_rope_tables(pos0, l)
            cos, sin = tl.table(cos), tl.table(sin)
            assert layer == N_A == DEPTH - 1
            qscale = float((MLA_NOPE + MLA_ROPE) ** -0.5 * math.log2(math.e))
            if past_latent is None:
                assert pos0 == 0
                latent_new, k_rope_new, qc, kc, vv = _mla_in(tl, x, sc1, sh1, p, j, cos, sin, qscale, True)
                a = _attention(qc, kc, vv, b, l, ATTN_TILE, 2)
            else:
                latent_new, k_rope_new, qc, krp = _mla_in(tl, x, sc1, sh1, p, j, cos, sin, qscale, False)
                a = _decode_attention(qc, latent_new, krp, past_latent, past_k_rope, p["w_uk_t"], p["w_uv"],
                                      b, l, pos0)
            a = a.reshape(tl.t, MLA_HEADS * MLA_V)
            w_out = p["w_mla_out"][j]
            name = "mla_out_ln_route"
        x1, gt, gi = _mix_out(tl, a, w_out, x, g1, lng[0], lnb[0], sc2, sh2,
                              p["wr"][layer], p["br"][layer], name)
        x = _moe_ffn(tl, x1, sc2, sh2, gt, gi.reshape(tl.t), g2, lng[1], lnb[1],
                     p["w_gu"], p["w_dn"], layer)
    return (x.reshape(b, l, D_MODEL), jnp.stack(new_ret), latent_new.reshape(b, l, KV_LORA),
            k_rope_new.reshape(b, l, MLA_ROPE))


def kernel(x_prompt, x_sample, state_retention, cache_kv_latent, cache_k_rope, c_prompt, c_sample,
           w_ada, b_ada, ln_g, ln_b, w_ret_in, w_ret_out, w_dq, q_norm_g, w_uq, w_mla_out,
           w_dkv, kv_norm_g, w_ukv, w_route_group, b_route_group, w_route_expert, b_route_expert,
           w_expert_gate_up, w_expert_down):
    bp, lp, _ = x_prompt.shape
    bs, ls, _ = x_sample.shape
    p = _prep_weights(w_ret_in, w_ret_out, w_dq, q_norm_g, w_uq, w_mla_out, w_dkv, kv_norm_g, w_ukv,
                      w_route_group, b_route_group, w_route_expert, b_route_expert,
                      w_expert_gate_up, w_expert_down)
    n_seq = bp + bs
    n_rows = -(-n_seq // 8) * 8
    c_all = jnp.concatenate([c_prompt, c_sample, jnp.zeros((n_rows - n_seq, D_MODEL), F32)], axis=0)
    mod = _ada(c_all, w_ada, b_ada)
    s0_p = jnp.zeros((N_A, bp, RET_HEADS, RET_DK, RET_DV), F32)
    ret_chunk_p = RET_CHUNK if lp % RET_CHUNK == 0 else CHUNK
    y_p, st_p, lat_p, kr_p = _trunk(x_prompt, mod[:, :bp], 0, s0_p, None, None,
                                    ret_chunk_p, max(ret_chunk_p, min(lp, RET_BLOCK)), ln_g, ln_b, p)
    past = cache_kv_latent.shape[1]
    y_s, st_s, lat_s, kr_s = _trunk(x_sample, mod[:, bp:n_seq], past, state_retention,
                                    cache_kv_latent, cache_k_rope, ls, ls, ln_g, ln_b, p)
    return (y_p, y_s, st_p, st_s, lat_p, kr_p, lat_s, kr_s)
```
